```python
import math
import jax, jax.numpy as jnp
from jax import lax
import numpy as np

D_MODEL = 1024
BATCH = 8
SEQ = 2048
DEPTH = 4

HEAD_DIM = 64
MOBA_HEADS = 4
MOBA_BLOCK = 256
MOBA_TOPK = 3
MOBA_Q_CHUNK = 64
DIFF_HEADS = 4
DIFF_QK_DIM = 32
DIFF_V_DIM = 2 * DIFF_QK_DIM
DIFF_Q_BLOCK = 128
SWA_Q_HEADS = 8
SWA_KV_HEADS = 2
SWA_WINDOW = 128
SWA_BLOCK = 128
D_FF = 2816

N_ALIBI = MOBA_HEADS + DIFF_HEADS + SWA_Q_HEADS
RMS_EPS = 1e-6
NEG = -1e30

A_Q = MOBA_HEADS * HEAD_DIM
A_K = MOBA_HEADS * HEAD_DIM
A_V = MOBA_HEADS * HEAD_DIM
B_Q = DIFF_HEADS * 2 * DIFF_QK_DIM
B_K = DIFF_HEADS * 2 * DIFF_QK_DIM
B_V = DIFF_HEADS * DIFF_V_DIM
C_Q = SWA_Q_HEADS * HEAD_DIM
C_KV = SWA_KV_HEADS * HEAD_DIM
PROJ_WIDTH = A_Q + A_K + A_V + B_Q + B_K + B_V + C_Q + 2 * C_KV
MIX_WIDTH = A_V + B_V + C_Q

kernel_name = "hybrid_moba_diff_swa_macaron"


def _split_points():
    widths = [A_Q, A_K, A_V, B_Q, B_K, B_V, C_Q, C_KV, C_KV]
    return [int(v) for v in np.cumsum(widths)[:-1]]


def _alibi_slopes():
    n = N_ALIBI
    return jnp.asarray(2.0 ** (-8.0 * (np.arange(n, dtype=np.float32) + 1.0) / n), dtype=jnp.float32)


def rmsnorm(x, g):
    xf = x.astype(jnp.float32)
    y = xf * lax.rsqrt(jnp.mean(xf * xf, axis=-1, keepdims=True) + RMS_EPS)
    return (y * g.astype(jnp.float32)).astype(x.dtype)


def swiglu(h, w_gate, w_up, w_down):
    return (jax.nn.silu(h @ w_gate) * (h @ w_up)) @ w_down


def moba_attention(q, k, v, slopes):
    B, S, H, Dh = q.shape
    L = MOBA_BLOCK
    nb = -(-S // L)
    Sp = nb * L
    padw = ((0, 0), (0, Sp - S), (0, 0), (0, 0))
    qf = jnp.pad(q.astype(jnp.float32), padw).transpose(0, 2, 1, 3)
    kf = jnp.pad(k.astype(jnp.float32), padw).transpose(0, 2, 1, 3)
    vf = jnp.pad(v.astype(jnp.float32), padw).transpose(0, 2, 1, 3)
    kb = kf.reshape(B, H, nb, L, Dh)
    vb = vf.reshape(B, H, nb, L, Dh)
    kmean = jnp.mean(kb, axis=3)
    gate = jnp.einsum('bhtd,bhnd->bhtn', qf, kmean)
    qblk = jnp.arange(Sp) // L
    past = jnp.arange(nb)[None, :] < qblk[:, None]
    gate = jnp.where(past[None, None], gate, NEG)
    n_sel = min(MOBA_TOPK, nb)
    _, idx = lax.top_k(gate, n_sel)
    valid = idx < qblk[None, None, :, None]

    QC = MOBA_Q_CHUNK
    nq = Sp // QC
    q_c = qf.reshape(B, H, nq, QC, Dh).transpose(2, 0, 1, 3, 4)
    idx_c = idx.reshape(B, H, nq, QC, n_sel).transpose(2, 0, 1, 3, 4)
    val_c = valid.reshape(B, H, nq, QC, n_sel).transpose(2, 0, 1, 3, 4)
    starts = jnp.arange(nq, dtype=jnp.int32) * QC
    gather = jax.vmap(jax.vmap(lambda blocks, ii: blocks[ii]))
    scale = Dh ** -0.5
    kpos_in = jnp.arange(L)

    def chunk(args):
        qc, ic, vc, t0 = args
        tq = t0 + jnp.arange(QC)
        own = t0 // L
        k_own = lax.dynamic_index_in_dim(kb, own, axis=2, keepdims=False)
        v_own = lax.dynamic_index_in_dim(vb, own, axis=2, keepdims=False)
        ks = gather(kb, ic)
        vs = gather(vb, ic)
        qs = qc * scale
        s_sel = jnp.einsum('bhqd,bhqjld->bhqjl', qs, ks)
        pos_sel = ic[..., None] * L + kpos_in
        dist_sel = (tq[None, None, :, None, None] - pos_sel).astype(jnp.float32)
        s_sel = s_sel - slopes[None, :, None, None, None] * dist_sel
        s_sel = jnp.where(vc[..., None], s_sel, NEG)
        s_own = jnp.einsum('bhqd,bhld->bhql', qs, k_own)
        dist_own = (tq[:, None] - (own * L + kpos_in)[None, :]).astype(jnp.float32)
        s_own = jnp.where(dist_own >= 0, s_own - slopes[None, :, None, None] * dist_own, NEG)
        s = jnp.concatenate([s_sel.reshape(B, H, QC, n_sel * L), s_own], axis=-1)
        p = jax.nn.softmax(s, axis=-1)
        p_sel = p[..., :n_sel * L].reshape(B, H, QC, n_sel, L)
        p_own = p[..., n_sel * L:]
        return (jnp.einsum('bhqjl,bhqjld->bhqd', p_sel, vs)
                + jnp.einsum('bhql,bhld->bhqd', p_own, v_own))

    out = lax.map(chunk, (q_c, idx_c, val_c, starts))
    out = out.transpose(1, 0, 3, 2, 4).reshape(B, Sp, H, Dh)
    return out[:, :S]


def diff_attention(q, k, v, slopes, lam, subln_g, lam_init):
    B, S, H, _, dq = q.shape
    dv = v.shape[-1]
    QB = DIFF_Q_BLOCK
    nq = S // QB
    qf = q.astype(jnp.float32) * (dq ** -0.5)
    kf = k.astype(jnp.float32)
    vf = v.astype(jnp.float32)
    q_c = qf.reshape(B, nq, QB, H, 2, dq).transpose(1, 0, 2, 3, 4, 5)
    starts = jnp.arange(nq, dtype=jnp.int32) * QB
    kpos = jnp.arange(S)

    def blk(args):
        qc, t0 = args
        tq = t0 + jnp.arange(QB)
        dist = (tq[:, None] - kpos[None, :]).astype(jnp.float32)
        bias = -slopes[:, None, None] * dist
        s = jnp.einsum('bqhmd,bshmd->bhmqs', qc, kf) + bias[None, :, None]
        s = jnp.where(dist >= 0, s, NEG)
        p = jax.nn.softmax(s, axis=-1)
        w = p[:, :, 0] - lam * p[:, :, 1]
        return jnp.einsum('bhqs,bshd->bqhd', w, vf)

    o = lax.map(blk, (q_c, starts))
    o = o.transpose(1, 0, 2, 3, 4).reshape(B, S, H, dv)
    return rmsnorm(o, subln_g) * (1.0 - lam_init)


def swa_attention(q, k, v, slopes, sinks):
    B, S, Hq, Dh = q.shape
    Hkv = k.shape[2]
    G = Hq // Hkv
    SB = SWA_BLOCK
    nb = S // SB
    qf = q.astype(jnp.float32).reshape(B, nb, SB, Hkv, G, Dh) * (Dh ** -0.5)
    kb = k.astype(jnp.float32).reshape(B, nb, SB, Hkv, Dh)
    vb = v.astype(jnp.float32).reshape(B, nb, SB, Hkv, Dh)
    zeros = jnp.zeros_like(kb[:, :1])
    kk = jnp.concatenate([jnp.concatenate([zeros, kb[:, :-1]], axis=1), kb], axis=2)
    vv = jnp.concatenate([jnp.concatenate([zeros, vb[:, :-1]], axis=1), vb], axis=2)
    qi = jnp.arange(SB)
    ki = jnp.arange(2 * SB) - SB
    dist = qi[:, None] - ki[None, :]
    in_win = (dist >= 0) & (dist < SWA_WINDOW)
    k_exists = (jnp.arange(nb)[:, None] * SB + ki[None, :]) >= 0
    mask = in_win[None] & k_exists[:, None, :]
    s = jnp.einsum('bnqkgd,bnskd->bkgnqs', qf, kk)
    bias = -slopes.reshape(Hkv, G)[:, :, None, None, None] * dist.astype(jnp.float32)
    s = jnp.where(mask, s + bias, NEG)
    sink = jnp.broadcast_to(sinks.astype(jnp.float32).reshape(Hkv, G)[None, :, :, None, None, None],
                            s.shape[:-1] + (1,))
    p = jax.nn.softmax(jnp.concatenate([s, sink], axis=-1), axis=-1)[..., :-1]
    o = jnp.einsum('bkgnqs,bnskd->bnqkgd', p, vv)
    return o.reshape(B, S, Hq, Dh)


def setup_inputs(seed: int = 0) -> dict:
    key = jax.random.key(seed)
    ks = jax.random.split(key, 20)
    f32 = jnp.float32

    def w(k, shape, fan_in):
        return jax.random.normal(k, shape, f32) * (fan_in ** -0.5)

    def gain(k, shape):
        return 1.0 + 0.05 * jax.random.normal(k, shape, f32)

    return {
        "x": jax.random.normal(ks[0], (BATCH, SEQ, D_MODEL), f32),
        "norm_ffn1": gain(ks[1], (DEPTH, D_MODEL)),
        "w1_gate": w(ks[2], (DEPTH, D_MODEL, D_FF), D_MODEL),
        "w1_up": w(ks[3], (DEPTH, D_MODEL, D_FF), D_MODEL),
        "w1_down": w(ks[4], (DEPTH, D_FF, D_MODEL), D_FF),
        "norm_mix": gain(ks[5], (DEPTH, D_MODEL)),
        "w_in": w(ks[6], (DEPTH, D_MODEL, PROJ_WIDTH), D_MODEL),
        "lam_q1": 0.1 * jax.random.normal(ks[7], (DEPTH, DIFF_QK_DIM), f32),
        "lam_k1": 0.1 * jax.random.normal(ks[8], (DEPTH, DIFF_QK_DIM), f32),
        "lam_q2": 0.1 * jax.random.normal(ks[9], (DEPTH, DIFF_QK_DIM), f32),
        "lam_k2": 0.1 * jax.random.normal(ks[10], (DEPTH, DIFF_QK_DIM), f32),
        "diff_subln": gain(ks[11], (DEPTH, DIFF_V_DIM)),
        "sinks": 0.5 * jax.random.normal(ks[12], (DEPTH, SWA_Q_HEADS), f32),
        "w_out": w(ks[13], (DEPTH, MIX_WIDTH, D_MODEL), MIX_WIDTH),
        "norm_ffn2": gain(ks[14], (DEPTH, D_MODEL)),
        "w2_gate": w(ks[15], (DEPTH, D_MODEL, D_FF), D_MODEL),
        "w2_up": w(ks[16], (DEPTH, D_MODEL, D_FF), D_MODEL),
        "w2_down": w(ks[17], (DEPTH, D_FF, D_MODEL), D_FF),
        "final_norm": gain(ks[18], (D_MODEL,)),
    }


def reference(x, norm_ffn1, w1_gate, w1_up, w1_down, norm_mix, w_in, lam_q1, lam_k1, lam_q2, lam_k2,
              diff_subln, sinks, w_out, norm_ffn2, w2_gate, w2_up, w2_down, final_norm):
    B, S, _ = x.shape
    slopes = _alibi_slopes()
    slopes_c = slopes[:SWA_Q_HEADS]
    slopes_b = slopes[SWA_Q_HEADS:SWA_Q_HEADS + DIFF_HEADS]
    slopes_a = slopes[SWA_Q_HEADS + DIFF_HEADS:]
    splits = _split_points()
    for l in range(DEPTH):
        x = x + 0.5 * swiglu(rmsnorm(x, norm_ffn1[l]), w1_gate[l], w1_up[l], w1_down[l])
        h = rmsnorm(x, norm_mix[l])
        p = h @ w_in[l]
        aq, ak, av, bq, bk, bv, cq, ck, cv = jnp.split(p, splits, axis=-1)
        o_a = moba_attention(aq.reshape(B, S, MOBA_HEADS, HEAD_DIM),
                             ak.reshape(B, S, MOBA_HEADS, HEAD_DIM),
                             av.reshape(B, S, MOBA_HEADS, HEAD_DIM), slopes_a)
        lam_init = 0.8 - 0.6 * math.exp(-0.3 * l)
        lam = (jnp.exp(jnp.sum(lam_q1[l].astype(jnp.float32) * lam_k1[l].astype(jnp.float32)))
               - jnp.exp(jnp.sum(lam_q2[l].astype(jnp.float32) * lam_k2[l].astype(jnp.float32)))
               + lam_init)
        o_b = diff_attention(bq.reshape(B, S, DIFF_HEADS, 2, DIFF_QK_DIM),
                             bk.reshape(B, S, DIFF_HEADS, 2, DIFF_QK_DIM),
                             bv.reshape(B, S, DIFF_HEADS, DIFF_V_DIM),
                             slopes_b, lam, diff_subln[l], lam_init)
        o_c = swa_attention(cq.reshape(B, S, SWA_Q_HEADS, HEAD_DIM),
                            ck.reshape(B, S, SWA_KV_HEADS, HEAD_DIM),
                            cv.reshape(B, S, SWA_KV_HEADS, HEAD_DIM), slopes_c, sinks[l])
        mix = jnp.concatenate([o_a.reshape(B, S, A_V), o_b.reshape(B, S, B_V),
                               o_c.reshape(B, S, C_Q)], axis=-1).astype(x.dtype)
        x = x + mix @ w_out[l]
        x = x + 0.5 * swiglu(rmsnorm(x, norm_ffn2[l]), w2_gate[l], w2_up[l], w2_down[l])
    return rmsnorm(x, final_norm)
```

```python
import functools
import math

import numpy as np
import jax
import jax.numpy as jnp
from jax import lax
from jax.experimental import pallas as pl
from jax.experimental.pallas import tpu as pltpu

F32 = jnp.float32
BF16 = jnp.bfloat16

D_MODEL = 1024
D_FF = 2816
HEAD_DIM = 64
MOBA_HEADS = 4
MOBA_BLOCK = 256
MOBA_TOPK = 3
DIFF_HEADS = 4
DIFF_QK_DIM = 32
SWA_Q_HEADS = 8
SWA_KV_HEADS = 2
SWA_GROUP = SWA_Q_HEADS // SWA_KV_HEADS
SWA_WINDOW = 128
SWA_BLOCK = 128
N_ALIBI = MOBA_HEADS + DIFF_HEADS + SWA_Q_HEADS
RMS_EPS = 1e-6
NEG = -1e30

N_Q_HEADS = MOBA_HEADS + DIFF_HEADS + SWA_Q_HEADS
N_K_HEADS = MOBA_HEADS + DIFF_HEADS + SWA_KV_HEADS
OPERAND_WIDTH = 128
AUG_FLAG_ROW = HEAD_DIM
AUG_BIAS_ROW = HEAD_DIM + 8
Q_TILE = 256
FFN_ROWS = 512
VMEM_LIMIT = 56 * 1024 * 1024


def _alibi_slopes():
    n = N_ALIBI
    return 2.0 ** (-8.0 * (np.arange(n, dtype=np.float32) + 1.0) / n)


def _rmsnorm(x, g):
    ms = jnp.mean(x * x, axis=-1, keepdims=True)
    return (x * lax.rsqrt(ms + RMS_EPS)) * g


def _swiglu_residual(x, g, wg_ref, wu_ref, wd_ref):
    h = _rmsnorm(x, g).astype(BF16)
    gate = jnp.dot(h, wg_ref[...], preferred_element_type=F32)
    up = jnp.dot(h, wu_ref[...], preferred_element_type=F32)
    act = (gate * jax.nn.sigmoid(gate) * up).astype(BF16)
    return x + 0.5 * jnp.dot(act, wd_ref[...], preferred_element_type=F32)


def _resident(shape):
    zeros = (0,) * len(shape)
    return pl.BlockSpec(shape, lambda *_: zeros, pipeline_mode=pl.Buffered(1))


def _ffn_kernel(x_ref, g_ref, wg_ref, wu_ref, wd_ref, o_ref):
    o_ref[...] = _swiglu_residual(x_ref[...], g_ref[...], wg_ref, wu_ref, wd_ref)


def _ffn_call(x2, g, wg, wu, wd):
    t = x2.shape[0]
    return pl.pallas_call(
        _ffn_kernel,
        grid=(t // FFN_ROWS,),
        in_specs=[
            pl.BlockSpec((FFN_ROWS, D_MODEL), lambda i: (i, 0)),
            _resident((1, D_MODEL)),
            _resident((D_MODEL, D_FF)),
            _resident((D_MODEL, D_FF)),
            _resident((D_FF, D_MODEL)),
        ],
        out_specs=pl.BlockSpec((FFN_ROWS, D_MODEL), lambda i: (i, 0)),
        out_shape=jax.ShapeDtypeStruct(x2.shape, F32),
        compiler_params=pltpu.CompilerParams(
            dimension_semantics=("arbitrary",), vmem_limit_bytes=VMEM_LIMIT),
        name="ffn1",
    )(x2, g, wg, wu, wd)


def _split_hi_lo(x):
    hi = x.astype(BF16)
    lo = (x - hi.astype(F32)).astype(BF16)
    return hi, lo


def _proj_kernel(x_ref, g_ref, wk_ref, wqv_ref, augk_ref, augq_ref,
                 qa_ref, qb_ref, qc_ref, ka_ref, kb_ref, kc_ref, va_ref, vb_ref, vc_ref,
                 kmean_ref):
    i = pl.program_id(1)

    @pl.when(i == 0)
    def _():
        kmean_ref[...] = jnp.zeros_like(kmean_ref)

    h = _rmsnorm(x_ref[0], g_ref[...]).astype(BF16)
    pk = jnp.dot(h, wk_ref[...], preferred_element_type=F32)
    pqv = lax.dot_general(wqv_ref[...], h, (((1,), (1,)), ((), ())),
                          preferred_element_type=F32)
    augk = augk_ref[...]

    def k_slab(s):
        return pk[:, s * OPERAND_WIDTH:(s + 1) * OPERAND_WIDTH]

    for hh in range(MOBA_HEADS):
        ka_ref[0, hh, 0] = (k_slab(hh) + augk).astype(BF16)
        kb_ref[0, hh, 0] = (k_slab(MOBA_HEADS + hh) + augk).astype(BF16)
    for hh in range(SWA_KV_HEADS):
        slab = (k_slab(MOBA_HEADS + DIFF_HEADS + hh) + augk).astype(BF16)
        kc_ref[0, hh, 0] = slab[:SWA_BLOCK]
        kc_ref[0, hh, 1] = slab[SWA_BLOCK:]

    v0 = N_Q_HEADS * HEAD_DIM

    def v_slab(s):
        return pqv[v0 + s * HEAD_DIM:v0 + (s + 1) * HEAD_DIM].astype(BF16)

    for hh in range(MOBA_HEADS):
        va_ref[0, hh, 0] = v_slab(hh)
        vb_ref[0, hh, 0] = v_slab(MOBA_HEADS + hh)
    for hh in range(SWA_KV_HEADS):
        slab = v_slab(MOBA_HEADS + DIFF_HEADS + hh)
        vc_ref[0, hh, 0] = slab[:, :SWA_BLOCK]
        vc_ref[0, hh, 1] = slab[:, SWA_BLOCK:]

    def q_rows(s):
        return pqv[s * HEAD_DIM:(s + 1) * HEAD_DIM]

    t = x_ref.shape[1]
    n_blocks = kmean_ref.shape[1]
    jrow = lax.broadcasted_iota(jnp.int32, (n_blocks, t), 0)
    moba_scale = HEAD_DIM ** -0.5
    for hh in range(MOBA_HEADS):
        kmean_ref[hh, pl.ds(i, 1), :] = jnp.mean(k_slab(hh), axis=0, keepdims=True)
        km_hi, km_lo = _split_hi_lo(kmean_ref[hh][:, :HEAD_DIM])
        qf = q_rows(hh)
        q_hi, q_lo = _split_hi_lo(qf)
        gate = (jnp.dot(km_hi, q_hi, preferred_element_type=F32)
                + jnp.dot(km_hi, q_lo, preferred_element_type=F32)
                + jnp.dot(km_lo, q_hi, preferred_element_type=F32))
        rank = jnp.zeros((n_blocks, t), F32)
        for m in range(n_blocks):
            gm = gate[m:m + 1, :]
            beats = (gm > gate) | ((gm == gate) & (m < jrow))
            rank = rank + jnp.where(beats, jnp.where(m < i, 1.0, 0.0), 0.0)
        chosen = ((rank < MOBA_TOPK) & (jrow < i)) | (jrow == i)
        flags = jnp.where(chosen, 0.0, NEG)
        aug = augq_ref[hh] + jnp.concatenate(
            [flags, jnp.zeros((HEAD_DIM - n_blocks, t), F32)], axis=0)
        qa_ref[0, hh, :HEAD_DIM, :] = (qf * moba_scale).astype(BF16)
        qa_ref[0, hh, HEAD_DIM:, :] = aug.astype(BF16)
    diff_scale = DIFF_QK_DIM ** -0.5
    for hh in range(DIFF_HEADS):
        s = MOBA_HEADS + hh
        qb_ref[0, hh, :HEAD_DIM, :] = (q_rows(s) * diff_scale).astype(BF16)
        qb_ref[0, hh, HEAD_DIM:, :] = augq_ref[s].astype(BF16)
    swa_scale = HEAD_DIM ** -0.5
    for hh in range(SWA_Q_HEADS):
        s = MOBA_HEADS + DIFF_HEADS + hh
        qc_ref[0, hh, :HEAD_DIM, :] = (q_rows(s) * swa_scale).astype(BF16)
        qc_ref[0, hh, HEAD_DIM:, :] = augq_ref[s].astype(BF16)


def _proj_call(x, g, wk, wqv, augk, augq):
    b, s, _ = x.shape
    nt = s // Q_TILE
    nsb = s // SWA_BLOCK
    per_tile = Q_TILE // SWA_BLOCK
    out_shape = (
        jax.ShapeDtypeStruct((b, MOBA_HEADS, OPERAND_WIDTH, s), BF16),
        jax.ShapeDtypeStruct((b, DIFF_HEADS, OPERAND_WIDTH, s), BF16),
        jax.ShapeDtypeStruct((b, SWA_Q_HEADS, OPERAND_WIDTH, s), BF16),
        jax.ShapeDtypeStruct((b, MOBA_HEADS, nt, Q_TILE, OPERAND_WIDTH), BF16),
        jax.ShapeDtypeStruct((b, DIFF_HEADS, nt, Q_TILE, OPERAND_WIDTH), BF16),
        jax.ShapeDtypeStruct((b, SWA_KV_HEADS, nsb, SWA_BLOCK, OPERAND_WIDTH), BF16),
        jax.ShapeDtypeStruct((b, MOBA_HEADS, nt, HEAD_DIM, Q_TILE), BF16),
        jax.ShapeDtypeStruct((b, DIFF_HEADS, nt, HEAD_DIM, Q_TILE), BF16),
        jax.ShapeDtypeStruct((b, SWA_KV_HEADS, nsb, HEAD_DIM, SWA_BLOCK), BF16),
    )
    out_specs = (
        pl.BlockSpec((1, MOBA_HEADS, OPERAND_WIDTH, Q_TILE), lambda bi, i: (bi, 0, 0, i)),
        pl.BlockSpec((1, DIFF_HEADS, OPERAND_WIDTH, Q_TILE), lambda bi, i: (bi, 0, 0, i)),
        pl.BlockSpec((1, SWA_Q_HEADS, OPERAND_WIDTH, Q_TILE), lambda bi, i: (bi, 0, 0, i)),
        pl.BlockSpec((1, MOBA_HEADS, 1, Q_TILE, OPERAND_WIDTH), lambda bi, i: (bi, 0, i, 0, 0)),
        pl.BlockSpec((1, DIFF_HEADS, 1, Q_TILE, OPERAND_WIDTH), lambda bi, i: (bi, 0, i, 0, 0)),
        pl.BlockSpec((1, SWA_KV_HEADS, per_tile, SWA_BLOCK, OPERAND_WIDTH),
                     lambda bi, i: (bi, 0, i, 0, 0)),
        pl.BlockSpec((1, MOBA_HEADS, 1, HEAD_DIM, Q_TILE), lambda bi, i: (bi, 0, i, 0, 0)),
        pl.BlockSpec((1, DIFF_HEADS, 1, HEAD_DIM, Q_TILE), lambda bi, i: (bi, 0, i, 0, 0)),
        pl.BlockSpec((1, SWA_KV_HEADS, per_tile, HEAD_DIM, SWA_BLOCK),
                     lambda bi, i: (bi, 0, i, 0, 0)),
    )
    return pl.pallas_call(
        _proj_kernel,
        grid=(b, nt),
        in_specs=[
            pl.BlockSpec((1, Q_TILE, D_MODEL), lambda bi, i: (bi, i, 0)),
            _resident((1, D_MODEL)),
            _resident(wk.shape),
            _resident(wqv.shape),
            pl.BlockSpec((Q_TILE, OPERAND_WIDTH), lambda bi, i: (i, 0)),
            pl.BlockSpec((N_Q_HEADS, HEAD_DIM, Q_TILE), lambda bi, i: (0, 0, i)),
        ],
        out_specs=out_specs,
        out_shape=out_shape,
        scratch_shapes=[pltpu.VMEM((MOBA_HEADS, nt, OPERAND_WIDTH), F32)],
        compiler_params=pltpu.CompilerParams(
            dimension_semantics=("arbitrary", "arbitrary"), vmem_limit_bytes=VMEM_LIMIT),
        name="proj",
    )(x, g, wk, wqv, augk, augq)


def _flash_first(s, v):
    m = jnp.max(s, axis=0, keepdims=True)
    p = jnp.exp(s - m)
    l = jnp.sum(p, axis=0, keepdims=True)
    acc = jnp.dot(v, p.astype(BF16), preferred_element_type=F32)
    return m, l, acc


def _flash_next(state, s, v):
    m, l, acc = state
    m_new = jnp.maximum(m, jnp.max(s, axis=0, keepdims=True))
    alpha = jnp.exp(m - m_new)
    p = jnp.exp(s - m_new)
    l = alpha * l + jnp.sum(p, axis=0, keepdims=True)
    acc = alpha * acc + jnp.dot(v, p.astype(BF16), preferred_element_type=F32)
    return m_new, l, acc


def _causal_tile(s):
    kr = lax.broadcasted_iota(jnp.int32, s.shape, 0)
    qc = lax.broadcasted_iota(jnp.int32, s.shape, 1)
    return jnp.where(kr <= qc, s, NEG)


def _moba_kernel(q_ref, k_ref, v_ref, o_ref):
    i = pl.program_id(2)
    q = q_ref[0, 0]

    def scores(j):
        return jnp.dot(k_ref[0, 0, j], q, preferred_element_type=F32)

    state = _flash_first(_causal_tile(scores(i)), v_ref[0, 0, i])

    def body(j, st):
        return _flash_next(st, scores(j), v_ref[0, 0, j])

    m, l, acc = lax.fori_loop(0, i, body, state)
    o_ref[0] = (acc / l).astype(o_ref.dtype)


def _moba_call(qa, ka, va):
    b, nh, _, s = qa.shape
    nt = s // Q_TILE
    return pl.pallas_call(
        _moba_kernel,
        grid=(b, nh, nt),
        in_specs=[
            pl.BlockSpec((1, 1, OPERAND_WIDTH, Q_TILE), lambda bi, h, i: (bi, h, 0, i)),
            pl.BlockSpec((1, 1, nt, Q_TILE, OPERAND_WIDTH), lambda bi, h, i: (bi, h, 0, 0, 0)),
            pl.BlockSpec((1, 1, nt, HEAD_DIM, Q_TILE), lambda bi, h, i: (bi, h, 0, 0, 0)),
        ],
        out_specs=pl.BlockSpec((1, HEAD_DIM, Q_TILE), lambda bi, h, i: (bi, h, i)),
        out_shape=jax.ShapeDtypeStruct((b, nh * HEAD_DIM, s), BF16),
        compiler_params=pltpu.CompilerParams(
            dimension_semantics=("arbitrary", "arbitrary", "arbitrary"),
            vmem_limit_bytes=VMEM_LIMIT),
        name="moba",
    )(qa, ka, va)


def _diff_kernel(q_ref, k_ref, v_ref, lq1_ref, lk1_ref, lq2_ref, lk2_ref, linit_ref, g_ref, o_ref):
    i = pl.program_id(2)
    q = q_ref[0, 0]
    row = lax.broadcasted_iota(jnp.int32, q.shape, 0)
    zero = jnp.zeros_like(q)
    q1 = jnp.where((row < DIFF_QK_DIM) | (row >= 2 * DIFF_QK_DIM), q, zero)
    q2 = jnp.where(row >= DIFF_QK_DIM, q, zero)

    def scores(j):
        k = k_ref[0, 0, j]
        return (jnp.dot(k, q1, preferred_element_type=F32),
                jnp.dot(k, q2, preferred_element_type=F32))

    s1, s2 = scores(i)
    v = v_ref[0, 0, i]
    st1 = _flash_first(_causal_tile(s1), v)
    st2 = _flash_first(_causal_tile(s2), v)

    def body(j, st):
        a, c = st
        s1, s2 = scores(j)
        v = v_ref[0, 0, j]
        return _flash_next(a, s1, v), _flash_next(c, s2, v)

    (m1, l1, acc1), (m2, l2, acc2) = lax.fori_loop(0, i, body, (st1, st2))
    lam_init = linit_ref[...]
    lam = (jnp.exp(jnp.sum(lq1_ref[...] * lk1_ref[...], axis=-1, keepdims=True))
           - jnp.exp(jnp.sum(lq2_ref[...] * lk2_ref[...], axis=-1, keepdims=True))
           + lam_init)
    o = acc1 / l1 - lam * (acc2 / l2)
    ms = jnp.mean(o * o, axis=0, keepdims=True)
    y = (o * lax.rsqrt(ms + RMS_EPS)) * g_ref[...]
    o_ref[0] = (y * (1.0 - lam_init)).astype(o_ref.dtype)


def _diff_call(qb, kb, vb, lq1, lk1, lq2, lk2, linit, g_col):
    b, nh, _, s = qb.shape
    nt = s // Q_TILE
    small = lambda shape: pl.BlockSpec(shape, lambda bi, h, i: (0,) * len(shape))
    return pl.pallas_call(
        _diff_kernel,
        grid=(b, nh, nt),
        in_specs=[
            pl.BlockSpec((1, 1, OPERAND_WIDTH, Q_TILE), lambda bi, h, i: (bi, h, 0, i)),
            pl.BlockSpec((1, 1, nt, Q_TILE, OPERAND_WIDTH), lambda bi, h, i: (bi, h, 0, 0, 0)),
            pl.BlockSpec((1, 1, nt, HEAD_DIM, Q_TILE), lambda bi, h, i: (bi, h, 0, 0, 0)),
            small((1, DIFF_QK_DIM)), small((1, DIFF_QK_DIM)),
            small((1, DIFF_QK_DIM)), small((1, DIFF_QK_DIM)),
            small((1, 1)), small((HEAD_DIM, 1)),
        ],
        out_specs=pl.BlockSpec((1, HEAD_DIM, Q_TILE), lambda bi, h, i: (bi, h, i)),
        out_shape=jax.ShapeDtypeStruct((b, nh * HEAD_DIM, s), BF16),
        compiler_params=pltpu.CompilerParams(
            dimension_semantics=("arbitrary", "arbitrary", "arbitrary"),
            vmem_limit_bytes=VMEM_LIMIT),
        name="diffattn",
    )(qb, kb, vb, lq1, lk1, lq2, lk2, linit, g_col)


def _swa_kernel(q_ref, k_ref, v_ref, sink_ref, o_ref):
    n = pl.program_id(2)
    prev = jnp.maximum(n - 1, 0)
    q = jnp.concatenate([q_ref[0, g] for g in range(SWA_GROUP)], axis=1)
    k = jnp.concatenate([k_ref[0, 0, prev], k_ref[0, 0, n]], axis=0)
    v = jnp.concatenate([v_ref[0, 0, prev], v_ref[0, 0, n]], axis=1)
    s = jnp.dot(k, q, preferred_element_type=F32)
    kr = lax.broadcasted_iota(jnp.int32, (2 * SWA_BLOCK, SWA_BLOCK), 0)
    qc = lax.broadcasted_iota(jnp.int32, (2 * SWA_BLOCK, SWA_BLOCK), 1)
    dist = qc + SWA_BLOCK - kr
    first_row = jnp.where(n > 0, 0, SWA_BLOCK)
    in_win = (dist >= 0) & (dist < SWA_WINDOW) & (kr >= first_row)
    band = jnp.where(in_win, 0.0, NEG)
    s = s + jnp.concatenate([band] * SWA_GROUP, axis=1)
    sink = sink_ref[0]
    m = jnp.maximum(jnp.max(s, axis=0, keepdims=True), sink)
    p = jnp.exp(s - m)
    l = jnp.sum(p, axis=0, keepdims=True) + jnp.exp(sink - m)
    o = jnp.dot(v, p.astype(BF16), preferred_element_type=F32) / l
    for g in range(SWA_GROUP):
        o_ref[0, g * HEAD_DIM:(g + 1) * HEAD_DIM, :] = (
            o[:, g * SWA_BLOCK:(g + 1) * SWA_BLOCK].astype(o_ref.dtype))


def _swa_call(qc, kc, vc, sink_rows):
    b, nq, _, s = qc.shape
    nkv = kc.shape[1]
    nsb = s // SWA_BLOCK
    return pl.pallas_call(
        _swa_kernel,
        grid=(b, nkv, nsb),
        in_specs=[
            pl.BlockSpec((1, SWA_GROUP, OPERAND_WIDTH, SWA_BLOCK), lambda bi, h, n: (bi, h, 0, n)),
            pl.BlockSpec((1, 1, nsb, SWA_BLOCK, OPERAND_WIDTH), lambda bi, h, n: (bi, h, 0, 0, 0)),
            pl.BlockSpec((1, 1, nsb, HEAD_DIM, SWA_BLOCK), lambda bi, h, n: (bi, h, 0, 0, 0)),
            pl.BlockSpec((1, 1, SWA_GROUP * SWA_BLOCK), lambda bi, h, n: (h, 0, 0)),
        ],
        out_specs=pl.BlockSpec((1, SWA_GROUP * HEAD_DIM, SWA_BLOCK), lambda bi, h, n: (bi, h, n)),
        out_shape=jax.ShapeDtypeStruct((b, nq * HEAD_DIM, s), BF16),
        compiler_params=pltpu.CompilerParams(
            dimension_semantics=("arbitrary", "arbitrary", "arbitrary"),
            vmem_limit_bytes=VMEM_LIMIT),
        name="swa",
    )(qc, kc, vc, sink_rows)


def _mix_ffn_kernel(x_ref, ma_ref, mb_ref, mc_ref, woa_ref, wob_ref, woc_ref,
                    g_ref, wg_ref, wu_ref, wd_ref, gf_ref, o_ref, *, final):
    tdims = (((0,), (0,)), ((), ()))
    x = x_ref[0]
    x = x + lax.dot_general(ma_ref[0], woa_ref[...], tdims, preferred_element_type=F32)
    x = x + lax.dot_general(mb_ref[0], wob_ref[...], tdims, preferred_element_type=F32)
    x = x + lax.dot_general(mc_ref[0], woc_ref[...], tdims, preferred_element_type=F32)
    x = _swiglu_residual(x, g_ref[...], wg_ref, wu_ref, wd_ref)
    if final:
        x = _rmsnorm(x, gf_ref[...])
    o_ref[0] = x


def _mix_ffn_call(x, ma, mb, mc, woa, wob, woc, g, wg, wu, wd, gf, final):
    b, s, _ = x.shape
    rows = FFN_ROWS
    return pl.pallas_call(
        functools.partial(_mix_ffn_kernel, final=final),
        grid=(b, s // rows),
        in_specs=[
            pl.BlockSpec((1, rows, D_MODEL), lambda bi, i: (bi, i, 0)),
            pl.BlockSpec((1, ma.shape[1], rows), lambda bi, i: (bi, 0, i)),
            pl.BlockSpec((1, mb.shape[1], rows), lambda bi, i: (bi, 0, i)),
            pl.BlockSpec((1, mc.shape[1], rows), lambda bi, i: (bi, 0, i)),
            _resident(woa.shape), _resident(wob.shape), _resident(woc.shape),
            _resident((1, D_MODEL)),
            _resident((D_MODEL, D_FF)), _resident((D_MODEL, D_FF)), _resident((D_FF, D_MODEL)),
            _resident((1, D_MODEL)),
        ],
        out_specs=pl.BlockSpec((1, rows, D_MODEL), lambda bi, i: (bi, i, 0)),
        out_shape=jax.ShapeDtypeStruct(x.shape, F32),
        compiler_params=pltpu.CompilerParams(
            dimension_semantics=("arbitrary", "arbitrary"), vmem_limit_bytes=VMEM_LIMIT),
        name="mix_ffn2_final" if final else "mix_ffn2",
    )(x, ma, mb, mc, woa, wob, woc, g, wg, wu, wd, gf)


def _split3(x):
    x1 = x.astype(BF16).astype(F32)
    r = x - x1
    x2 = r.astype(BF16).astype(F32)
    x3 = (r - x2).astype(BF16).astype(F32)
    return [x1, x2, x3]


def _bias_rows_kernel(raw_ref, o_ref):
    raw = raw_ref[0]
    pieces = _split3(raw[0:1]) + 2 * _split3(raw[1:2])
    s = raw.shape[1]
    row = lax.broadcasted_iota(jnp.int32, (16, s), 0)
    blk = jnp.zeros((16, s), F32)
    for r, piece in enumerate(pieces):
        blk = jnp.where(row == r, piece, blk)
    lead = AUG_BIAS_ROW - HEAD_DIM
    o_ref[0] = jnp.concatenate(
        [jnp.zeros((lead, s), F32), blk, jnp.zeros((HEAD_DIM - lead - 16, s), F32)], axis=0)


def _operand_tables(s):
    slopes = _alibi_slopes()
    head_slopes = np.concatenate([
        slopes[SWA_Q_HEADS + DIFF_HEADS:],
        slopes[SWA_Q_HEADS:SWA_Q_HEADS + DIFF_HEADS],
        slopes[:SWA_Q_HEADS],
    ]).astype(np.float32)
    pos = jnp.arange(s, dtype=F32)
    sl = jnp.asarray(head_slopes)[:, None]
    raw = jnp.zeros((N_Q_HEADS, 8, s), F32)
    raw = raw.at[:, 0, :].set(-sl * pos[None, :])
    raw = raw.at[:, 1, :].set(jnp.broadcast_to(sl, (N_Q_HEADS, s)))
    augq = pl.pallas_call(
        _bias_rows_kernel,
        grid=(N_Q_HEADS,),
        in_specs=[pl.BlockSpec((1, 8, s), lambda h: (h, 0, 0))],
        out_specs=pl.BlockSpec((1, HEAD_DIM, s), lambda h: (h, 0, 0)),
        out_shape=jax.ShapeDtypeStruct((N_Q_HEADS, HEAD_DIM, s), F32),
        name="bias_rows",
    )(raw)
    blk = jnp.arange(s, dtype=jnp.int32) // MOBA_BLOCK
    onehot = (blk[:, None] == jnp.arange(8, dtype=jnp.int32)[None, :]).astype(F32)
    ones = jnp.ones((s, 3), F32)
    base = jnp.broadcast_to((blk * MOBA_BLOCK).astype(F32)[:, None], (s, 3))
    off = jnp.broadcast_to((jnp.arange(s, dtype=jnp.int32) % MOBA_BLOCK).astype(F32)[:, None], (s, 3))
    augk = jnp.concatenate([
        jnp.zeros((s, HEAD_DIM), F32), onehot, ones, base, off,
        jnp.zeros((s, OPERAND_WIDTH - HEAD_DIM - 17), F32)], axis=1)
    return augq, augk


def _key_weight(w_in_l):
    a_k0 = MOBA_HEADS * HEAD_DIM
    b_k0 = 3 * MOBA_HEADS * HEAD_DIM + DIFF_HEADS * HEAD_DIM
    c_k0 = 3 * MOBA_HEADS * HEAD_DIM + 3 * DIFF_HEADS * HEAD_DIM + SWA_Q_HEADS * HEAD_DIM
    cols = jnp.concatenate([
        w_in_l[:, a_k0:a_k0 + MOBA_HEADS * HEAD_DIM],
        w_in_l[:, b_k0:b_k0 + DIFF_HEADS * HEAD_DIM],
        w_in_l[:, c_k0:c_k0 + SWA_KV_HEADS * HEAD_DIM]], axis=1)
    cols = cols.reshape(D_MODEL, N_K_HEADS, HEAD_DIM)
    cols = jnp.pad(cols, ((0, 0), (0, 0), (0, OPERAND_WIDTH - HEAD_DIM)))
    return cols.reshape(D_MODEL, N_K_HEADS * OPERAND_WIDTH).astype(BF16)


def _query_value_weight_t(w_in_l):
    a0 = 0
    b0 = 3 * MOBA_HEADS * HEAD_DIM
    c0 = b0 + 3 * DIFF_HEADS * HEAD_DIM
    aw = MOBA_HEADS * HEAD_DIM
    bw = DIFF_HEADS * HEAD_DIM
    cqw = SWA_Q_HEADS * HEAD_DIM
    ckw = SWA_KV_HEADS * HEAD_DIM
    cols = jnp.concatenate([
        w_in_l[:, a0:a0 + aw], w_in_l[:, b0:b0 + bw], w_in_l[:, c0:c0 + cqw],
        w_in_l[:, a0 + 2 * aw:a0 + 3 * aw], w_in_l[:, b0 + 2 * bw:b0 + 3 * bw],
        w_in_l[:, c0 + cqw + ckw:c0 + cqw + 2 * ckw]], axis=1)
    return cols.T.astype(BF16)


def kernel(x, norm_ffn1, w1_gate, w1_up, w1_down, norm_mix, w_in, lam_q1, lam_k1, lam_q2, lam_k2,
           diff_subln, sinks, w_out, norm_ffn2, w2_gate, w2_up, w2_down, final_norm):
    b, s, d = x.shape
    depth = w_in.shape[0]
    assert d == D_MODEL and s % Q_TILE == 0 and (b * s) % FFN_ROWS == 0
    augq, augk = _operand_tables(s)
    a_w = MOBA_HEADS * HEAD_DIM
    b_w = DIFF_HEADS * HEAD_DIM
    gf = final_norm.reshape(1, d)
    for l in range(depth):
        x2 = _ffn_call(x.reshape(b * s, d), norm_ffn1[l].reshape(1, d),
                       w1_gate[l].astype(BF16), w1_up[l].astype(BF16), w1_down[l].astype(BF16))
        x = x2.reshape(b, s, d)
        qa, qb, qc, ka, kb, kc, va, vb, vc = _proj_call(
            x, norm_mix[l].reshape(1, d), _key_weight(w_in[l]), _query_value_weight_t(w_in[l]),
            augk, augq)
        ma = _moba_call(qa, ka, va)
        lam_init = 0.8 - 0.6 * math.exp(-0.3 * l)
        mb = _diff_call(qb, kb, vb,
                        lam_q1[l].reshape(1, -1), lam_k1[l].reshape(1, -1),
                        lam_q2[l].reshape(1, -1), lam_k2[l].reshape(1, -1),
                        jnp.full((1, 1), lam_init, F32), diff_subln[l].reshape(-1, 1))
        sink_rows = jnp.repeat(sinks[l].reshape(SWA_KV_HEADS, 1, SWA_GROUP), SWA_BLOCK, axis=2)
        mc = _swa_call(qc, kc, vc, sink_rows)
        wo = w_out[l].astype(BF16)
        x = _mix_ffn_call(x, ma, mb, mc, wo[:a_w], wo[a_w:a_w + b_w], wo[a_w + b_w:],
                          norm_ffn2[l].reshape(1, d),
                          w2_gate[l].astype(BF16), w2_up[l].astype(BF16), w2_down[l].astype(BF16),
                          gf, final=(l == depth - 1))
    return x
```

```python
import functools
import math

import numpy as np
import jax
import jax.numpy as jnp
from jax import lax
from jax.experimental import pallas as pl
from jax.experimental.pallas import tpu as pltpu

F32 = jnp.float32
BF16 = jnp.bfloat16

D_MODEL = 1024
D_FF = 2816
HEAD_DIM = 64
MOBA_HEADS = 4
MOBA_BLOCK = 256
MOBA_TOPK = 3
DIFF_HEADS = 4
DIFF_QK_DIM = 32
SWA_Q_HEADS = 8
SWA_KV_HEADS = 2
SWA_GROUP = SWA_Q_HEADS // SWA_KV_HEADS
SWA_WINDOW = 128
SWA_BLOCK = 128
N_ALIBI = MOBA_HEADS + DIFF_HEADS + SWA_Q_HEADS
RMS_EPS = 1e-6
NEG = -1e30

N_Q_HEADS = MOBA_HEADS + DIFF_HEADS + SWA_Q_HEADS
N_K_HEADS = MOBA_HEADS + DIFF_HEADS + SWA_KV_HEADS
OPERAND_WIDTH = 128
AUG_FLAG_ROW = HEAD_DIM
AUG_BIAS_ROW = HEAD_DIM + 8
Q_TILE = 256
FFN_ROWS = 512
VMEM_LIMIT = 56 * 1024 * 1024


def _alibi_slopes():
    n = N_ALIBI
    return 2.0 ** (-8.0 * (np.arange(n, dtype=np.float32) + 1.0) / n)


def _rmsnorm(x, g):
    ms = jnp.mean(x * x, axis=-1, keepdims=True)
    return (x * lax.rsqrt(ms + RMS_EPS)) * g


def _swiglu_residual(x, g, wg_ref, wu_ref, wd_ref):
    h = _rmsnorm(x, g).astype(BF16)
    gate = jnp.dot(h, wg_ref[...], preferred_element_type=F32)
    up = jnp.dot(h, wu_ref[...], preferred_element_type=F32)
    act = (gate * jax.nn.sigmoid(gate) * up).astype(BF16)
    return x + 0.5 * jnp.dot(act, wd_ref[...], preferred_element_type=F32)


def _resident(shape):
    zeros = (0,) * len(shape)
    return pl.BlockSpec(shape, lambda *_: zeros, pipeline_mode=pl.Buffered(1))


def _ffn_kernel(x_ref, g_ref, wg_ref, wu_ref, wd_ref, o_ref):
    o_ref[...] = _swiglu_residual(x_ref[...], g_ref[...], wg_ref, wu_ref, wd_ref)


def _ffn_call(x2, g, wg, wu, wd):
    t = x2.shape[0]
    return pl.pallas_call(
        _ffn_kernel,
        grid=(t // FFN_ROWS,),
        in_specs=[
            pl.BlockSpec((FFN_ROWS, D_MODEL), lambda i: (i, 0)),
            _resident((1, D_MODEL)),
            _resident((D_MODEL, D_FF)),
            _resident((D_MODEL, D_FF)),
            _resident((D_FF, D_MODEL)),
        ],
        out_specs=pl.BlockSpec((FFN_ROWS, D_MODEL), lambda i: (i, 0)),
        out_shape=jax.ShapeDtypeStruct(x2.shape, F32),
        compiler_params=pltpu.CompilerParams(
            dimension_semantics=("arbitrary",), vmem_limit_bytes=VMEM_LIMIT),
        name="ffn1",
    )(x2, g, wg, wu, wd)


def _split_hi_lo(x):
    hi = x.astype(BF16)
    lo = (x - hi.astype(F32)).astype(BF16)
    return hi, lo


def _proj_kernel(x_ref, g_ref, wk_ref, wqv_ref, augk_ref, augq_ref,
                 qa_ref, qb_ref, qc_ref, ka_ref, kb_ref, kc_ref, va_ref, vb_ref, vc_ref,
                 kmean_ref):
    i = pl.program_id(1)

    @pl.when(i == 0)
    def _():
        kmean_ref[...] = jnp.zeros_like(kmean_ref)

    h = _rmsnorm(x_ref[0], g_ref[...]).astype(BF16)
    pk = jnp.dot(h, wk_ref[...], preferred_element_type=F32)
    pqv = lax.dot_general(wqv_ref[...], h, (((1,), (1,)), ((), ())),
                          preferred_element_type=F32)
    augk = augk_ref[...]

    def k_slab(s):
        return pk[:, s * OPERAND_WIDTH:(s + 1) * OPERAND_WIDTH]

    for hh in range(MOBA_HEADS):
        ka_ref[0, hh, 0] = (k_slab(hh) + augk).astype(BF16)
        kb_ref[0, hh, 0] = (k_slab(MOBA_HEADS + hh) + augk).astype(BF16)
    for hh in range(SWA_KV_HEADS):
        slab = (k_slab(MOBA_HEADS + DIFF_HEADS + hh) + augk).astype(BF16)
        kc_ref[0, hh, 0] = slab[:SWA_BLOCK]
        kc_ref[0, hh, 1] = slab[SWA_BLOCK:]

    v0 = N_Q_HEADS * HEAD_DIM

    def v_slab(s):
        return pqv[v0 + s * HEAD_DIM:v0 + (s + 1) * HEAD_DIM].astype(BF16)

    for hh in range(MOBA_HEADS):
        va_ref[0, hh, 0] = v_slab(hh)
        vb_ref[0, hh, 0] = v_slab(MOBA_HEADS + hh)
    for hh in range(SWA_KV_HEADS):
        slab = v_slab(MOBA_HEADS + DIFF_HEADS + hh)
        vc_ref[0, hh, 0] = slab[:, :SWA_BLOCK]
        vc_ref[0, hh, 1] = slab[:, SWA_BLOCK:]

    def q_rows(s):
        return pqv[s * HEAD_DIM:(s + 1) * HEAD_DIM]

    t = x_ref.shape[1]
    n_blocks = kmean_ref.shape[1]
    jrow = lax.broadcasted_iota(jnp.int32, (n_blocks, t), 0)
    moba_scale = HEAD_DIM ** -0.5
    for hh in range(MOBA_HEADS):
        kmean_ref[hh, pl.ds(i, 1), :] = jnp.mean(k_slab(hh), axis=0, keepdims=True)
        km_hi, km_lo = _split_hi_lo(kmean_ref[hh][:, :HEAD_DIM])
        qf = q_rows(hh)
        q_hi, q_lo = _split_hi_lo(qf)
        gate = (jnp.dot(km_hi, q_hi, preferred_element_type=F32)
                + jnp.dot(km_hi, q_lo, preferred_element_type=F32)
                + jnp.dot(km_lo, q_hi, preferred_element_type=F32))
        rank = jnp.zeros((n_blocks, t), F32)
        for m in range(n_blocks):
            gm = gate[m:m + 1, :]
            beats = (gm > gate) | ((gm == gate) & (m < jrow))
            rank = rank + jnp.where(beats, jnp.where(m < i, 1.0, 0.0), 0.0)
        chosen = ((rank < MOBA_TOPK) & (jrow < i)) | (jrow == i)
        flags = jnp.where(chosen, 0.0, NEG)
        aug = augq_ref[hh] + jnp.concatenate(
            [flags, jnp.zeros((HEAD_DIM - n_blocks, t), F32)], axis=0)
        qa_ref[0, hh, :HEAD_DIM, :] = (qf * moba_scale).astype(BF16)
        qa_ref[0, hh, HEAD_DIM:, :] = aug.astype(BF16)
    diff_scale = DIFF_QK_DIM ** -0.5
    for hh in range(DIFF_HEADS):
        s = MOBA_HEADS + hh
        qb_ref[0, hh, :HEAD_DIM, :] = (q_rows(s) * diff_scale).astype(BF16)
        qb_ref[0, hh, HEAD_DIM:, :] = augq_ref[s].astype(BF16)
    swa_scale = HEAD_DIM ** -0.5
    for hh in range(SWA_Q_HEADS):
        s = MOBA_HEADS + DIFF_HEADS + hh
        qc_ref[0, hh, :HEAD_DIM, :] = (q_rows(s) * swa_scale).astype(BF16)
        qc_ref[0, hh, HEAD_DIM:, :] = augq_ref[s].astype(BF16)


def _proj_call(x, g, wk, wqv, augk, augq):
    b, s, _ = x.shape
    nt = s // Q_TILE
    nsb = s // SWA_BLOCK
    per_tile = Q_TILE // SWA_BLOCK
    out_shape = (
        jax.ShapeDtypeStruct((b, MOBA_HEADS, OPERAND_WIDTH, s), BF16),
        jax.ShapeDtypeStruct((b, DIFF_HEADS, OPERAND_WIDTH, s), BF16),
        jax.ShapeDtypeStruct((b, SWA_Q_HEADS, OPERAND_WIDTH, s), BF16),
        jax.ShapeDtypeStruct((b, MOBA_HEADS, nt, Q_TILE, OPERAND_WIDTH), BF16),
        jax.ShapeDtypeStruct((b, DIFF_HEADS, nt, Q_TILE, OPERAND_WIDTH), BF16),
        jax.ShapeDtypeStruct((b, SWA_KV_HEADS, nsb, SWA_BLOCK, OPERAND_WIDTH), BF16),
        jax.ShapeDtypeStruct((b, MOBA_HEADS, nt, HEAD_DIM, Q_TILE), BF16),
        jax.ShapeDtypeStruct((b, DIFF_HEADS, nt, HEAD_DIM, Q_TILE), BF16),
        jax.ShapeDtypeStruct((b, SWA_KV_HEADS, nsb, HEAD_DIM, SWA_BLOCK), BF16),
    )
    out_specs = (
        pl.BlockSpec((1, MOBA_HEADS, OPERAND_WIDTH, Q_TILE), lambda bi, i: (bi, 0, 0, i)),
        pl.BlockSpec((1, DIFF_HEADS, OPERAND_WIDTH, Q_TILE), lambda bi, i: (bi, 0, 0, i)),
        pl.BlockSpec((1, SWA_Q_HEADS, OPERAND_WIDTH, Q_TILE), lambda bi, i: (bi, 0, 0, i)),
        pl.BlockSpec((1, MOBA_HEADS, 1, Q_TILE, OPERAND_WIDTH), lambda bi, i: (bi, 0, i, 0, 0)),
        pl.BlockSpec((1, DIFF_HEADS, 1, Q_TILE, OPERAND_WIDTH), lambda bi, i: (bi, 0, i, 0, 0)),
        pl.BlockSpec((1, SWA_KV_HEADS, per_tile, SWA_BLOCK, OPERAND_WIDTH),
                     lambda bi, i: (bi, 0, i, 0, 0)),
        pl.BlockSpec((1, MOBA_HEADS, 1, HEAD_DIM, Q_TILE), lambda bi, i: (bi, 0, i, 0, 0)),
        pl.BlockSpec((1, DIFF_HEADS, 1, HEAD_DIM, Q_TILE), lambda bi, i: (bi, 0, i, 0, 0)),
        pl.BlockSpec((1, SWA_KV_HEADS, per_tile, HEAD_DIM, SWA_BLOCK),
                     lambda bi, i: (bi, 0, i, 0, 0)),
    )
    return pl.pallas_call(
        _proj_kernel,
        grid=(b, nt),
        in_specs=[
            pl.BlockSpec((1, Q_TILE, D_MODEL), lambda bi, i: (bi, i, 0)),
            _resident((1, D_MODEL)),
            _resident(wk.shape),
            _resident(wqv.shape),
            pl.BlockSpec((Q_TILE, OPERAND_WIDTH), lambda bi, i: (i, 0)),
            pl.BlockSpec((N_Q_HEADS, HEAD_DIM, Q_TILE), lambda bi, i: (0, 0, i)),
        ],
        out_specs=out_specs,
        out_shape=out_shape,
        scratch_shapes=[pltpu.VMEM((MOBA_HEADS, nt, OPERAND_WIDTH), F32)],
        compiler_params=pltpu.CompilerParams(
            dimension_semantics=("arbitrary", "arbitrary"), vmem_limit_bytes=VMEM_LIMIT),
        name="proj",
    )(x, g, wk, wqv, augk, augq)


def _flash_step(n_chains, q_of, k_of, v_of, j, mask, first, m_ref, l_ref, acc_ref):
    scores = [jnp.dot(k_of(c, j), q_of(c), preferred_element_type=F32)
              for c in range(n_chains)]
    probs, alphas = [], []
    for c in range(n_chains):
        s = scores[c] if mask is None else jnp.where(mask, scores[c], NEG)
        m_new = jnp.max(s, axis=0, keepdims=True)
        if not first:
            m_old = m_ref[c]
            m_new = jnp.maximum(m_old, m_new)
            alphas.append(jnp.exp(m_old - m_new))
        p = jnp.exp(s - m_new)
        m_ref[c] = m_new
        l_new = jnp.sum(p, axis=0, keepdims=True)
        l_ref[c] = l_new if first else alphas[c] * l_ref[c] + l_new
        probs.append(p.astype(BF16))
    for c in range(n_chains):
        pv = jnp.dot(v_of(c, j), probs[c], preferred_element_type=F32)
        acc_ref[c] = pv if first else alphas[c] * acc_ref[c] + pv


def _flash_chains(n_chains, q_of, k_of, v_of, masked_tiles, n_rest, m_ref, l_ref, acc_ref):
    for t, (j, mask) in enumerate(masked_tiles):
        _flash_step(n_chains, q_of, k_of, v_of, j, mask, t == 0, m_ref, l_ref, acc_ref)

    def body(j, carry):
        _flash_step(n_chains, q_of, k_of, v_of, j, None, False, m_ref, l_ref, acc_ref)
        return carry

    lax.fori_loop(0, n_rest, body, 0)


def _flash_scratch(n_chains, width):
    return [pltpu.VMEM((n_chains, 1, width), F32), pltpu.VMEM((n_chains, 1, width), F32),
            pltpu.VMEM((n_chains, HEAD_DIM, width), F32)]


def _attn_params():
    return pltpu.CompilerParams(
        dimension_semantics=("arbitrary", "arbitrary"), vmem_limit_bytes=VMEM_LIMIT)


MOBA_Q_TILES = 2


def _moba_kernel(q_ref, k_ref, v_ref, o_ref, m_ref, l_ref, acc_ref):
    i = pl.program_id(1)
    nh = q_ref.shape[1]
    width = q_ref.shape[3]
    kr = lax.broadcasted_iota(jnp.int32, (Q_TILE, width), 0)
    qc = lax.broadcasted_iota(jnp.int32, (Q_TILE, width), 1)
    own = [(MOBA_Q_TILES * i + t, kr + t * Q_TILE <= qc) for t in range(MOBA_Q_TILES)]
    _flash_chains(nh, lambda c: q_ref[0, c], lambda c, j: k_ref[0, c, j],
                  lambda c, j: v_ref[0, c, j], own, MOBA_Q_TILES * i, m_ref, l_ref, acc_ref)
    for h in range(nh):
        o_ref[0, h * HEAD_DIM:(h + 1) * HEAD_DIM, :] = (acc_ref[h] / l_ref[h]).astype(o_ref.dtype)


def _moba_call(qa, ka, va):
    b, nh, _, s = qa.shape
    nt = s // Q_TILE
    width = MOBA_Q_TILES * Q_TILE
    return pl.pallas_call(
        _moba_kernel,
        grid=(b, s // width),
        in_specs=[
            pl.BlockSpec((1, nh, OPERAND_WIDTH, width), lambda bi, i: (bi, 0, 0, i)),
            pl.BlockSpec((1, nh, nt, Q_TILE, OPERAND_WIDTH), lambda bi, i: (bi, 0, 0, 0, 0)),
            pl.BlockSpec((1, nh, nt, HEAD_DIM, Q_TILE), lambda bi, i: (bi, 0, 0, 0, 0)),
        ],
        out_specs=pl.BlockSpec((1, nh * HEAD_DIM, width), lambda bi, i: (bi, 0, i)),
        out_shape=jax.ShapeDtypeStruct((b, nh * HEAD_DIM, s), BF16),
        scratch_shapes=_flash_scratch(nh, width),
        compiler_params=_attn_params(),
        name="moba",
    )(qa, ka, va)


def _diff_kernel(q_ref, k_ref, v_ref, lq1_ref, lk1_ref, lq2_ref, lk2_ref, linit_ref, g_ref, o_ref,
                 qm_ref, m_ref, l_ref, acc_ref):
    i = pl.program_id(1)
    nh = q_ref.shape[1]
    row = lax.broadcasted_iota(jnp.int32, q_ref.shape[2:], 0)
    for h in range(nh):
        q = q_ref[0, h]
        zero = jnp.zeros_like(q)
        qm_ref[h, :, :Q_TILE] = jnp.where((row < DIFF_QK_DIM) | (row >= 2 * DIFF_QK_DIM), q, zero)
        qm_ref[h, :, Q_TILE:] = jnp.where(row >= DIFF_QK_DIM, q, zero)
    kr = lax.broadcasted_iota(jnp.int32, (Q_TILE, 2 * Q_TILE), 0)
    qc = lax.broadcasted_iota(jnp.int32, (Q_TILE, 2 * Q_TILE), 1)
    causal = kr <= jnp.where(qc >= Q_TILE, qc - Q_TILE, qc)
    _flash_chains(nh, lambda c: qm_ref[c], lambda c, j: k_ref[0, c, j],
                  lambda c, j: v_ref[0, c, j], [(i, causal)], i, m_ref, l_ref, acc_ref)
    lam_init = linit_ref[...]
    lam = (jnp.exp(jnp.sum(lq1_ref[...] * lk1_ref[...], axis=-1, keepdims=True))
           - jnp.exp(jnp.sum(lq2_ref[...] * lk2_ref[...], axis=-1, keepdims=True))
           + lam_init)
    for h in range(nh):
        w = acc_ref[h] / l_ref[h]
        o = w[:, :Q_TILE] - lam * w[:, Q_TILE:]
        ms = jnp.mean(o * o, axis=0, keepdims=True)
        y = (o * lax.rsqrt(ms + RMS_EPS)) * g_ref[...]
        o_ref[0, h * HEAD_DIM:(h + 1) * HEAD_DIM, :] = (y * (1.0 - lam_init)).astype(o_ref.dtype)


def _diff_call(qb, kb, vb, lq1, lk1, lq2, lk2, linit, g_col):
    b, nh, _, s = qb.shape
    nt = s // Q_TILE
    small = lambda shape: pl.BlockSpec(shape, lambda bi, i: (0,) * len(shape))
    return pl.pallas_call(
        _diff_kernel,
        grid=(b, nt),
        in_specs=[
            pl.BlockSpec((1, nh, OPERAND_WIDTH, Q_TILE), lambda bi, i: (bi, 0, 0, i)),
            pl.BlockSpec((1, nh, nt, Q_TILE, OPERAND_WIDTH), lambda bi, i: (bi, 0, 0, 0, 0)),
            pl.BlockSpec((1, nh, nt, HEAD_DIM, Q_TILE), lambda bi, i: (bi, 0, 0, 0, 0)),
            small((1, DIFF_QK_DIM)), small((1, DIFF_QK_DIM)),
            small((1, DIFF_QK_DIM)), small((1, DIFF_QK_DIM)),
            small((1, 1)), small((HEAD_DIM, 1)),
        ],
        out_specs=pl.BlockSpec((1, nh * HEAD_DIM, Q_TILE), lambda bi, i: (bi, 0, i)),
        out_shape=jax.ShapeDtypeStruct((b, nh * HEAD_DIM, s), BF16),
        scratch_shapes=([pltpu.VMEM((nh, OPERAND_WIDTH, 2 * Q_TILE), BF16)]
                        + _flash_scratch(nh, 2 * Q_TILE)),
        compiler_params=_attn_params(),
        name="diffattn",
    )(qb, kb, vb, lq1, lk1, lq2, lk2, linit, g_col)


SWA_BLOCKS_PER_STEP = 2


def _swa_kernel(q_ref, k_ref, v_ref, sink_ref, o_ref):
    i = pl.program_id(1)
    kr = lax.broadcasted_iota(jnp.int32, (2 * SWA_BLOCK, SWA_BLOCK), 0)
    qc = lax.broadcasted_iota(jnp.int32, (2 * SWA_BLOCK, SWA_BLOCK), 1)
    dist = qc + SWA_BLOCK - kr
    in_win = (dist >= 0) & (dist < SWA_WINDOW)
    chains = [(b2, kvh) for b2 in range(SWA_BLOCKS_PER_STEP) for kvh in range(SWA_KV_HEADS)]
    blocks, bands = [], []
    for b2 in range(SWA_BLOCKS_PER_STEP):
        n = SWA_BLOCKS_PER_STEP * i + b2
        blocks.append((jnp.maximum(n - 1, 0), n))
        first_row = jnp.where(n > 0, 0, SWA_BLOCK)
        band = jnp.where(in_win & (kr >= first_row), 0.0, NEG)
        bands.append(jnp.concatenate([band] * SWA_GROUP, axis=1))
    scores = []
    for b2, kvh in chains:
        prev, n = blocks[b2]
        cols = slice(b2 * SWA_BLOCK, (b2 + 1) * SWA_BLOCK)
        q = jnp.concatenate([q_ref[0, kvh * SWA_GROUP + g, :, cols]
                             for g in range(SWA_GROUP)], axis=1)
        k = jnp.concatenate([k_ref[0, kvh, prev], k_ref[0, kvh, n]], axis=0)
        scores.append(jnp.dot(k, q, preferred_element_type=F32))
    probs, denoms = [], []
    for (b2, kvh), s in zip(chains, scores):
        s = s + bands[b2]
        sink = sink_ref[kvh]
        m = jnp.maximum(jnp.max(s, axis=0, keepdims=True), sink)
        p = jnp.exp(s - m)
        denoms.append(jnp.sum(p, axis=0, keepdims=True) + jnp.exp(sink - m))
        probs.append(p.astype(BF16))
    for (b2, kvh), p, l in zip(chains, probs, denoms):
        prev, n = blocks[b2]
        cols = slice(b2 * SWA_BLOCK, (b2 + 1) * SWA_BLOCK)
        v = jnp.concatenate([v_ref[0, kvh, prev], v_ref[0, kvh, n]], axis=1)
        o = jnp.dot(v, p, preferred_element_type=F32) / l
        for g in range(SWA_GROUP):
            r0 = (kvh * SWA_GROUP + g) * HEAD_DIM
            o_ref[0, r0:r0 + HEAD_DIM, cols] = (
                o[:, g * SWA_BLOCK:(g + 1) * SWA_BLOCK].astype(o_ref.dtype))


def _swa_call(qc, kc, vc, sink_rows):
    b, nq, _, s = qc.shape
    nkv = kc.shape[1]
    nsb = s // SWA_BLOCK
    step = SWA_BLOCKS_PER_STEP * SWA_BLOCK
    return pl.pallas_call(
        _swa_kernel,
        grid=(b, s // step),
        in_specs=[
            pl.BlockSpec((1, nq, OPERAND_WIDTH, step), lambda bi, i: (bi, 0, 0, i)),
            pl.BlockSpec((1, nkv, nsb, SWA_BLOCK, OPERAND_WIDTH), lambda bi, i: (bi, 0, 0, 0, 0)),
            pl.BlockSpec((1, nkv, nsb, HEAD_DIM, SWA_BLOCK), lambda bi, i: (bi, 0, 0, 0, 0)),
            pl.BlockSpec(sink_rows.shape, lambda bi, i: (0, 0, 0)),
        ],
        out_specs=pl.BlockSpec((1, nq * HEAD_DIM, step), lambda bi, i: (bi, 0, i)),
        out_shape=jax.ShapeDtypeStruct((b, nq * HEAD_DIM, s), BF16),
        compiler_params=_attn_params(),
        name="swa",
    )(qc, kc, vc, sink_rows)


def _mix_ffn_kernel(x_ref, ma_ref, mb_ref, mc_ref, woa_ref, wob_ref, woc_ref,
                    g_ref, wg_ref, wu_ref, wd_ref, gf_ref, o_ref, *, final):
    tdims = (((0,), (0,)), ((), ()))
    x = x_ref[0]
    x = x + lax.dot_general(ma_ref[0], woa_ref[...], tdims, preferred_element_type=F32)
    x = x + lax.dot_general(mb_ref[0], wob_ref[...], tdims, preferred_element_type=F32)
    x = x + lax.dot_general(mc_ref[0], woc_ref[...], tdims, preferred_element_type=F32)
    x = _swiglu_residual(x, g_ref[...], wg_ref, wu_ref, wd_ref)
    if final:
        x = _rmsnorm(x, gf_ref[...])
    o_ref[0] = x


def _mix_ffn_call(x, ma, mb, mc, woa, wob, woc, g, wg, wu, wd, gf, final):
    b, s, _ = x.shape
    rows = FFN_ROWS
    return pl.pallas_call(
        functools.partial(_mix_ffn_kernel, final=final),
        grid=(b, s // rows),
        in_specs=[
            pl.BlockSpec((1, rows, D_MODEL), lambda bi, i: (bi, i, 0)),
            pl.BlockSpec((1, ma.shape[1], rows), lambda bi, i: (bi, 0, i)),
            pl.BlockSpec((1, mb.shape[1], rows), lambda bi, i: (bi, 0, i)),
            pl.BlockSpec((1, mc.shape[1], rows), lambda bi, i: (bi, 0, i)),
            _resident(woa.shape), _resident(wob.shape), _resident(woc.shape),
            _resident((1, D_MODEL)),
            _resident((D_MODEL, D_FF)), _resident((D_MODEL, D_FF)), _resident((D_FF, D_MODEL)),
            _resident((1, D_MODEL)),
        ],
        out_specs=pl.BlockSpec((1, rows, D_MODEL), lambda bi, i: (bi, i, 0)),
        out_shape=jax.ShapeDtypeStruct(x.shape, F32),
        compiler_params=pltpu.CompilerParams(
            dimension_semantics=("arbitrary", "arbitrary"), vmem_limit_bytes=VMEM_LIMIT),
        name="mix_ffn2_final" if final else "mix_ffn2",
    )(x, ma, mb, mc, woa, wob, woc, g, wg, wu, wd, gf)


def _split3(x):
    x1 = x.astype(BF16).astype(F32)
    r = x - x1
    x2 = r.astype(BF16).astype(F32)
    x3 = (r - x2).astype(BF16).astype(F32)
    return [x1, x2, x3]


def _bias_rows_kernel(raw_ref, o_ref):
    raw = raw_ref[0]
    pieces = _split3(raw[0:1]) + 2 * _split3(raw[1:2])
    s = raw.shape[1]
    row = lax.broadcasted_iota(jnp.int32, (16, s), 0)
    blk = jnp.zeros((16, s), F32)
    for r, piece in enumerate(pieces):
        blk = jnp.where(row == r, piece, blk)
    lead = AUG_BIAS_ROW - HEAD_DIM
    o_ref[0] = jnp.concatenate(
        [jnp.zeros((lead, s), F32), blk, jnp.zeros((HEAD_DIM - lead - 16, s), F32)], axis=0)


def _operand_tables(s):
    slopes = _alibi_slopes()
    head_slopes = np.concatenate([
        slopes[SWA_Q_HEADS + DIFF_HEADS:],
        slopes[SWA_Q_HEADS:SWA_Q_HEADS + DIFF_HEADS],
        slopes[:SWA_Q_HEADS],
    ]).astype(np.float32)
    pos = jnp.arange(s, dtype=F32)
    sl = jnp.asarray(head_slopes)[:, None]
    raw = jnp.zeros((N_Q_HEADS, 8, s), F32)
    raw = raw.at[:, 0, :].set(-sl * pos[None, :])
    raw = raw.at[:, 1, :].set(jnp.broadcast_to(sl, (N_Q_HEADS, s)))
    augq = pl.pallas_call(
        _bias_rows_kernel,
        grid=(N_Q_HEADS,),
        in_specs=[pl.BlockSpec((1, 8, s), lambda h: (h, 0, 0))],
        out_specs=pl.BlockSpec((1, HEAD_DIM, s), lambda h: (h, 0, 0)),
        out_shape=jax.ShapeDtypeStruct((N_Q_HEADS, HEAD_DIM, s), F32),
        name="bias_rows",
    )(raw)
    blk = jnp.arange(s, dtype=jnp.int32) // MOBA_BLOCK
    onehot = (blk[:, None] == jnp.arange(8, dtype=jnp.int32)[None, :]).astype(F32)
    ones = jnp.ones((s, 3), F32)
    base = jnp.broadcast_to((blk * MOBA_BLOCK).astype(F32)[:, None], (s, 3))
    off = jnp.broadcast_to((jnp.arange(s, dtype=jnp.int32) % MOBA_BLOCK).astype(F32)[:, None], (s, 3))
    augk = jnp.concatenate([
        jnp.zeros((s, HEAD_DIM), F32), onehot, ones, base, off,
        jnp.zeros((s, OPERAND_WIDTH - HEAD_DIM - 17), F32)], axis=1)
    return augq, augk


def _key_weight(w_in_l):
    a_k0 = MOBA_HEADS * HEAD_DIM
    b_k0 = 3 * MOBA_HEADS * HEAD_DIM + DIFF_HEADS * HEAD_DIM
    c_k0 = 3 * MOBA_HEADS * HEAD_DIM + 3 * DIFF_HEADS * HEAD_DIM + SWA_Q_HEADS * HEAD_DIM
    cols = jnp.concatenate([
        w_in_l[:, a_k0:a_k0 + MOBA_HEADS * HEAD_DIM],
        w_in_l[:, b_k0:b_k0 + DIFF_HEADS * HEAD_DIM],
        w_in_l[:, c_k0:c_k0 + SWA_KV_HEADS * HEAD_DIM]], axis=1)
    cols = cols.reshape(D_MODEL, N_K_HEADS, HEAD_DIM)
    cols = jnp.pad(cols, ((0, 0), (0, 0), (0, OPERAND_WIDTH - HEAD_DIM)))
    return cols.reshape(D_MODEL, N_K_HEADS * OPERAND_WIDTH).astype(BF16)


def _query_value_weight_t(w_in_l):
    a0 = 0
    b0 = 3 * MOBA_HEADS * HEAD_DIM
    c0 = b0 + 3 * DIFF_HEADS * HEAD_DIM
    aw = MOBA_HEADS * HEAD_DIM
    bw = DIFF_HEADS * HEAD_DIM
    cqw = SWA_Q_HEADS * HEAD_DIM
    ckw = SWA_KV_HEADS * HEAD_DIM
    cols = jnp.concatenate([
        w_in_l[:, a0:a0 + aw], w_in_l[:, b0:b0 + bw], w_in_l[:, c0:c0 + cqw],
        w_in_l[:, a0 + 2 * aw:a0 + 3 * aw], w_in_l[:, b0 + 2 * bw:b0 + 3 * bw],
        w_in_l[:, c0 + cqw + ckw:c0 + cqw + 2 * ckw]], axis=1)
    return cols.T.astype(BF16)


def kernel(x, norm_ffn1, w1_gate, w1_up, w1_down, norm_mix, w_in, lam_q1, lam_k1, lam_q2, lam_k2,
           diff_subln, sinks, w_out, norm_ffn2, w2_gate, w2_up, w2_down, final_norm):
    b, s, d = x.shape
    depth = w_in.shape[0]
    assert d == D_MODEL and s % Q_TILE == 0 and (b * s) % FFN_ROWS == 0
    augq, augk = _operand_tables(s)
    a_w = MOBA_HEADS * HEAD_DIM
    b_w = DIFF_HEADS * HEAD_DIM
    gf = final_norm.reshape(1, d)
    for l in range(depth):
        x2 = _ffn_call(x.reshape(b * s, d), norm_ffn1[l].reshape(1, d),
                       w1_gate[l].astype(BF16), w1_up[l].astype(BF16), w1_down[l].astype(BF16))
        x = x2.reshape(b, s, d)
        qa, qb, qc, ka, kb, kc, va, vb, vc = _proj_call(
            x, norm_mix[l].reshape(1, d), _key_weight(w_in[l]), _query_value_weight_t(w_in[l]),
            augk, augq)
        ma = _moba_call(qa, ka, va)
        lam_init = 0.8 - 0.6 * math.exp(-0.3 * l)
        mb = _diff_call(qb, kb, vb,
                        lam_q1[l].reshape(1, -1), lam_k1[l].reshape(1, -1),
                        lam_q2[l].reshape(1, -1), lam_k2[l].reshape(1, -1),
                        jnp.full((1, 1), lam_init, F32), diff_subln[l].reshape(-1, 1))
        sink_rows = jnp.repeat(sinks[l].reshape(SWA_KV_HEADS, 1, SWA_GROUP), SWA_BLOCK, axis=2)
        mc = _swa_call(qc, kc, vc, sink_rows)
        wo = w_out[l].astype(BF16)
        x = _mix_ffn_call(x, ma, mb, mc, wo[:a_w], wo[a_w:a_w + b_w], wo[a_w + b_w:],
                          norm_ffn2[l].reshape(1, d),
                          w2_gate[l].astype(BF16), w2_up[l].astype(BF16), w2_down[l].astype(BF16),
                          gf, final=(l == depth - 1))
    return x
```

```python
import functools
import math

import numpy as np
import jax
import jax.numpy as jnp
from jax import lax
from jax.experimental import pallas as pl
from jax.experimental.pallas import tpu as pltpu

F32 = jnp.float32
BF16 = jnp.bfloat16

D_MODEL = 1024
D_FF = 2816
HEAD_DIM = 64
MOBA_HEADS = 4
MOBA_BLOCK = 256
MOBA_TOPK = 3
DIFF_HEADS = 4
DIFF_QK_DIM = 32
SWA_Q_HEADS = 8
SWA_KV_HEADS = 2
SWA_GROUP = SWA_Q_HEADS // SWA_KV_HEADS
SWA_WINDOW = 128
SWA_BLOCK = 128
N_ALIBI = MOBA_HEADS + DIFF_HEADS + SWA_Q_HEADS
RMS_EPS = 1e-6
NEG = -1e30

N_Q_HEADS = MOBA_HEADS + DIFF_HEADS + SWA_Q_HEADS
N_K_HEADS = MOBA_HEADS + DIFF_HEADS + SWA_KV_HEADS
OPERAND_WIDTH = 128
VALUE_ROWS = HEAD_DIM + 16
LOG2E = math.log2(math.e)
AUG_FLAG_ROW = HEAD_DIM
AUG_BIAS_ROW = HEAD_DIM + 8
Q_TILE = 256
FFN_ROWS = 512
VMEM_LIMIT = 56 * 1024 * 1024


def _alibi_slopes():
    n = N_ALIBI
    return 2.0 ** (-8.0 * (np.arange(n, dtype=np.float32) + 1.0) / n)


def _rmsnorm(x, g):
    ms = jnp.mean(x * x, axis=-1, keepdims=True)
    return (x * lax.rsqrt(ms + RMS_EPS)) * g


def _swiglu_residual(x, g, wg_ref, wu_ref, wd_ref):
    h = _rmsnorm(x, g).astype(BF16)
    gate = jnp.dot(h, wg_ref[...], preferred_element_type=F32)
    up = jnp.dot(h, wu_ref[...], preferred_element_type=F32)
    act = (gate * jax.nn.sigmoid(gate) * up).astype(BF16)
    return x + 0.5 * jnp.dot(act, wd_ref[...], preferred_element_type=F32)


def _resident(shape):
    zeros = (0,) * len(shape)
    return pl.BlockSpec(shape, lambda *_: zeros, pipeline_mode=pl.Buffered(1))


def _layer_resident(stacked, layer, rows=None, row_block=0):
    shape = stacked.shape[1:] if rows is None else (rows,) + stacked.shape[2:]
    index = (layer, row_block) + (0,) * (len(shape) - 1)
    return pl.BlockSpec((None,) + shape, lambda *_: index, pipeline_mode=pl.Buffered(1))


def _ffn_kernel(x_ref, g_ref, wg_ref, wu_ref, wd_ref, o_ref):
    o_ref[...] = _swiglu_residual(x_ref[...], g_ref[...], wg_ref, wu_ref, wd_ref)


def _ffn_call(x2, g, wg, wu, wd, layer):
    t = x2.shape[0]
    return pl.pallas_call(
        _ffn_kernel,
        grid=(t // FFN_ROWS,),
        in_specs=[
            pl.BlockSpec((FFN_ROWS, D_MODEL), lambda i: (i, 0)),
            _layer_resident(g, layer),
            _layer_resident(wg, layer),
            _layer_resident(wu, layer),
            _layer_resident(wd, layer),
        ],
        out_specs=pl.BlockSpec((FFN_ROWS, D_MODEL), lambda i: (i, 0)),
        out_shape=jax.ShapeDtypeStruct(x2.shape, F32),
        compiler_params=pltpu.CompilerParams(
            dimension_semantics=("arbitrary",), vmem_limit_bytes=VMEM_LIMIT),
        name="ffn1",
    )(x2, g, wg, wu, wd)


def _split_hi_lo(x):
    hi = x.astype(BF16)
    lo = (x - hi.astype(F32)).astype(BF16)
    return hi, lo


def _proj_kernel(x_ref, g_ref, wk_ref, wqv_ref, augk_ref, augq_ref,
                 qa_ref, qb_ref, qc_ref, ka_ref, kb_ref, kc_ref, va_ref, vb_ref, vc_ref,
                 kmean_ref):
    i = pl.program_id(1)

    @pl.when(i == 0)
    def _():
        kmean_ref[...] = jnp.zeros_like(kmean_ref)

    h = _rmsnorm(x_ref[0], g_ref[...]).astype(BF16)
    pk = jnp.dot(h, wk_ref[...], preferred_element_type=F32)
    pqv = lax.dot_general(wqv_ref[...], h, (((1,), (1,)), ((), ())),
                          preferred_element_type=F32)
    augk = augk_ref[...]

    def k_slab(s):
        return pk[:, s * OPERAND_WIDTH:(s + 1) * OPERAND_WIDTH]

    for hh in range(MOBA_HEADS):
        ka_ref[0, hh, 0] = (k_slab(hh) + augk).astype(BF16)
        kb_ref[0, hh, 0] = (k_slab(MOBA_HEADS + hh) + augk).astype(BF16)
    for hh in range(SWA_KV_HEADS):
        slab = (k_slab(MOBA_HEADS + DIFF_HEADS + hh) + augk).astype(BF16)
        kc_ref[0, hh, 0] = slab[:SWA_BLOCK]
        kc_ref[0, hh, 1] = slab[SWA_BLOCK:]

    v0 = N_Q_HEADS * HEAD_DIM

    t = x_ref.shape[1]
    ones_rows = jnp.where(
        lax.broadcasted_iota(jnp.int32, (VALUE_ROWS - HEAD_DIM, t), 0) == 0, 1.0, 0.0)

    def v_slab(s):
        v = pqv[v0 + s * HEAD_DIM:v0 + (s + 1) * HEAD_DIM]
        return jnp.concatenate([v, ones_rows], axis=0).astype(BF16)

    for hh in range(MOBA_HEADS):
        va_ref[0, hh, 0] = v_slab(hh)
        vb_ref[0, hh, 0] = v_slab(MOBA_HEADS + hh)
    for hh in range(SWA_KV_HEADS):
        slab = v_slab(MOBA_HEADS + DIFF_HEADS + hh)
        vc_ref[0, hh, 0] = slab[:, :SWA_BLOCK]
        vc_ref[0, hh, 1] = slab[:, SWA_BLOCK:]

    def q_rows(s):
        return pqv[s * HEAD_DIM:(s + 1) * HEAD_DIM]

    n_blocks = kmean_ref.shape[1]
    jrow = lax.broadcasted_iota(jnp.int32, (n_blocks, t), 0)
    moba_scale = HEAD_DIM ** -0.5 * LOG2E
    for hh in range(MOBA_HEADS):
        kmean_ref[hh, pl.ds(i, 1), :] = jnp.mean(k_slab(hh), axis=0, keepdims=True)
        km_hi, km_lo = _split_hi_lo(kmean_ref[hh][:, :HEAD_DIM])
        qf = q_rows(hh)
        q_hi, q_lo = _split_hi_lo(qf)
        gate = (jnp.dot(km_hi, q_hi, preferred_element_type=F32)
                + jnp.dot(km_hi, q_lo, preferred_element_type=F32)
                + jnp.dot(km_lo, q_hi, preferred_element_type=F32))
        rank = jnp.zeros((n_blocks, t), F32)
        for m in range(n_blocks):
            gm = gate[m:m + 1, :]
            beats = (gm > gate) | ((gm == gate) & (m < jrow))
            rank = rank + jnp.where(beats, jnp.where(m < i, 1.0, 0.0), 0.0)
        chosen = ((rank < MOBA_TOPK) & (jrow < i)) | (jrow == i)
        flags = jnp.where(chosen, 0.0, NEG)
        aug = augq_ref[hh] + jnp.concatenate(
            [flags, jnp.zeros((HEAD_DIM - n_blocks, t), F32)], axis=0)
        qa_ref[0, hh, :HEAD_DIM, :] = (qf * moba_scale).astype(BF16)
        qa_ref[0, hh, HEAD_DIM:, :] = aug.astype(BF16)
    diff_scale = DIFF_QK_DIM ** -0.5 * LOG2E
    for hh in range(DIFF_HEADS):
        s = MOBA_HEADS + hh
        qb_ref[0, hh, :HEAD_DIM, :] = (q_rows(s) * diff_scale).astype(BF16)
        qb_ref[0, hh, HEAD_DIM:, :] = augq_ref[s].astype(BF16)
    swa_scale = HEAD_DIM ** -0.5 * LOG2E
    for hh in range(SWA_Q_HEADS):
        s = MOBA_HEADS + DIFF_HEADS + hh
        qc_ref[0, hh, :HEAD_DIM, :] = (q_rows(s) * swa_scale).astype(BF16)
        qc_ref[0, hh, HEAD_DIM:, :] = augq_ref[s].astype(BF16)


def _proj_call(x, g, wk, wqv, augk, augq, layer):
    b, s, _ = x.shape
    nt = s // Q_TILE
    nsb = s // SWA_BLOCK
    per_tile = Q_TILE // SWA_BLOCK
    out_shape = (
        jax.ShapeDtypeStruct((b, MOBA_HEADS, OPERAND_WIDTH, s), BF16),
        jax.ShapeDtypeStruct((b, DIFF_HEADS, OPERAND_WIDTH, s), BF16),
        jax.ShapeDtypeStruct((b, SWA_Q_HEADS, OPERAND_WIDTH, s), BF16),
        jax.ShapeDtypeStruct((b, MOBA_HEADS, nt, Q_TILE, OPERAND_WIDTH), BF16),
        jax.ShapeDtypeStruct((b, DIFF_HEADS, nt, Q_TILE, OPERAND_WIDTH), BF16),
        jax.ShapeDtypeStruct((b, SWA_KV_HEADS, nsb, SWA_BLOCK, OPERAND_WIDTH), BF16),
        jax.ShapeDtypeStruct((b, MOBA_HEADS, nt, VALUE_ROWS, Q_TILE), BF16),
        jax.ShapeDtypeStruct((b, DIFF_HEADS, nt, VALUE_ROWS, Q_TILE), BF16),
        jax.ShapeDtypeStruct((b, SWA_KV_HEADS, nsb, VALUE_ROWS, SWA_BLOCK), BF16),
    )
    out_specs = (
        pl.BlockSpec((1, MOBA_HEADS, OPERAND_WIDTH, Q_TILE), lambda bi, i: (bi, 0, 0, i)),
        pl.BlockSpec((1, DIFF_HEADS, OPERAND_WIDTH, Q_TILE), lambda bi, i: (bi, 0, 0, i)),
        pl.BlockSpec((1, SWA_Q_HEADS, OPERAND_WIDTH, Q_TILE), lambda bi, i: (bi, 0, 0, i)),
        pl.BlockSpec((1, MOBA_HEADS, 1, Q_TILE, OPERAND_WIDTH), lambda bi, i: (bi, 0, i, 0, 0)),
        pl.BlockSpec((1, DIFF_HEADS, 1, Q_TILE, OPERAND_WIDTH), lambda bi, i: (bi, 0, i, 0, 0)),
        pl.BlockSpec((1, SWA_KV_HEADS, per_tile, SWA_BLOCK, OPERAND_WIDTH),
                     lambda bi, i: (bi, 0, i, 0, 0)),
        pl.BlockSpec((1, MOBA_HEADS, 1, VALUE_ROWS, Q_TILE), lambda bi, i: (bi, 0, i, 0, 0)),
        pl.BlockSpec((1, DIFF_HEADS, 1, VALUE_ROWS, Q_TILE), lambda bi, i: (bi, 0, i, 0, 0)),
        pl.BlockSpec((1, SWA_KV_HEADS, per_tile, VALUE_ROWS, SWA_BLOCK),
                     lambda bi, i: (bi, 0, i, 0, 0)),
    )
    return pl.pallas_call(
        _proj_kernel,
        grid=(b, nt),
        in_specs=[
            pl.BlockSpec((1, Q_TILE, D_MODEL), lambda bi, i: (bi, i, 0)),
            _layer_resident(g, layer),
            _layer_resident(wk, layer),
            _layer_resident(wqv, layer),
            pl.BlockSpec((Q_TILE, OPERAND_WIDTH), lambda bi, i: (i, 0)),
            pl.BlockSpec((N_Q_HEADS, HEAD_DIM, Q_TILE), lambda bi, i: (0, 0, i)),
        ],
        out_specs=out_specs,
        out_shape=out_shape,
        scratch_shapes=[pltpu.VMEM((MOBA_HEADS, nt, OPERAND_WIDTH), F32)],
        compiler_params=pltpu.CompilerParams(
            dimension_semantics=("arbitrary", "arbitrary"), vmem_limit_bytes=VMEM_LIMIT),
        name="proj",
    )(x, g, wk, wqv, augk, augq)


def _flash_unrolled(n_chains, q_of, k_of, v_of, tiles):
    def scores_of(t):
        j, _ = tiles[t]
        return [jnp.dot(k_of(c, j), q_of(c), preferred_element_type=F32)
                for c in range(n_chains)]

    m = [None] * n_chains
    acc = [None] * n_chains
    upcoming = scores_of(0)
    for t, (j, mask) in enumerate(tiles):
        scores = upcoming
        if t + 1 < len(tiles):
            upcoming = scores_of(t + 1)
        probs, alphas = [], []
        for c in range(n_chains):
            s = scores[c] if mask is None else jnp.where(mask, scores[c], NEG)
            m_new = jnp.max(s, axis=0, keepdims=True)
            if t > 0:
                m_new = jnp.maximum(m[c], m_new)
                alphas.append(jnp.exp2(m[c] - m_new))
            m[c] = m_new
            probs.append(jnp.exp2(s - m_new).astype(BF16))
        for c in range(n_chains):
            pv = jnp.dot(v_of(c, j), probs[c], preferred_element_type=F32)
            acc[c] = pv if t == 0 else alphas[c] * acc[c] + pv
    return acc


def _normalized(acc):
    return acc[:HEAD_DIM] / acc[HEAD_DIM:HEAD_DIM + 1]


def _attn_params():
    return pltpu.CompilerParams(
        dimension_semantics=("arbitrary", "arbitrary"), vmem_limit_bytes=VMEM_LIMIT)


ATTN_TILE = 2 * Q_TILE


def _tile_pair_getters(k_ref, v_ref):
    def k_of(c, j):
        return jnp.concatenate([k_ref[0, c, 2 * j], k_ref[0, c, 2 * j + 1]], axis=0)

    def v_of(c, j):
        return jnp.concatenate([v_ref[0, c, 2 * j], v_ref[0, c, 2 * j + 1]], axis=1)

    return k_of, v_of


def _causal_then_past(i_static, causal):
    return [(i_static, causal)] + [(j, None) for j in range(i_static)]


def _moba_kernel(q_ref, k_ref, v_ref, o_ref):
    nh = q_ref.shape[1]
    k_of, v_of = _tile_pair_getters(k_ref, v_ref)
    for i_static in range(k_ref.shape[2] * Q_TILE // ATTN_TILE):
        @pl.when(pl.program_id(1) == i_static)
        def _(i_static=i_static):
            kr = lax.broadcasted_iota(jnp.int32, (ATTN_TILE, ATTN_TILE), 0)
            qc = lax.broadcasted_iota(jnp.int32, (ATTN_TILE, ATTN_TILE), 1)
            acc = _flash_unrolled(nh, lambda c: q_ref[0, c], k_of, v_of,
                                  _causal_then_past(i_static, kr <= qc))
            for h in range(nh):
                o_ref[0, h * HEAD_DIM:(h + 1) * HEAD_DIM, :] = (
                    _normalized(acc[h]).astype(o_ref.dtype))


def _moba_call(qa, ka, va):
    b, nh, _, s = qa.shape
    nt = s // Q_TILE
    return pl.pallas_call(
        _moba_kernel,
        grid=(b, s // ATTN_TILE),
        in_specs=[
            pl.BlockSpec((1, nh, OPERAND_WIDTH, ATTN_TILE), lambda bi, i: (bi, 0, 0, i)),
            pl.BlockSpec((1, nh, nt, Q_TILE, OPERAND_WIDTH), lambda bi, i: (bi, 0, 0, 0, 0)),
            pl.BlockSpec((1, nh, nt, VALUE_ROWS, Q_TILE), lambda bi, i: (bi, 0, 0, 0, 0)),
        ],
        out_specs=pl.BlockSpec((1, nh * HEAD_DIM, ATTN_TILE), lambda bi, i: (bi, 0, i)),
        out_shape=jax.ShapeDtypeStruct((b, nh * HEAD_DIM, s), BF16),
        compiler_params=_attn_params(),
        name="moba",
    )(qa, ka, va)


def _diff_kernel(q_ref, k_ref, v_ref, lq1_ref, lk1_ref, lq2_ref, lk2_ref, linit_ref, g_ref, o_ref,
                 qm_ref):
    nh = q_ref.shape[1]
    row = lax.broadcasted_iota(jnp.int32, q_ref.shape[2:], 0)
    for h in range(nh):
        q = q_ref[0, h]
        zero = jnp.zeros_like(q)
        qm_ref[h, :, :ATTN_TILE] = jnp.where(
            (row < DIFF_QK_DIM) | (row >= 2 * DIFF_QK_DIM), q, zero)
        qm_ref[h, :, ATTN_TILE:] = jnp.where(row >= DIFF_QK_DIM, q, zero)
    k_of, v_of = _tile_pair_getters(k_ref, v_ref)
    lam_init = linit_ref[...]
    lam = (jnp.exp(jnp.sum(lq1_ref[...] * lk1_ref[...], axis=-1, keepdims=True))
           - jnp.exp(jnp.sum(lq2_ref[...] * lk2_ref[...], axis=-1, keepdims=True))
           + lam_init)
    for i_static in range(k_ref.shape[2] * Q_TILE // ATTN_TILE):
        @pl.when(pl.program_id(1) == i_static)
        def _(i_static=i_static):
            kr = lax.broadcasted_iota(jnp.int32, (ATTN_TILE, 2 * ATTN_TILE), 0)
            qc = lax.broadcasted_iota(jnp.int32, (ATTN_TILE, 2 * ATTN_TILE), 1)
            causal = kr <= jnp.where(qc >= ATTN_TILE, qc - ATTN_TILE, qc)
            acc = _flash_unrolled(nh, lambda c: qm_ref[c], k_of, v_of,
                                  _causal_then_past(i_static, causal))
            for h in range(nh):
                w = _normalized(acc[h])
                o = w[:, :ATTN_TILE] - lam * w[:, ATTN_TILE:]
                ms = jnp.mean(o * o, axis=0, keepdims=True)
                y = (o * lax.rsqrt(ms + RMS_EPS)) * g_ref[...]
                o_ref[0, h * HEAD_DIM:(h + 1) * HEAD_DIM, :] = (
                    (y * (1.0 - lam_init)).astype(o_ref.dtype))


def _diff_call(qb, kb, vb, lam_q1, lam_k1, lam_q2, lam_k2, lam_init, g_col, layer):
    b, nh, _, s = qb.shape
    nt = s // Q_TILE
    per_layer = lambda arr: pl.BlockSpec(
        (None,) + arr.shape[1:], lambda bi, i: (layer,) + (0,) * (arr.ndim - 1))
    return pl.pallas_call(
        _diff_kernel,
        grid=(b, s // ATTN_TILE),
        in_specs=[
            pl.BlockSpec((1, nh, OPERAND_WIDTH, ATTN_TILE), lambda bi, i: (bi, 0, 0, i)),
            pl.BlockSpec((1, nh, nt, Q_TILE, OPERAND_WIDTH), lambda bi, i: (bi, 0, 0, 0, 0)),
            pl.BlockSpec((1, nh, nt, VALUE_ROWS, Q_TILE), lambda bi, i: (bi, 0, 0, 0, 0)),
            per_layer(lam_q1), per_layer(lam_k1), per_layer(lam_q2), per_layer(lam_k2),
            per_layer(lam_init), per_layer(g_col),
        ],
        out_specs=pl.BlockSpec((1, nh * HEAD_DIM, ATTN_TILE), lambda bi, i: (bi, 0, i)),
        out_shape=jax.ShapeDtypeStruct((b, nh * HEAD_DIM, s), BF16),
        scratch_shapes=[pltpu.VMEM((nh, OPERAND_WIDTH, 2 * ATTN_TILE), BF16)],
        compiler_params=_attn_params(),
        name="diffattn",
    )(qb, kb, vb, lam_q1, lam_k1, lam_q2, lam_k2, lam_init, g_col)


SWA_BLOCKS_PER_STEP = 2


def _swa_kernel(q_ref, k_ref, v_ref, sink_ref, o_ref):
    i = pl.program_id(1)
    kr = lax.broadcasted_iota(jnp.int32, (2 * SWA_BLOCK, SWA_BLOCK), 0)
    qc = lax.broadcasted_iota(jnp.int32, (2 * SWA_BLOCK, SWA_BLOCK), 1)
    dist = qc + SWA_BLOCK - kr
    in_win = (dist >= 0) & (dist < SWA_WINDOW)
    chains = [(b2, kvh) for b2 in range(SWA_BLOCKS_PER_STEP) for kvh in range(SWA_KV_HEADS)]
    blocks, bands = [], []
    for b2 in range(SWA_BLOCKS_PER_STEP):
        n = SWA_BLOCKS_PER_STEP * i + b2
        blocks.append((jnp.maximum(n - 1, 0), n))
        first_row = jnp.where(n > 0, 0, SWA_BLOCK)
        band = jnp.where(in_win & (kr >= first_row), 0.0, NEG)
        bands.append(jnp.concatenate([band] * SWA_GROUP, axis=1))
    scores = []
    for b2, kvh in chains:
        prev, n = blocks[b2]
        cols = slice(b2 * SWA_BLOCK, (b2 + 1) * SWA_BLOCK)
        q = jnp.concatenate([q_ref[0, kvh * SWA_GROUP + g, :, cols]
                             for g in range(SWA_GROUP)], axis=1)
        k = jnp.concatenate([k_ref[0, kvh, prev], k_ref[0, kvh, n]], axis=0)
        scores.append(jnp.dot(k, q, preferred_element_type=F32))
    probs, sink_terms = [], []
    for (b2, kvh), s in zip(chains, scores):
        s = s + bands[b2]
        sink = sink_ref[kvh] * LOG2E
        m = jnp.maximum(jnp.max(s, axis=0, keepdims=True), sink)
        probs.append(jnp.exp2(s - m).astype(BF16))
        sink_terms.append(jnp.exp2(sink - m))
    for (b2, kvh), p, sink_term in zip(chains, probs, sink_terms):
        prev, n = blocks[b2]
        cols = slice(b2 * SWA_BLOCK, (b2 + 1) * SWA_BLOCK)
        v = jnp.concatenate([v_ref[0, kvh, prev], v_ref[0, kvh, n]], axis=1)
        acc = jnp.dot(v, p, preferred_element_type=F32)
        o = acc[:HEAD_DIM] / (acc[HEAD_DIM:HEAD_DIM + 1] + sink_term)
        for g in range(SWA_GROUP):
            r0 = (kvh * SWA_GROUP + g) * HEAD_DIM
            o_ref[0, r0:r0 + HEAD_DIM, cols] = (
                o[:, g * SWA_BLOCK:(g + 1) * SWA_BLOCK].astype(o_ref.dtype))


def _swa_call(qc, kc, vc, sink_rows, layer):
    b, nq, _, s = qc.shape
    nkv = kc.shape[1]
    nsb = s // SWA_BLOCK
    step = SWA_BLOCKS_PER_STEP * SWA_BLOCK
    return pl.pallas_call(
        _swa_kernel,
        grid=(b, s // step),
        in_specs=[
            pl.BlockSpec((1, nq, OPERAND_WIDTH, step), lambda bi, i: (bi, 0, 0, i)),
            pl.BlockSpec((1, nkv, nsb, SWA_BLOCK, OPERAND_WIDTH), lambda bi, i: (bi, 0, 0, 0, 0)),
            pl.BlockSpec((1, nkv, nsb, VALUE_ROWS, SWA_BLOCK), lambda bi, i: (bi, 0, 0, 0, 0)),
            pl.BlockSpec((None,) + sink_rows.shape[1:], lambda bi, i: (layer, 0, 0, 0)),
        ],
        out_specs=pl.BlockSpec((1, nq * HEAD_DIM, step), lambda bi, i: (bi, 0, i)),
        out_shape=jax.ShapeDtypeStruct((b, nq * HEAD_DIM, s), BF16),
        compiler_params=_attn_params(),
        name="swa",
    )(qc, kc, vc, sink_rows)


def _mix_ffn_kernel(x_ref, ma_ref, mb_ref, mc_ref, woa_ref, wob_ref, woc_ref,
                    g_ref, wg_ref, wu_ref, wd_ref, gf_ref, o_ref, *, final):
    tdims = (((0,), (0,)), ((), ()))
    x = x_ref[0]
    x = x + lax.dot_general(ma_ref[0], woa_ref[...], tdims, preferred_element_type=F32)
    x = x + lax.dot_general(mb_ref[0], wob_ref[...], tdims, preferred_element_type=F32)
    x = x + lax.dot_general(mc_ref[0], woc_ref[...], tdims, preferred_element_type=F32)
    x = _swiglu_residual(x, g_ref[...], wg_ref, wu_ref, wd_ref)
    if final:
        x = _rmsnorm(x, gf_ref[...])
    o_ref[0] = x


def _mix_ffn_call(x, ma, mb, mc, wo, g, wg, wu, wd, gf, layer, final):
    b, s, _ = x.shape
    rows = FFN_ROWS
    a_rows, b_rows, c_rows = ma.shape[1], mb.shape[1], mc.shape[1]
    assert a_rows == b_rows and c_rows == a_rows + b_rows
    return pl.pallas_call(
        functools.partial(_mix_ffn_kernel, final=final),
        grid=(b, s // rows),
        in_specs=[
            pl.BlockSpec((1, rows, D_MODEL), lambda bi, i: (bi, i, 0)),
            pl.BlockSpec((1, a_rows, rows), lambda bi, i: (bi, 0, i)),
            pl.BlockSpec((1, b_rows, rows), lambda bi, i: (bi, 0, i)),
            pl.BlockSpec((1, c_rows, rows), lambda bi, i: (bi, 0, i)),
            _layer_resident(wo, layer, a_rows, 0),
            _layer_resident(wo, layer, b_rows, 1),
            _layer_resident(wo, layer, c_rows, 1),
            _layer_resident(g, layer),
            _layer_resident(wg, layer), _layer_resident(wu, layer), _layer_resident(wd, layer),
            _resident((1, D_MODEL)),
        ],
        out_specs=pl.BlockSpec((1, rows, D_MODEL), lambda bi, i: (bi, i, 0)),
        out_shape=jax.ShapeDtypeStruct(x.shape, F32),
        compiler_params=pltpu.CompilerParams(
            dimension_semantics=("arbitrary", "arbitrary"), vmem_limit_bytes=VMEM_LIMIT),
        name="mix_ffn2_final" if final else "mix_ffn2",
    )(x, ma, mb, mc, wo, wo, wo, g, wg, wu, wd, gf)


def _split3(x):
    x1 = x.astype(BF16).astype(F32)
    r = x - x1
    x2 = r.astype(BF16).astype(F32)
    x3 = (r - x2).astype(BF16).astype(F32)
    return [x1, x2, x3]


def _bias_rows_kernel(raw_ref, o_ref):
    raw = raw_ref[0]
    pieces = _split3(raw[0:1]) + 2 * _split3(raw[1:2])
    s = raw.shape[1]
    row = lax.broadcasted_iota(jnp.int32, (16, s), 0)
    blk = jnp.zeros((16, s), F32)
    for r, piece in enumerate(pieces):
        blk = jnp.where(row == r, piece, blk)
    lead = AUG_BIAS_ROW - HEAD_DIM
    o_ref[0] = jnp.concatenate(
        [jnp.zeros((lead, s), F32), blk, jnp.zeros((HEAD_DIM - lead - 16, s), F32)], axis=0)


def _operand_tables(s):
    slopes = _alibi_slopes()
    head_slopes = np.concatenate([
        slopes[SWA_Q_HEADS + DIFF_HEADS:],
        slopes[SWA_Q_HEADS:SWA_Q_HEADS + DIFF_HEADS],
        slopes[:SWA_Q_HEADS],
    ]).astype(np.float32)
    pos = jnp.arange(s, dtype=F32)
    sl = jnp.asarray(head_slopes * np.float32(LOG2E))[:, None]
    raw = jnp.zeros((N_Q_HEADS, 8, s), F32)
    raw = raw.at[:, 0, :].set(-sl * pos[None, :])
    raw = raw.at[:, 1, :].set(jnp.broadcast_to(sl, (N_Q_HEADS, s)))
    augq = pl.pallas_call(
        _bias_rows_kernel,
        grid=(N_Q_HEADS,),
        in_specs=[pl.BlockSpec((1, 8, s), lambda h: (h, 0, 0))],
        out_specs=pl.BlockSpec((1, HEAD_DIM, s), lambda h: (h, 0, 0)),
        out_shape=jax.ShapeDtypeStruct((N_Q_HEADS, HEAD_DIM, s), F32),
        name="bias_rows",
    )(raw)
    blk = jnp.arange(s, dtype=jnp.int32) // MOBA_BLOCK
    onehot = (blk[:, None] == jnp.arange(8, dtype=jnp.int32)[None, :]).astype(F32)
    ones = jnp.ones((s, 3), F32)
    base = jnp.broadcast_to((blk * MOBA_BLOCK).astype(F32)[:, None], (s, 3))
    off = jnp.broadcast_to((jnp.arange(s, dtype=jnp.int32) % MOBA_BLOCK).astype(F32)[:, None], (s, 3))
    augk = jnp.concatenate([
        jnp.zeros((s, HEAD_DIM), F32), onehot, ones, base, off,
        jnp.zeros((s, OPERAND_WIDTH - HEAD_DIM - 17), F32)], axis=1)
    return augq, augk


def _key_weight(w_in):
    depth = w_in.shape[0]
    a_k0 = MOBA_HEADS * HEAD_DIM
    b_k0 = 3 * MOBA_HEADS * HEAD_DIM + DIFF_HEADS * HEAD_DIM
    c_k0 = 3 * MOBA_HEADS * HEAD_DIM + 3 * DIFF_HEADS * HEAD_DIM + SWA_Q_HEADS * HEAD_DIM
    cols = jnp.concatenate([
        w_in[:, :, a_k0:a_k0 + MOBA_HEADS * HEAD_DIM],
        w_in[:, :, b_k0:b_k0 + DIFF_HEADS * HEAD_DIM],
        w_in[:, :, c_k0:c_k0 + SWA_KV_HEADS * HEAD_DIM]], axis=2)
    cols = cols.astype(BF16).reshape(depth, D_MODEL, N_K_HEADS, HEAD_DIM)
    cols = jnp.pad(cols, ((0, 0), (0, 0), (0, 0), (0, OPERAND_WIDTH - HEAD_DIM)))
    return cols.reshape(depth, D_MODEL, N_K_HEADS * OPERAND_WIDTH)


def _query_value_weight_t(w_in):
    a0 = 0
    b0 = 3 * MOBA_HEADS * HEAD_DIM
    c0 = b0 + 3 * DIFF_HEADS * HEAD_DIM
    aw = MOBA_HEADS * HEAD_DIM
    bw = DIFF_HEADS * HEAD_DIM
    cqw = SWA_Q_HEADS * HEAD_DIM
    ckw = SWA_KV_HEADS * HEAD_DIM
    cols = jnp.concatenate([
        w_in[:, :, a0:a0 + aw], w_in[:, :, b0:b0 + bw], w_in[:, :, c0:c0 + cqw],
        w_in[:, :, a0 + 2 * aw:a0 + 3 * aw], w_in[:, :, b0 + 2 * bw:b0 + 3 * bw],
        w_in[:, :, c0 + cqw + ckw:c0 + cqw + 2 * ckw]], axis=2)
    return jnp.swapaxes(cols.astype(BF16), 1, 2)


def kernel(x, norm_ffn1, w1_gate, w1_up, w1_down, norm_mix, w_in, lam_q1, lam_k1, lam_q2, lam_k2,
           diff_subln, sinks, w_out, norm_ffn2, w2_gate, w2_up, w2_down, final_norm):
    b, s, d = x.shape
    depth = w_in.shape[0]
    assert d == D_MODEL and s % ATTN_TILE == 0 and (b * s) % FFN_ROWS == 0
    augq, augk = _operand_tables(s)
    as_rows = lambda p: p.reshape(depth, 1, -1)
    g1, gm, g2 = as_rows(norm_ffn1), as_rows(norm_mix), as_rows(norm_ffn2)
    w1g, w1u, w1d = w1_gate.astype(BF16), w1_up.astype(BF16), w1_down.astype(BF16)
    w2g, w2u, w2d = w2_gate.astype(BF16), w2_up.astype(BF16), w2_down.astype(BF16)
    wk, wqv, wo = _key_weight(w_in), _query_value_weight_t(w_in), w_out.astype(BF16)
    lq1, lk1, lq2, lk2 = as_rows(lam_q1), as_rows(lam_k1), as_rows(lam_q2), as_rows(lam_k2)
    lam_init = jnp.asarray([0.8 - 0.6 * math.exp(-0.3 * l) for l in range(depth)],
                           F32).reshape(depth, 1, 1)
    subln = diff_subln.reshape(depth, -1, 1)
    sink_rows = jnp.repeat(sinks.reshape(depth, SWA_KV_HEADS, 1, SWA_GROUP), SWA_BLOCK, axis=3)
    gf = final_norm.reshape(1, d)
    for l in range(depth):
        x = _ffn_call(x.reshape(b * s, d), g1, w1g, w1u, w1d, l).reshape(b, s, d)
        qa, qb, qc, ka, kb, kc, va, vb, vc = _proj_call(x, gm, wk, wqv, augk, augq, l)
        ma = _moba_call(qa, ka, va)
        mb = _diff_call(qb, kb, vb, lq1, lk1, lq2, lk2, lam_init, subln, l)
        mc = _swa_call(qc, kc, vc, sink_rows, l)
        x = _mix_ffn_call(x, ma, mb, mc, wo, g2, w2g, w2u, w2d, gf, l, final=(l == depth - 1))
    return x
```

```python
import functools
import math

import numpy as np
import jax
import jax.numpy as jnp
from jax import lax
from jax.experimental import pallas as pl
from jax.experimental.pallas import tpu as pltpu

F32 = jnp.float32
BF16 = jnp.bfloat16

D_MODEL = 1024
D_FF = 2816
HEAD_DIM = 64
MOBA_HEADS = 4
MOBA_BLOCK = 256
MOBA_TOPK = 3
DIFF_HEADS = 4
DIFF_QK_DIM = 32
SWA_Q_HEADS = 8
SWA_KV_HEADS = 2
SWA_GROUP = SWA_Q_HEADS // SWA_KV_HEADS
SWA_WINDOW = 128
SWA_BLOCK = 128
N_ALIBI = MOBA_HEADS + DIFF_HEADS + SWA_Q_HEADS
RMS_EPS = 1e-6
NEG = -1e30

N_Q_HEADS = MOBA_HEADS + DIFF_HEADS + SWA_Q_HEADS
OPERAND_WIDTH = 128
VALUE_ROWS = HEAD_DIM + 16
LOG2E = math.log2(math.e)
AUG_FLAG_ROW = HEAD_DIM
AUG_BIAS_ROW = HEAD_DIM + 8
Q_TILE = 256
FFN_ROWS = 512
VMEM_LIMIT = 56 * 1024 * 1024


def _alibi_slopes():
    n = N_ALIBI
    return 2.0 ** (-8.0 * (np.arange(n, dtype=np.float32) + 1.0) / n)


def _rmsnorm(x, g):
    ms = jnp.mean(x * x, axis=-1, keepdims=True)
    return (x * lax.rsqrt(ms + RMS_EPS)) * g


def _swiglu_residual(x, g, wg_ref, wu_ref, wd_ref):
    h = _rmsnorm(x, g).astype(BF16)
    gate = jnp.dot(h, wg_ref[...], preferred_element_type=F32)
    up = jnp.dot(h, wu_ref[...], preferred_element_type=F32)
    act = (gate * jax.nn.sigmoid(gate) * up).astype(BF16)
    return x + 0.5 * jnp.dot(act, wd_ref[...], preferred_element_type=F32)


def _resident(shape):
    zeros = (0,) * len(shape)
    return pl.BlockSpec(shape, lambda *_: zeros, pipeline_mode=pl.Buffered(1))


def _layer_resident(stacked, layer, rows=None, row_block=0):
    shape = stacked.shape[1:] if rows is None else (rows,) + stacked.shape[2:]
    index = (layer, row_block) + (0,) * (len(shape) - 1)
    return pl.BlockSpec((None,) + shape, lambda *_: index, pipeline_mode=pl.Buffered(1))


def _ffn_kernel(x_ref, g_ref, wg_ref, wu_ref, wd_ref, o_ref):
    half = x_ref.shape[0] // 2
    for r in range(2):
        rows = slice(r * half, (r + 1) * half)
        o_ref[rows, :] = _swiglu_residual(x_ref[rows, :], g_ref[...], wg_ref, wu_ref, wd_ref)


def _ffn_call(x2, g, wg, wu, wd, layer):
    t = x2.shape[0]
    return pl.pallas_call(
        _ffn_kernel,
        grid=(t // FFN_ROWS,),
        in_specs=[
            pl.BlockSpec((FFN_ROWS, D_MODEL), lambda i: (i, 0)),
            _layer_resident(g, layer),
            _layer_resident(wg, layer),
            _layer_resident(wu, layer),
            _layer_resident(wd, layer),
        ],
        out_specs=pl.BlockSpec((FFN_ROWS, D_MODEL), lambda i: (i, 0)),
        out_shape=jax.ShapeDtypeStruct(x2.shape, F32),
        compiler_params=pltpu.CompilerParams(
            dimension_semantics=("arbitrary",), vmem_limit_bytes=VMEM_LIMIT),
        name="ffn1",
    )(x2, g, wg, wu, wd)


def _split_hi_lo(x):
    hi = x.astype(BF16)
    lo = (x - hi.astype(F32)).astype(BF16)
    return hi, lo


_AQ0 = 0
_AK0 = _AQ0 + MOBA_HEADS * HEAD_DIM
_AV0 = _AK0 + MOBA_HEADS * HEAD_DIM
_BQ0 = _AV0 + MOBA_HEADS * HEAD_DIM
_BK0 = _BQ0 + DIFF_HEADS * HEAD_DIM
_BV0 = _BK0 + DIFF_HEADS * HEAD_DIM
_CQ0 = _BV0 + DIFF_HEADS * HEAD_DIM
_CK0 = _CQ0 + SWA_Q_HEADS * HEAD_DIM
_CV0 = _CK0 + SWA_KV_HEADS * HEAD_DIM
PROJ_ROWS = 2 * Q_TILE


def _proj_kernel(x_ref, g_ref, wt_ref, augk_ref, augq_ref,
                 qa_ref, qb_ref, qc_ref, ka_ref, kb_ref, kc_ref, va_ref, vb_ref, vc_ref,
                 kmean_ref):
    i = pl.program_id(1)
    t = x_ref.shape[1]
    blocks_per_step = t // MOBA_BLOCK

    @pl.when(i == 0)
    def _():
        kmean_ref[...] = jnp.zeros_like(kmean_ref)

    h = _rmsnorm(x_ref[0], g_ref[...]).astype(BF16)
    p = lax.dot_general(wt_ref[...], h, (((1,), (1,)), ((), ())),
                        preferred_element_type=F32)
    augk = augk_ref[...]
    lane = lax.broadcasted_iota(jnp.int32, (t, OPERAND_WIDTH), 1)

    def key_pair(r0):
        kt = p[r0:r0 + 2 * HEAD_DIM].T
        even = jnp.where(lane < HEAD_DIM, kt, augk).astype(BF16)
        odd = jnp.where(lane < HEAD_DIM, pltpu.roll(kt, HEAD_DIM, 1), augk).astype(BF16)
        return kt, (even, odd)

    def store_tiles(ref, hh, slab, rows, axis):
        for n in range(slab.shape[axis] // rows):
            piece = slab[n * rows:(n + 1) * rows] if axis == 0 else slab[:, n * rows:(n + 1) * rows]
            ref[0, hh, n] = piece

    for pair in range(MOBA_HEADS // 2):
        kt, slabs = key_pair(_AK0 + pair * 2 * HEAD_DIM)
        for blk in range(blocks_per_step):
            kmean_ref[pair, pl.ds(blocks_per_step * i + blk, 1), :] = jnp.mean(
                kt[blk * MOBA_BLOCK:(blk + 1) * MOBA_BLOCK], axis=0, keepdims=True)
        for par in range(2):
            store_tiles(ka_ref, 2 * pair + par, slabs[par], Q_TILE, 0)
    for pair in range(DIFF_HEADS // 2):
        _, slabs = key_pair(_BK0 + pair * 2 * HEAD_DIM)
        for par in range(2):
            store_tiles(kb_ref, 2 * pair + par, slabs[par], Q_TILE, 0)
    for pair in range(SWA_KV_HEADS // 2):
        _, slabs = key_pair(_CK0 + pair * 2 * HEAD_DIM)
        for par in range(2):
            store_tiles(kc_ref, 2 * pair + par, slabs[par], SWA_BLOCK, 0)

    ones_rows = jnp.where(
        lax.broadcasted_iota(jnp.int32, (VALUE_ROWS - HEAD_DIM, t), 0) == 0, 1.0, 0.0)

    def v_slab(r0):
        return jnp.concatenate([p[r0:r0 + HEAD_DIM], ones_rows], axis=0).astype(BF16)

    for hh in range(MOBA_HEADS):
        store_tiles(va_ref, hh, v_slab(_AV0 + hh * HEAD_DIM), Q_TILE, 1)
    for hh in range(DIFF_HEADS):
        store_tiles(vb_ref, hh, v_slab(_BV0 + hh * HEAD_DIM), Q_TILE, 1)
    for hh in range(SWA_KV_HEADS):
        store_tiles(vc_ref, hh, v_slab(_CV0 + hh * HEAD_DIM), SWA_BLOCK, 1)

    n_blocks = kmean_ref.shape[1]
    jrow = lax.broadcasted_iota(jnp.int32, (n_blocks, t), 0)
    qcol = lax.broadcasted_iota(jnp.int32, (n_blocks, t), 1)
    qblk = blocks_per_step * i + qcol // MOBA_BLOCK
    zeros_rows = jnp.zeros((HEAD_DIM, t), F32)
    moba_scale = HEAD_DIM ** -0.5 * LOG2E
    for hh in range(MOBA_HEADS):
        qf = p[_AQ0 + hh * HEAD_DIM:_AQ0 + (hh + 1) * HEAD_DIM]
        q_pad = jnp.concatenate([qf, zeros_rows] if hh % 2 == 0 else [zeros_rows, qf], axis=0)
        km_hi, km_lo = _split_hi_lo(kmean_ref[hh // 2])
        q_hi, q_lo = _split_hi_lo(q_pad)
        gate = (jnp.dot(km_hi, q_hi, preferred_element_type=F32)
                + jnp.dot(km_hi, q_lo, preferred_element_type=F32)
                + jnp.dot(km_lo, q_hi, preferred_element_type=F32))
        rank = jnp.zeros((n_blocks, t), F32)
        for m in range(n_blocks):
            gm = gate[m:m + 1, :]
            beats = (gm > gate) | ((gm == gate) & (m < jrow))
            rank = rank + jnp.where(beats & (m < qblk), 1.0, 0.0)
        chosen = ((rank < MOBA_TOPK) & (jrow < qblk)) | (jrow == qblk)
        flags = jnp.where(chosen, 0.0, NEG)
        aug = augq_ref[hh] + jnp.concatenate(
            [flags, jnp.zeros((HEAD_DIM - n_blocks, t), F32)], axis=0)
        qa_ref[0, hh, :HEAD_DIM, :] = (qf * moba_scale).astype(BF16)
        qa_ref[0, hh, HEAD_DIM:, :] = aug.astype(BF16)
    diff_scale = DIFF_QK_DIM ** -0.5 * LOG2E
    for hh in range(DIFF_HEADS):
        qf = p[_BQ0 + hh * HEAD_DIM:_BQ0 + (hh + 1) * HEAD_DIM]
        qb_ref[0, hh, :HEAD_DIM, :] = (qf * diff_scale).astype(BF16)
        qb_ref[0, hh, HEAD_DIM:, :] = augq_ref[MOBA_HEADS + hh].astype(BF16)
    swa_scale = HEAD_DIM ** -0.5 * LOG2E
    for hh in range(SWA_Q_HEADS):
        qf = p[_CQ0 + hh * HEAD_DIM:_CQ0 + (hh + 1) * HEAD_DIM]
        qc_ref[0, hh, :HEAD_DIM, :] = (qf * swa_scale).astype(BF16)
        qc_ref[0, hh, HEAD_DIM:, :] = augq_ref[MOBA_HEADS + DIFF_HEADS + hh].astype(BF16)


def _proj_call(x, g, wt, augk, augq, layer):
    b, s, _ = x.shape
    nt = s // Q_TILE
    nsb = s // SWA_BLOCK
    rows = PROJ_ROWS
    tiles = rows // Q_TILE
    sblocks = rows // SWA_BLOCK
    out_shape = (
        jax.ShapeDtypeStruct((b, MOBA_HEADS, OPERAND_WIDTH, s), BF16),
        jax.ShapeDtypeStruct((b, DIFF_HEADS, OPERAND_WIDTH, s), BF16),
        jax.ShapeDtypeStruct((b, SWA_Q_HEADS, OPERAND_WIDTH, s), BF16),
        jax.ShapeDtypeStruct((b, MOBA_HEADS, nt, Q_TILE, OPERAND_WIDTH), BF16),
        jax.ShapeDtypeStruct((b, DIFF_HEADS, nt, Q_TILE, OPERAND_WIDTH), BF16),
        jax.ShapeDtypeStruct((b, SWA_KV_HEADS, nsb, SWA_BLOCK, OPERAND_WIDTH), BF16),
        jax.ShapeDtypeStruct((b, MOBA_HEADS, nt, VALUE_ROWS, Q_TILE), BF16),
        jax.ShapeDtypeStruct((b, DIFF_HEADS, nt, VALUE_ROWS, Q_TILE), BF16),
        jax.ShapeDtypeStruct((b, SWA_KV_HEADS, nsb, VALUE_ROWS, SWA_BLOCK), BF16),
    )
    out_specs = (
        pl.BlockSpec((1, MOBA_HEADS, OPERAND_WIDTH, rows), lambda bi, i: (bi, 0, 0, i)),
        pl.BlockSpec((1, DIFF_HEADS, OPERAND_WIDTH, rows), lambda bi, i: (bi, 0, 0, i)),
        pl.BlockSpec((1, SWA_Q_HEADS, OPERAND_WIDTH, rows), lambda bi, i: (bi, 0, 0, i)),
        pl.BlockSpec((1, MOBA_HEADS, tiles, Q_TILE, OPERAND_WIDTH), lambda bi, i: (bi, 0, i, 0, 0)),
        pl.BlockSpec((1, DIFF_HEADS, tiles, Q_TILE, OPERAND_WIDTH), lambda bi, i: (bi, 0, i, 0, 0)),
        pl.BlockSpec((1, SWA_KV_HEADS, sblocks, SWA_BLOCK, OPERAND_WIDTH),
                     lambda bi, i: (bi, 0, i, 0, 0)),
        pl.BlockSpec((1, MOBA_HEADS, tiles, VALUE_ROWS, Q_TILE), lambda bi, i: (bi, 0, i, 0, 0)),
        pl.BlockSpec((1, DIFF_HEADS, tiles, VALUE_ROWS, Q_TILE), lambda bi, i: (bi, 0, i, 0, 0)),
        pl.BlockSpec((1, SWA_KV_HEADS, sblocks, VALUE_ROWS, SWA_BLOCK),
                     lambda bi, i: (bi, 0, i, 0, 0)),
    )
    return pl.pallas_call(
        _proj_kernel,
        grid=(b, s // rows),
        in_specs=[
            pl.BlockSpec((1, rows, D_MODEL), lambda bi, i: (bi, i, 0)),
            _layer_resident(g, layer),
            _layer_resident(wt, layer),
            pl.BlockSpec((rows, OPERAND_WIDTH), lambda bi, i: (i, 0)),
            pl.BlockSpec((N_Q_HEADS, HEAD_DIM, rows), lambda bi, i: (0, 0, i)),
        ],
        out_specs=out_specs,
        out_shape=out_shape,
        scratch_shapes=[pltpu.VMEM((MOBA_HEADS // 2, s // MOBA_BLOCK, OPERAND_WIDTH), F32)],
        compiler_params=pltpu.CompilerParams(
            dimension_semantics=("arbitrary", "arbitrary"), vmem_limit_bytes=VMEM_LIMIT),
        name="proj",
    )(x, g, wt, augk, augq)


def _flash_unrolled(n_chains, q_of, k_of, v_of, tiles):
    def scores_of(t):
        j, _ = tiles[t]
        return [jnp.dot(k_of(c, j), q_of(c), preferred_element_type=F32)
                for c in range(n_chains)]

    m = [None] * n_chains
    acc = [None] * n_chains
    upcoming = scores_of(0)
    for t, (j, mask) in enumerate(tiles):
        scores = upcoming
        if t + 1 < len(tiles):
            upcoming = scores_of(t + 1)
        probs, alphas = [], []
        for c in range(n_chains):
            s = scores[c] if mask is None else jnp.where(mask, scores[c], NEG)
            m_new = jnp.max(s, axis=0, keepdims=True)
            if t > 0:
                m_new = jnp.maximum(m[c], m_new)
                alphas.append(jnp.exp2(m[c] - m_new))
            m[c] = m_new
            probs.append(jnp.exp2(s - m_new).astype(BF16))
        for c in range(n_chains):
            pv = jnp.dot(v_of(c, j), probs[c], preferred_element_type=F32)
            acc[c] = pv if t == 0 else alphas[c] * acc[c] + pv
    return acc


def _normalized(acc):
    return acc[:HEAD_DIM] / acc[HEAD_DIM:HEAD_DIM + 1]


def _attn_params():
    return pltpu.CompilerParams(
        dimension_semantics=("arbitrary", "arbitrary"), vmem_limit_bytes=VMEM_LIMIT)


ATTN_TILE = 2 * Q_TILE


def _tile_pair_getters(k_ref, v_ref):
    def k_of(c, j):
        return jnp.concatenate([k_ref[0, c, 2 * j], k_ref[0, c, 2 * j + 1]], axis=0)

    def v_of(c, j):
        return jnp.concatenate([v_ref[0, c, 2 * j], v_ref[0, c, 2 * j + 1]], axis=1)

    return k_of, v_of


def _causal_then_past(i_static, causal):
    return [(i_static, causal)] + [(j, None) for j in range(i_static)]


def _moba_kernel(q_ref, k_ref, v_ref, o_ref):
    nh = q_ref.shape[1]
    k_of, v_of = _tile_pair_getters(k_ref, v_ref)
    for i_static in range(k_ref.shape[2] * Q_TILE // ATTN_TILE):
        @pl.when(pl.program_id(1) == i_static)
        def _(i_static=i_static):
            kr = lax.broadcasted_iota(jnp.int32, (ATTN_TILE, ATTN_TILE), 0)
            qc = lax.broadcasted_iota(jnp.int32, (ATTN_TILE, ATTN_TILE), 1)
            acc = _flash_unrolled(nh, lambda c: q_ref[0, c], k_of, v_of,
                                  _causal_then_past(i_static, kr <= qc))
            for h in range(nh):
                o_ref[0, h * HEAD_DIM:(h + 1) * HEAD_DIM, :] = (
                    _normalized(acc[h]).astype(o_ref.dtype))


def _moba_call(qa, ka, va):
    b, nh, _, s = qa.shape
    nt = s // Q_TILE
    return pl.pallas_call(
        _moba_kernel,
        grid=(b, s // ATTN_TILE),
        in_specs=[
            pl.BlockSpec((1, nh, OPERAND_WIDTH, ATTN_TILE), lambda bi, i: (bi, 0, 0, i)),
            pl.BlockSpec((1, nh, nt, Q_TILE, OPERAND_WIDTH), lambda bi, i: (bi, 0, 0, 0, 0)),
            pl.BlockSpec((1, nh, nt, VALUE_ROWS, Q_TILE), lambda bi, i: (bi, 0, 0, 0, 0)),
        ],
        out_specs=pl.BlockSpec((1, nh * HEAD_DIM, ATTN_TILE), lambda bi, i: (bi, 0, i)),
        out_shape=jax.ShapeDtypeStruct((b, nh * HEAD_DIM, s), BF16),
        compiler_params=_attn_params(),
        name="moba",
    )(qa, ka, va)


def _diff_kernel(q_ref, k_ref, v_ref, lq1_ref, lk1_ref, lq2_ref, lk2_ref, linit_ref, g_ref, o_ref,
                 qm_ref):
    nh = q_ref.shape[1]
    row = lax.broadcasted_iota(jnp.int32, q_ref.shape[2:], 0)
    for h in range(nh):
        q = q_ref[0, h]
        zero = jnp.zeros_like(q)
        qm_ref[h, :, :ATTN_TILE] = jnp.where(
            (row < DIFF_QK_DIM) | (row >= 2 * DIFF_QK_DIM), q, zero)
        qm_ref[h, :, ATTN_TILE:] = jnp.where(row >= DIFF_QK_DIM, q, zero)
    k_of, v_of = _tile_pair_getters(k_ref, v_ref)
    lam_init = linit_ref[...]
    lam = (jnp.exp(jnp.sum(lq1_ref[...] * lk1_ref[...], axis=-1, keepdims=True))
           - jnp.exp(jnp.sum(lq2_ref[...] * lk2_ref[...], axis=-1, keepdims=True))
           + lam_init)
    for i_static in range(k_ref.shape[2] * Q_TILE // ATTN_TILE):
        @pl.when(pl.program_id(1) == i_static)
        def _(i_static=i_static):
            kr = lax.broadcasted_iota(jnp.int32, (ATTN_TILE, 2 * ATTN_TILE), 0)
            qc = lax.broadcasted_iota(jnp.int32, (ATTN_TILE, 2 * ATTN_TILE), 1)
            causal = kr <= jnp.where(qc >= ATTN_TILE, qc - ATTN_TILE, qc)
            acc = _flash_unrolled(nh, lambda c: qm_ref[c], k_of, v_of,
                                  _causal_then_past(i_static, causal))
            for h in range(nh):
                w = _normalized(acc[h])
                o = w[:, :ATTN_TILE] - lam * w[:, ATTN_TILE:]
                ms = jnp.mean(o * o, axis=0, keepdims=True)
                y = (o * lax.rsqrt(ms + RMS_EPS)) * g_ref[...]
                o_ref[0, h * HEAD_DIM:(h + 1) * HEAD_DIM, :] = (
                    (y * (1.0 - lam_init)).astype(o_ref.dtype))


def _diff_call(qb, kb, vb, lam_q1, lam_k1, lam_q2, lam_k2, lam_init, g_col, layer):
    b, nh, _, s = qb.shape
    nt = s // Q_TILE
    per_layer = lambda arr: pl.BlockSpec(
        (None,) + arr.shape[1:], lambda bi, i: (layer,) + (0,) * (arr.ndim - 1))
    return pl.pallas_call(
        _diff_kernel,
        grid=(b, s // ATTN_TILE),
        in_specs=[
            pl.BlockSpec((1, nh, OPERAND_WIDTH, ATTN_TILE), lambda bi, i: (bi, 0, 0, i)),
            pl.BlockSpec((1, nh, nt, Q_TILE, OPERAND_WIDTH), lambda bi, i: (bi, 0, 0, 0, 0)),
            pl.BlockSpec((1, nh, nt, VALUE_ROWS, Q_TILE), lambda bi, i: (bi, 0, 0, 0, 0)),
            per_layer(lam_q1), per_layer(lam_k1), per_layer(lam_q2), per_layer(lam_k2),
            per_layer(lam_init), per_layer(g_col),
        ],
        out_specs=pl.BlockSpec((1, nh * HEAD_DIM, ATTN_TILE), lambda bi, i: (bi, 0, i)),
        out_shape=jax.ShapeDtypeStruct((b, nh * HEAD_DIM, s), BF16),
        scratch_shapes=[pltpu.VMEM((nh, OPERAND_WIDTH, 2 * ATTN_TILE), BF16)],
        compiler_params=_attn_params(),
        name="diffattn",
    )(qb, kb, vb, lam_q1, lam_k1, lam_q2, lam_k2, lam_init, g_col)


SWA_BLOCKS_PER_STEP = 4


def _swa_kernel(q_ref, k_ref, v_ref, sink_ref, o_ref):
    i = pl.program_id(1)
    kr = lax.broadcasted_iota(jnp.int32, (2 * SWA_BLOCK, SWA_BLOCK), 0)
    qc = lax.broadcasted_iota(jnp.int32, (2 * SWA_BLOCK, SWA_BLOCK), 1)
    dist = qc + SWA_BLOCK - kr
    in_win = (dist >= 0) & (dist < SWA_WINDOW)
    chains = [(b2, kvh) for b2 in range(SWA_BLOCKS_PER_STEP) for kvh in range(SWA_KV_HEADS)]
    blocks, bands = [], []
    for b2 in range(SWA_BLOCKS_PER_STEP):
        n = SWA_BLOCKS_PER_STEP * i + b2
        blocks.append((jnp.maximum(n - 1, 0), n))
        first_row = jnp.where(n > 0, 0, SWA_BLOCK)
        band = jnp.where(in_win & (kr >= first_row), 0.0, NEG)
        bands.append(jnp.concatenate([band] * SWA_GROUP, axis=1))
    scores = []
    for b2, kvh in chains:
        prev, n = blocks[b2]
        cols = slice(b2 * SWA_BLOCK, (b2 + 1) * SWA_BLOCK)
        q = jnp.concatenate([q_ref[0, kvh * SWA_GROUP + g, :, cols]
                             for g in range(SWA_GROUP)], axis=1)
        k = jnp.concatenate([k_ref[0, kvh, prev], k_ref[0, kvh, n]], axis=0)
        scores.append(jnp.dot(k, q, preferred_element_type=F32))
    probs, sink_terms = [], []
    for (b2, kvh), s in zip(chains, scores):
        s = s + bands[b2]
        sink = sink_ref[kvh] * LOG2E
        m = jnp.maximum(jnp.max(s, axis=0, keepdims=True), sink)
        probs.append(jnp.exp2(s - m).astype(BF16))
        sink_terms.append(jnp.exp2(sink - m))
    for (b2, kvh), p, sink_term in zip(chains, probs, sink_terms):
        prev, n = blocks[b2]
        cols = slice(b2 * SWA_BLOCK, (b2 + 1) * SWA_BLOCK)
        v = jnp.concatenate([v_ref[0, kvh, prev], v_ref[0, kvh, n]], axis=1)
        acc = jnp.dot(v, p, preferred_element_type=F32)
        o = acc[:HEAD_DIM] / (acc[HEAD_DIM:HEAD_DIM + 1] + sink_term)
        for g in range(SWA_GROUP):
            r0 = (kvh * SWA_GROUP + g) * HEAD_DIM
            o_ref[0, r0:r0 + HEAD_DIM, cols] = (
                o[:, g * SWA_BLOCK:(g + 1) * SWA_BLOCK].astype(o_ref.dtype))


def _swa_call(qc, kc, vc, sink_rows, layer):
    b, nq, _, s = qc.shape
    nkv = kc.shape[1]
    nsb = s // SWA_BLOCK
    step = SWA_BLOCKS_PER_STEP * SWA_BLOCK
    return pl.pallas_call(
        _swa_kernel,
        grid=(b, s // step),
        in_specs=[
            pl.BlockSpec((1, nq, OPERAND_WIDTH, step), lambda bi, i: (bi, 0, 0, i)),
            pl.BlockSpec((1, nkv, nsb, SWA_BLOCK, OPERAND_WIDTH), lambda bi, i: (bi, 0, 0, 0, 0)),
            pl.BlockSpec((1, nkv, nsb, VALUE_ROWS, SWA_BLOCK), lambda bi, i: (bi, 0, 0, 0, 0)),
            pl.BlockSpec((None,) + sink_rows.shape[1:], lambda bi, i: (layer, 0, 0, 0)),
        ],
        out_specs=pl.BlockSpec((1, nq * HEAD_DIM, step), lambda bi, i: (bi, 0, i)),
        out_shape=jax.ShapeDtypeStruct((b, nq * HEAD_DIM, s), BF16),
        compiler_params=_attn_params(),
        name="swa",
    )(qc, kc, vc, sink_rows)


def _mix_ffn_kernel(x_ref, ma_ref, mb_ref, mc_ref, woa_ref, wob_ref, woc_ref,
                    g_ref, wg_ref, wu_ref, wd_ref, gf_ref, o_ref, *, final):
    tdims = (((0,), (0,)), ((), ()))
    x = x_ref[0]
    x = x + lax.dot_general(ma_ref[0], woa_ref[...], tdims, preferred_element_type=F32)
    x = x + lax.dot_general(mb_ref[0], wob_ref[...], tdims, preferred_element_type=F32)
    x = x + lax.dot_general(mc_ref[0], woc_ref[...], tdims, preferred_element_type=F32)
    x = _swiglu_residual(x, g_ref[...], wg_ref, wu_ref, wd_ref)
    if final:
        x = _rmsnorm(x, gf_ref[...])
    o_ref[0] = x


def _mix_ffn_call(x, ma, mb, mc, wo, g, wg, wu, wd, gf, layer, final):
    b, s, _ = x.shape
    rows = FFN_ROWS
    a_rows, b_rows, c_rows = ma.shape[1], mb.shape[1], mc.shape[1]
    assert a_rows == b_rows and c_rows == a_rows + b_rows
    return pl.pallas_call(
        functools.partial(_mix_ffn_kernel, final=final),
        grid=(b, s // rows),
        in_specs=[
            pl.BlockSpec((1, rows, D_MODEL), lambda bi, i: (bi, i, 0)),
            pl.BlockSpec((1, a_rows, rows), lambda bi, i: (bi, 0, i)),
            pl.BlockSpec((1, b_rows, rows), lambda bi, i: (bi, 0, i)),
            pl.BlockSpec((1, c_rows, rows), lambda bi, i: (bi, 0, i)),
            _layer_resident(wo, layer, a_rows, 0),
            _layer_resident(wo, layer, b_rows, 1),
            _layer_resident(wo, layer, c_rows, 1),
            _layer_resident(g, layer),
            _layer_resident(wg, layer), _layer_resident(wu, layer), _layer_resident(wd, layer),
            _resident((1, D_MODEL)),
        ],
        out_specs=pl.BlockSpec((1, rows, D_MODEL), lambda bi, i: (bi, i, 0)),
        out_shape=jax.ShapeDtypeStruct(x.shape, F32),
        compiler_params=pltpu.CompilerParams(
            dimension_semantics=("arbitrary", "arbitrary"), vmem_limit_bytes=VMEM_LIMIT),
        name="mix_ffn2_final" if final else "mix_ffn2",
    )(x, ma, mb, mc, wo, wo, wo, g, wg, wu, wd, gf)


def _split3(x):
    x1 = x.astype(BF16).astype(F32)
    r = x - x1
    x2 = r.astype(BF16).astype(F32)
    x3 = (r - x2).astype(BF16).astype(F32)
    return [x1, x2, x3]


def _bias_rows_kernel(raw_ref, o_ref):
    raw = raw_ref[0]
    pieces = _split3(raw[0:1]) + 2 * _split3(raw[1:2])
    s = raw.shape[1]
    row = lax.broadcasted_iota(jnp.int32, (16, s), 0)
    blk = jnp.zeros((16, s), F32)
    for r, piece in enumerate(pieces):
        blk = jnp.where(row == r, piece, blk)
    lead = AUG_BIAS_ROW - HEAD_DIM
    o_ref[0] = jnp.concatenate(
        [jnp.zeros((lead, s), F32), blk, jnp.zeros((HEAD_DIM - lead - 16, s), F32)], axis=0)


def _operand_tables(s):
    slopes = _alibi_slopes()
    head_slopes = np.concatenate([
        slopes[SWA_Q_HEADS + DIFF_HEADS:],
        slopes[SWA_Q_HEADS:SWA_Q_HEADS + DIFF_HEADS],
        slopes[:SWA_Q_HEADS],
    ]).astype(np.float32)
    pos = jnp.arange(s, dtype=F32)
    sl = jnp.asarray(head_slopes * np.float32(LOG2E))[:, None]
    raw = jnp.zeros((N_Q_HEADS, 8, s), F32)
    raw = raw.at[:, 0, :].set(-sl * pos[None, :])
    raw = raw.at[:, 1, :].set(jnp.broadcast_to(sl, (N_Q_HEADS, s)))
    augq = pl.pallas_call(
        _bias_rows_kernel,
        grid=(N_Q_HEADS,),
        in_specs=[pl.BlockSpec((1, 8, s), lambda h: (h, 0, 0))],
        out_specs=pl.BlockSpec((1, HEAD_DIM, s), lambda h: (h, 0, 0)),
        out_shape=jax.ShapeDtypeStruct((N_Q_HEADS, HEAD_DIM, s), F32),
        name="bias_rows",
    )(raw)
    blk = jnp.arange(s, dtype=jnp.int32) // MOBA_BLOCK
    onehot = (blk[:, None] == jnp.arange(8, dtype=jnp.int32)[None, :]).astype(F32)
    ones = jnp.ones((s, 3), F32)
    base = jnp.broadcast_to((blk * MOBA_BLOCK).astype(F32)[:, None], (s, 3))
    off = jnp.broadcast_to((jnp.arange(s, dtype=jnp.int32) % MOBA_BLOCK).astype(F32)[:, None], (s, 3))
    augk = jnp.concatenate([
        jnp.zeros((s, HEAD_DIM), F32), onehot, ones, base, off,
        jnp.zeros((s, OPERAND_WIDTH - HEAD_DIM - 17), F32)], axis=1)
    return augq, augk


def kernel(x, norm_ffn1, w1_gate, w1_up, w1_down, norm_mix, w_in, lam_q1, lam_k1, lam_q2, lam_k2,
           diff_subln, sinks, w_out, norm_ffn2, w2_gate, w2_up, w2_down, final_norm):
    b, s, d = x.shape
    depth = w_in.shape[0]
    assert d == D_MODEL and s % ATTN_TILE == 0 and (b * s) % FFN_ROWS == 0
    augq, augk = _operand_tables(s)
    as_rows = lambda p: p.reshape(depth, 1, -1)
    g1, gm, g2 = as_rows(norm_ffn1), as_rows(norm_mix), as_rows(norm_ffn2)
    w1g, w1u, w1d = w1_gate.astype(BF16), w1_up.astype(BF16), w1_down.astype(BF16)
    w2g, w2u, w2d = w2_gate.astype(BF16), w2_up.astype(BF16), w2_down.astype(BF16)
    wt, wo = jnp.swapaxes(w_in, 1, 2).astype(BF16), w_out.astype(BF16)
    lq1, lk1, lq2, lk2 = as_rows(lam_q1), as_rows(lam_k1), as_rows(lam_q2), as_rows(lam_k2)
    lam_init = jnp.asarray([0.8 - 0.6 * math.exp(-0.3 * l) for l in range(depth)],
                           F32).reshape(depth, 1, 1)
    subln = diff_subln.reshape(depth, -1, 1)
    sink_rows = jnp.repeat(sinks.reshape(depth, SWA_KV_HEADS, 1, SWA_GROUP), SWA_BLOCK, axis=3)
    gf = final_norm.reshape(1, d)
    for l in range(depth):
        x = _ffn_call(x.reshape(b * s, d), g1, w1g, w1u, w1d, l).reshape(b, s, d)
        qa, qb, qc, ka, kb, kc, va, vb, vc = _proj_call(x, gm, wt, augk, augq, l)
        ma = _moba_call(qa, ka, va)
        mb = _diff_call(qb, kb, vb, lq1, lk1, lq2, lk2, lam_init, subln, l)
        mc = _swa_call(qc, kc, vc, sink_rows, l)
        x = _mix_ffn_call(x, ma, mb, mc, wo, g2, w2g, w2u, w2d, gf, l, final=(l == depth - 1))
    return x
```

```python
import functools
import math

import numpy as np
import jax
import jax.numpy as jnp
from jax import lax
from jax.experimental import pallas as pl
from jax.experimental.pallas import tpu as pltpu

F32 = jnp.float32
BF16 = jnp.bfloat16

D_MODEL = 1024
D_FF = 2816
HEAD_DIM = 64
MOBA_HEADS = 4
MOBA_BLOCK = 256
MOBA_TOPK = 3
DIFF_HEADS = 4
DIFF_QK_DIM = 32
SWA_Q_HEADS = 8
SWA_KV_HEADS = 2
SWA_GROUP = SWA_Q_HEADS // SWA_KV_HEADS
SWA_WINDOW = 128
SWA_BLOCK = 128
N_ALIBI = MOBA_HEADS + DIFF_HEADS + SWA_Q_HEADS
RMS_EPS = 1e-6
NEG = -1e30

N_Q_HEADS = MOBA_HEADS + DIFF_HEADS + SWA_Q_HEADS
OPERAND_WIDTH = 128
VALUE_ROWS = HEAD_DIM + 16
LOG2E = math.log2(math.e)
AUG_FLAG_ROW = HEAD_DIM
AUG_BIAS_ROW = HEAD_DIM + 8
Q_TILE = 256
FFN_ROWS = 512
VMEM_LIMIT = 56 * 1024 * 1024


def _alibi_slopes():
    n = N_ALIBI
    return 2.0 ** (-8.0 * (np.arange(n, dtype=np.float32) + 1.0) / n)


def _rmsnorm(x, g):
    ms = jnp.mean(x * x, axis=-1, keepdims=True)
    return (x * lax.rsqrt(ms + RMS_EPS)) * g


def _swiglu_residual(x, g, wg_ref, wu_ref, wd_ref):
    h = _rmsnorm(x, g).astype(BF16)
    gate = jnp.dot(h, wg_ref[...], preferred_element_type=F32)
    up = jnp.dot(h, wu_ref[...], preferred_element_type=F32)
    act = (gate * jax.nn.sigmoid(gate) * up).astype(BF16)
    return x + 0.5 * jnp.dot(act, wd_ref[...], preferred_element_type=F32)


def _resident(shape):
    zeros = (0,) * len(shape)
    return pl.BlockSpec(shape, lambda *_: zeros, pipeline_mode=pl.Buffered(1))


def _layer_resident(stacked, layer, rows=None, row_block=0):
    shape = stacked.shape[1:] if rows is None else (rows,) + stacked.shape[2:]
    index = (layer, row_block) + (0,) * (len(shape) - 1)
    return pl.BlockSpec((None,) + shape, lambda *_: index, pipeline_mode=pl.Buffered(1))


def _convert_plan(w, layer, n_steps, step_index):
    rows, cols = w.shape[1:]
    n_blocks = max(d for d in range(1, n_steps + 1) if rows % d == 0 and (rows // d) % 16 == 0)
    block = lambda *ids: jnp.minimum(step_index(*ids), n_blocks - 1)
    in_spec = pl.BlockSpec((None, rows // n_blocks, cols), lambda *ids: (layer, block(*ids), 0))
    out_spec = pl.BlockSpec((rows // n_blocks, cols), lambda *ids: (block(*ids), 0))
    return in_spec, out_spec, jax.ShapeDtypeStruct((rows, cols), BF16)


def _convert_blocks(in_refs, out_refs):
    for src, dst in zip(in_refs, out_refs):
        dst[...] = src[...].astype(BF16)


def _ffn_kernel(x_ref, g_ref, wg_ref, wu_ref, wd_ref, o_ref):
    half = x_ref.shape[0] // 2
    for r in range(2):
        rows = slice(r * half, (r + 1) * half)
        o_ref[rows, :] = _swiglu_residual(x_ref[rows, :], g_ref[...], wg_ref, wu_ref, wd_ref)


def _ffn_call(x2, g, wg, wu, wd, layer):
    t = x2.shape[0]
    return pl.pallas_call(
        _ffn_kernel,
        grid=(t // FFN_ROWS,),
        in_specs=[
            pl.BlockSpec((FFN_ROWS, D_MODEL), lambda i: (i, 0)),
            _layer_resident(g, layer),
            _resident(wg.shape), _resident(wu.shape), _resident(wd.shape),
        ],
        out_specs=pl.BlockSpec((FFN_ROWS, D_MODEL), lambda i: (i, 0)),
        out_shape=jax.ShapeDtypeStruct(x2.shape, F32),
        compiler_params=pltpu.CompilerParams(
            dimension_semantics=("arbitrary",), vmem_limit_bytes=VMEM_LIMIT),
        name="ffn1",
    )(x2, g, wg, wu, wd)


def _split_hi_lo(x):
    hi = x.astype(BF16)
    lo = (x - hi.astype(F32)).astype(BF16)
    return hi, lo


_AQ0 = 0
_AK0 = _AQ0 + MOBA_HEADS * HEAD_DIM
_AV0 = _AK0 + MOBA_HEADS * HEAD_DIM
_BQ0 = _AV0 + MOBA_HEADS * HEAD_DIM
_BK0 = _BQ0 + DIFF_HEADS * HEAD_DIM
_BV0 = _BK0 + DIFF_HEADS * HEAD_DIM
_CQ0 = _BV0 + DIFF_HEADS * HEAD_DIM
_CK0 = _CQ0 + SWA_Q_HEADS * HEAD_DIM
_CV0 = _CK0 + SWA_KV_HEADS * HEAD_DIM
PROJ_ROWS = 2 * Q_TILE


def _proj_kernel(x_ref, g_ref, wt_ref, augk_ref, augq_ref, cg_ref, cu_ref, cd_ref,
                 qa_ref, qb_ref, qc_ref, ka_ref, kb_ref, kc_ref, va_ref, vb_ref, vc_ref,
                 og_ref, ou_ref, od_ref, kmean_ref):
    i = pl.program_id(1)
    t = x_ref.shape[1]
    blocks_per_step = t // MOBA_BLOCK
    _convert_blocks((cg_ref, cu_ref, cd_ref), (og_ref, ou_ref, od_ref))

    @pl.when(i == 0)
    def _():
        kmean_ref[...] = jnp.zeros_like(kmean_ref)

    h = _rmsnorm(x_ref[0], g_ref[...]).astype(BF16)
    p = lax.dot_general(wt_ref[...], h, (((1,), (1,)), ((), ())),
                        preferred_element_type=F32)
    augk = augk_ref[...]
    lane = lax.broadcasted_iota(jnp.int32, (t, OPERAND_WIDTH), 1)

    def key_pair(r0):
        kt = p[r0:r0 + 2 * HEAD_DIM].T
        even = jnp.where(lane < HEAD_DIM, kt, augk).astype(BF16)
        odd = jnp.where(lane < HEAD_DIM, pltpu.roll(kt, HEAD_DIM, 1), augk).astype(BF16)
        return kt, (even, odd)

    def store_tiles(ref, hh, slab, rows, axis):
        for n in range(slab.shape[axis] // rows):
            piece = slab[n * rows:(n + 1) * rows] if axis == 0 else slab[:, n * rows:(n + 1) * rows]
            ref[0, hh, n] = piece

    for pair in range(MOBA_HEADS // 2):
        kt, slabs = key_pair(_AK0 + pair * 2 * HEAD_DIM)
        for blk in range(blocks_per_step):
            kmean_ref[pair, pl.ds(blocks_per_step * i + blk, 1), :] = jnp.mean(
                kt[blk * MOBA_BLOCK:(blk + 1) * MOBA_BLOCK], axis=0, keepdims=True)
        for par in range(2):
            store_tiles(ka_ref, 2 * pair + par, slabs[par], Q_TILE, 0)
    for pair in range(DIFF_HEADS // 2):
        _, slabs = key_pair(_BK0 + pair * 2 * HEAD_DIM)
        for par in range(2):
            store_tiles(kb_ref, 2 * pair + par, slabs[par], Q_TILE, 0)
    for pair in range(SWA_KV_HEADS // 2):
        _, slabs = key_pair(_CK0 + pair * 2 * HEAD_DIM)
        for par in range(2):
            store_tiles(kc_ref, 2 * pair + par, slabs[par], SWA_BLOCK, 0)

    ones_rows = jnp.where(
        lax.broadcasted_iota(jnp.int32, (VALUE_ROWS - HEAD_DIM, t), 0) == 0, 1.0, 0.0)

    def v_slab(r0):
        return jnp.concatenate([p[r0:r0 + HEAD_DIM], ones_rows], axis=0).astype(BF16)

    for hh in range(MOBA_HEADS):
        store_tiles(va_ref, hh, v_slab(_AV0 + hh * HEAD_DIM), Q_TILE, 1)
    for hh in range(DIFF_HEADS):
        store_tiles(vb_ref, hh, v_slab(_BV0 + hh * HEAD_DIM), Q_TILE, 1)
    for hh in range(SWA_KV_HEADS):
        store_tiles(vc_ref, hh, v_slab(_CV0 + hh * HEAD_DIM), SWA_BLOCK, 1)

    n_blocks = kmean_ref.shape[1]
    jrow = lax.broadcasted_iota(jnp.int32, (n_blocks, t), 0)
    qcol = lax.broadcasted_iota(jnp.int32, (n_blocks, t), 1)
    qblk = blocks_per_step * i + qcol // MOBA_BLOCK
    zeros_rows = jnp.zeros((HEAD_DIM, t), F32)
    moba_scale = HEAD_DIM ** -0.5 * LOG2E
    for hh in range(MOBA_HEADS):
        qf = p[_AQ0 + hh * HEAD_DIM:_AQ0 + (hh + 1) * HEAD_DIM]
        q_pad = jnp.concatenate([qf, zeros_rows] if hh % 2 == 0 else [zeros_rows, qf], axis=0)
        km_hi, km_lo = _split_hi_lo(kmean_ref[hh // 2])
        q_hi, q_lo = _split_hi_lo(q_pad)
        gate = (jnp.dot(km_hi, q_hi, preferred_element_type=F32)
                + jnp.dot(km_hi, q_lo, preferred_element_type=F32)
                + jnp.dot(km_lo, q_hi, preferred_element_type=F32))
        rank = jnp.zeros((n_blocks, t), F32)
        for m in range(n_blocks):
            gm = gate[m:m + 1, :]
            beats = (gm > gate) | ((gm == gate) & (m < jrow))
            rank = rank + jnp.where(beats & (m < qblk), 1.0, 0.0)
        chosen = ((rank < MOBA_TOPK) & (jrow < qblk)) | (jrow == qblk)
        flags = jnp.where(chosen, 0.0, NEG)
        aug = augq_ref[hh] + jnp.concatenate(
            [flags, jnp.zeros((HEAD_DIM - n_blocks, t), F32)], axis=0)
        qa_ref[0, hh, :HEAD_DIM, :] = (qf * moba_scale).astype(BF16)
        qa_ref[0, hh, HEAD_DIM:, :] = aug.astype(BF16)
    diff_scale = DIFF_QK_DIM ** -0.5 * LOG2E
    for hh in range(DIFF_HEADS):
        qf = p[_BQ0 + hh * HEAD_DIM:_BQ0 + (hh + 1) * HEAD_DIM]
        qb_ref[0, hh, :HEAD_DIM, :] = (qf * diff_scale).astype(BF16)
        qb_ref[0, hh, HEAD_DIM:, :] = augq_ref[MOBA_HEADS + hh].astype(BF16)
    swa_scale = HEAD_DIM ** -0.5 * LOG2E
    for hh in range(SWA_Q_HEADS):
        qf = p[_CQ0 + hh * HEAD_DIM:_CQ0 + (hh + 1) * HEAD_DIM]
        qc_ref[0, hh, :HEAD_DIM, :] = (qf * swa_scale).astype(BF16)
        qc_ref[0, hh, HEAD_DIM:, :] = augq_ref[MOBA_HEADS + DIFF_HEADS + hh].astype(BF16)


def _proj_call(x, g, wt, augk, augq, ffn_weights, layer):
    b, s, _ = x.shape
    nt = s // Q_TILE
    nsb = s // SWA_BLOCK
    rows = PROJ_ROWS
    tiles = rows // Q_TILE
    sblocks = rows // SWA_BLOCK
    steps = s // rows
    conv = [_convert_plan(w, layer, b * steps, lambda bi, i: bi * steps + i) for w in ffn_weights]
    out_shape = (
        jax.ShapeDtypeStruct((b, MOBA_HEADS, OPERAND_WIDTH, s), BF16),
        jax.ShapeDtypeStruct((b, DIFF_HEADS, OPERAND_WIDTH, s), BF16),
        jax.ShapeDtypeStruct((b, SWA_Q_HEADS, OPERAND_WIDTH, s), BF16),
        jax.ShapeDtypeStruct((b, MOBA_HEADS, nt, Q_TILE, OPERAND_WIDTH), BF16),
        jax.ShapeDtypeStruct((b, DIFF_HEADS, nt, Q_TILE, OPERAND_WIDTH), BF16),
        jax.ShapeDtypeStruct((b, SWA_KV_HEADS, nsb, SWA_BLOCK, OPERAND_WIDTH), BF16),
        jax.ShapeDtypeStruct((b, MOBA_HEADS, nt, VALUE_ROWS, Q_TILE), BF16),
        jax.ShapeDtypeStruct((b, DIFF_HEADS, nt, VALUE_ROWS, Q_TILE), BF16),
        jax.ShapeDtypeStruct((b, SWA_KV_HEADS, nsb, VALUE_ROWS, SWA_BLOCK), BF16),
    )
    out_specs = (
        pl.BlockSpec((1, MOBA_HEADS, OPERAND_WIDTH, rows), lambda bi, i: (bi, 0, 0, i)),
        pl.BlockSpec((1, DIFF_HEADS, OPERAND_WIDTH, rows), lambda bi, i: (bi, 0, 0, i)),
        pl.BlockSpec((1, SWA_Q_HEADS, OPERAND_WIDTH, rows), lambda bi, i: (bi, 0, 0, i)),
        pl.BlockSpec((1, MOBA_HEADS, tiles, Q_TILE, OPERAND_WIDTH), lambda bi, i: (bi, 0, i, 0, 0)),
        pl.BlockSpec((1, DIFF_HEADS, tiles, Q_TILE, OPERAND_WIDTH), lambda bi, i: (bi, 0, i, 0, 0)),
        pl.BlockSpec((1, SWA_KV_HEADS, sblocks, SWA_BLOCK, OPERAND_WIDTH),
                     lambda bi, i: (bi, 0, i, 0, 0)),
        pl.BlockSpec((1, MOBA_HEADS, tiles, VALUE_ROWS, Q_TILE), lambda bi, i: (bi, 0, i, 0, 0)),
        pl.BlockSpec((1, DIFF_HEADS, tiles, VALUE_ROWS, Q_TILE), lambda bi, i: (bi, 0, i, 0, 0)),
        pl.BlockSpec((1, SWA_KV_HEADS, sblocks, VALUE_ROWS, SWA_BLOCK),
                     lambda bi, i: (bi, 0, i, 0, 0)),
    )
    outs = pl.pallas_call(
        _proj_kernel,
        grid=(b, steps),
        in_specs=[
            pl.BlockSpec((1, rows, D_MODEL), lambda bi, i: (bi, i, 0)),
            _layer_resident(g, layer),
            _layer_resident(wt, layer),
            pl.BlockSpec((rows, OPERAND_WIDTH), lambda bi, i: (i, 0)),
            pl.BlockSpec((N_Q_HEADS, HEAD_DIM, rows), lambda bi, i: (0, 0, i)),
        ] + [c[0] for c in conv],
        out_specs=out_specs + tuple(c[1] for c in conv),
        out_shape=out_shape + tuple(c[2] for c in conv),
        scratch_shapes=[pltpu.VMEM((MOBA_HEADS // 2, s // MOBA_BLOCK, OPERAND_WIDTH), F32)],
        compiler_params=pltpu.CompilerParams(
            dimension_semantics=("arbitrary", "arbitrary"), vmem_limit_bytes=VMEM_LIMIT),
        name="proj",
    )(x, g, wt, augk, augq, *ffn_weights)
    return outs[:9], outs[9:]


def _flash_unrolled(n_chains, q_of, k_of, v_of, tiles):
    def scores_of(t):
        j, _ = tiles[t]
        return [jnp.dot(k_of(c, j), q_of(c), preferred_element_type=F32)
                for c in range(n_chains)]

    m = [None] * n_chains
    acc = [None] * n_chains
    upcoming = scores_of(0)
    for t, (j, mask) in enumerate(tiles):
        scores = upcoming
        if t + 1 < len(tiles):
            upcoming = scores_of(t + 1)
        probs, alphas = [], []
        for c in range(n_chains):
            s = scores[c] if mask is None else jnp.where(mask, scores[c], NEG)
            m_new = jnp.max(s, axis=0, keepdims=True)
            if t > 0:
                m_new = jnp.maximum(m[c], m_new)
                alphas.append(jnp.exp2(m[c] - m_new))
            m[c] = m_new
            probs.append(jnp.exp2(s - m_new).astype(BF16))
        for c in range(n_chains):
            pv = jnp.dot(v_of(c, j), probs[c], preferred_element_type=F32)
            acc[c] = pv if t == 0 else alphas[c] * acc[c] + pv
    return acc


def _normalized(acc):
    return acc[:HEAD_DIM] / acc[HEAD_DIM:HEAD_DIM + 1]


def _attn_params():
    return pltpu.CompilerParams(
        dimension_semantics=("arbitrary", "arbitrary"), vmem_limit_bytes=VMEM_LIMIT)


ATTN_TILE = 2 * Q_TILE


def _tile_pair_getters(k_ref, v_ref):
    def k_of(c, j):
        return jnp.concatenate([k_ref[0, c, 2 * j], k_ref[0, c, 2 * j + 1]], axis=0)

    def v_of(c, j):
        return jnp.concatenate([v_ref[0, c, 2 * j], v_ref[0, c, 2 * j + 1]], axis=1)

    return k_of, v_of


def _causal_then_past(i_static, causal):
    return [(i_static, causal)] + [(j, None) for j in range(i_static)]


def _moba_kernel(q_ref, k_ref, v_ref, *refs):
    n_conv = len(refs) // 2
    o_ref = refs[n_conv]
    _convert_blocks(refs[:n_conv], refs[n_conv + 1:])
    nh = q_ref.shape[1]
    k_of, v_of = _tile_pair_getters(k_ref, v_ref)
    for i_static in range(k_ref.shape[2] * Q_TILE // ATTN_TILE):
        @pl.when(pl.program_id(1) == i_static)
        def _(i_static=i_static):
            kr = lax.broadcasted_iota(jnp.int32, (ATTN_TILE, ATTN_TILE), 0)
            qc = lax.broadcasted_iota(jnp.int32, (ATTN_TILE, ATTN_TILE), 1)
            acc = _flash_unrolled(nh, lambda c: q_ref[0, c], k_of, v_of,
                                  _causal_then_past(i_static, kr <= qc))
            for h in range(nh):
                o_ref[0, h * HEAD_DIM:(h + 1) * HEAD_DIM, :] = (
                    _normalized(acc[h]).astype(o_ref.dtype))


def _moba_call(qa, ka, va, ffn_weights, layer):
    b, nh, _, s = qa.shape
    nt = s // Q_TILE
    steps = s // ATTN_TILE
    conv = [_convert_plan(w, layer, b * steps, lambda bi, i: bi * steps + i) for w in ffn_weights]
    outs = pl.pallas_call(
        _moba_kernel,
        grid=(b, steps),
        in_specs=[
            pl.BlockSpec((1, nh, OPERAND_WIDTH, ATTN_TILE), lambda bi, i: (bi, 0, 0, i)),
            pl.BlockSpec((1, nh, nt, Q_TILE, OPERAND_WIDTH), lambda bi, i: (bi, 0, 0, 0, 0)),
            pl.BlockSpec((1, nh, nt, VALUE_ROWS, Q_TILE), lambda bi, i: (bi, 0, 0, 0, 0)),
        ] + [c[0] for c in conv],
        out_specs=(pl.BlockSpec((1, nh * HEAD_DIM, ATTN_TILE), lambda bi, i: (bi, 0, i)),)
        + tuple(c[1] for c in conv),
        out_shape=(jax.ShapeDtypeStruct((b, nh * HEAD_DIM, s), BF16),) + tuple(c[2] for c in conv),
        compiler_params=_attn_params(),
        name="moba",
    )(qa, ka, va, *ffn_weights)
    return outs[0], outs[1:]


def _diff_kernel(q_ref, k_ref, v_ref, lq1_ref, lk1_ref, lq2_ref, lk2_ref, linit_ref, g_ref, o_ref,
                 qm_ref):
    nh = q_ref.shape[1]
    row = lax.broadcasted_iota(jnp.int32, q_ref.shape[2:], 0)
    for h in range(nh):
        q = q_ref[0, h]
        zero = jnp.zeros_like(q)
        qm_ref[h, :, :ATTN_TILE] = jnp.where(
            (row < DIFF_QK_DIM) | (row >= 2 * DIFF_QK_DIM), q, zero)
        qm_ref[h, :, ATTN_TILE:] = jnp.where(row >= DIFF_QK_DIM, q, zero)
    k_of, v_of = _tile_pair_getters(k_ref, v_ref)
    lam_init = linit_ref[...]
    lam = (jnp.exp(jnp.sum(lq1_ref[...] * lk1_ref[...], axis=-1, keepdims=True))
           - jnp.exp(jnp.sum(lq2_ref[...] * lk2_ref[...], axis=-1, keepdims=True))
           + lam_init)
    for i_static in range(k_ref.shape[2] * Q_TILE // ATTN_TILE):
        @pl.when(pl.program_id(1) == i_static)
        def _(i_static=i_static):
            kr = lax.broadcasted_iota(jnp.int32, (ATTN_TILE, 2 * ATTN_TILE), 0)
            qc = lax.broadcasted_iota(jnp.int32, (ATTN_TILE, 2 * ATTN_TILE), 1)
            causal = kr <= jnp.where(qc >= ATTN_TILE, qc - ATTN_TILE, qc)
            acc = _flash_unrolled(nh, lambda c: qm_ref[c], k_of, v_of,
                                  _causal_then_past(i_static, causal))
            for h in range(nh):
                w = _normalized(acc[h])
                o = w[:, :ATTN_TILE] - lam * w[:, ATTN_TILE:]
                ms = jnp.mean(o * o, axis=0, keepdims=True)
                y = (o * lax.rsqrt(ms + RMS_EPS)) * g_ref[...]
                o_ref[0, h * HEAD_DIM:(h + 1) * HEAD_DIM, :] = (
                    (y * (1.0 - lam_init)).astype(o_ref.dtype))


def _diff_call(qb, kb, vb, lam_q1, lam_k1, lam_q2, lam_k2, lam_init, g_col, layer):
    b, nh, _, s = qb.shape
    nt = s // Q_TILE
    per_layer = lambda arr: pl.BlockSpec(
        (None,) + arr.shape[1:], lambda bi, i: (layer,) + (0,) * (arr.ndim - 1))
    return pl.pallas_call(
        _diff_kernel,
        grid=(b, s // ATTN_TILE),
        in_specs=[
            pl.BlockSpec((1, nh, OPERAND_WIDTH, ATTN_TILE), lambda bi, i: (bi, 0, 0, i)),
            pl.BlockSpec((1, nh, nt, Q_TILE, OPERAND_WIDTH), lambda bi, i: (bi, 0, 0, 0, 0)),
            pl.BlockSpec((1, nh, nt, VALUE_ROWS, Q_TILE), lambda bi, i: (bi, 0, 0, 0, 0)),
            per_layer(lam_q1), per_layer(lam_k1), per_layer(lam_q2), per_layer(lam_k2),
            per_layer(lam_init), per_layer(g_col),
        ],
        out_specs=pl.BlockSpec((1, nh * HEAD_DIM, ATTN_TILE), lambda bi, i: (bi, 0, i)),
        out_shape=jax.ShapeDtypeStruct((b, nh * HEAD_DIM, s), BF16),
        scratch_shapes=[pltpu.VMEM((nh, OPERAND_WIDTH, 2 * ATTN_TILE), BF16)],
        compiler_params=_attn_params(),
        name="diffattn",
    )(qb, kb, vb, lam_q1, lam_k1, lam_q2, lam_k2, lam_init, g_col)


SWA_BLOCKS_PER_STEP = 4


def _swa_kernel(q_ref, k_ref, v_ref, sink_ref, o_ref):
    i = pl.program_id(1)
    kr = lax.broadcasted_iota(jnp.int32, (2 * SWA_BLOCK, SWA_BLOCK), 0)
    qc = lax.broadcasted_iota(jnp.int32, (2 * SWA_BLOCK, SWA_BLOCK), 1)
    dist = qc + SWA_BLOCK - kr
    in_win = (dist >= 0) & (dist < SWA_WINDOW)
    chains = [(b2, kvh) for b2 in range(SWA_BLOCKS_PER_STEP) for kvh in range(SWA_KV_HEADS)]
    blocks, bands = [], []
    for b2 in range(SWA_BLOCKS_PER_STEP):
        n = SWA_BLOCKS_PER_STEP * i + b2
        blocks.append((jnp.maximum(n - 1, 0), n))
        first_row = jnp.where(n > 0, 0, SWA_BLOCK)
        band = jnp.where(in_win & (kr >= first_row), 0.0, NEG)
        bands.append(jnp.concatenate([band] * SWA_GROUP, axis=1))
    scores = []
    for b2, kvh in chains:
        prev, n = blocks[b2]
        cols = slice(b2 * SWA_BLOCK, (b2 + 1) * SWA_BLOCK)
        q = jnp.concatenate([q_ref[0, kvh * SWA_GROUP + g, :, cols]
                             for g in range(SWA_GROUP)], axis=1)
        k = jnp.concatenate([k_ref[0, kvh, prev], k_ref[0, kvh, n]], axis=0)
        scores.append(jnp.dot(k, q, preferred_element_type=F32))
    probs, sink_terms = [], []
    for (b2, kvh), s in zip(chains, scores):
        s = s + bands[b2]
        sink = sink_ref[kvh] * LOG2E
        m = jnp.maximum(jnp.max(s, axis=0, keepdims=True), sink)
        probs.append(jnp.exp2(s - m).astype(BF16))
        sink_terms.append(jnp.exp2(sink - m))
    for (b2, kvh), p, sink_term in zip(chains, probs, sink_terms):
        prev, n = blocks[b2]
        cols = slice(b2 * SWA_BLOCK, (b2 + 1) * SWA_BLOCK)
        v = jnp.concatenate([v_ref[0, kvh, prev], v_ref[0, kvh, n]], axis=1)
        acc = jnp.dot(v, p, preferred_element_type=F32)
        o = acc[:HEAD_DIM] / (acc[HEAD_DIM:HEAD_DIM + 1] + sink_term)
        for g in range(SWA_GROUP):
            r0 = (kvh * SWA_GROUP + g) * HEAD_DIM
            o_ref[0, r0:r0 + HEAD_DIM, cols] = (
                o[:, g * SWA_BLOCK:(g + 1) * SWA_BLOCK].astype(o_ref.dtype))


def _swa_call(qc, kc, vc, sink_rows, layer):
    b, nq, _, s = qc.shape
    nkv = kc.shape[1]
    nsb = s // SWA_BLOCK
    step = SWA_BLOCKS_PER_STEP * SWA_BLOCK
    return pl.pallas_call(
        _swa_kernel,
        grid=(b, s // step),
        in_specs=[
            pl.BlockSpec((1, nq, OPERAND_WIDTH, step), lambda bi, i: (bi, 0, 0, i)),
            pl.BlockSpec((1, nkv, nsb, SWA_BLOCK, OPERAND_WIDTH), lambda bi, i: (bi, 0, 0, 0, 0)),
            pl.BlockSpec((1, nkv, nsb, VALUE_ROWS, SWA_BLOCK), lambda bi, i: (bi, 0, 0, 0, 0)),
            pl.BlockSpec((None,) + sink_rows.shape[1:], lambda bi, i: (layer, 0, 0, 0)),
        ],
        out_specs=pl.BlockSpec((1, nq * HEAD_DIM, step), lambda bi, i: (bi, 0, i)),
        out_shape=jax.ShapeDtypeStruct((b, nq * HEAD_DIM, s), BF16),
        compiler_params=_attn_params(),
        name="swa",
    )(qc, kc, vc, sink_rows)


def _mix_ffn_kernel(x_ref, ma_ref, mb_ref, mc_ref, woa_ref, wob_ref, woc_ref,
                    g_ref, wg_ref, wu_ref, wd_ref, gf_ref, o_ref, *, final):
    tdims = (((0,), (0,)), ((), ()))
    half = x_ref.shape[1] // 2
    for r in range(2):
        rows = slice(r * half, (r + 1) * half)
        x = x_ref[0, rows, :]
        x = x + lax.dot_general(ma_ref[0, :, rows], woa_ref[...], tdims, preferred_element_type=F32)
        x = x + lax.dot_general(mb_ref[0, :, rows], wob_ref[...], tdims, preferred_element_type=F32)
        x = x + lax.dot_general(mc_ref[0, :, rows], woc_ref[...], tdims, preferred_element_type=F32)
        x = _swiglu_residual(x, g_ref[...], wg_ref, wu_ref, wd_ref)
        if final:
            x = _rmsnorm(x, gf_ref[...])
        o_ref[0, rows, :] = x


def _mix_ffn_call(x, ma, mb, mc, wo, g, wg, wu, wd, gf, layer, final):
    b, s, _ = x.shape
    rows = FFN_ROWS
    a_rows, b_rows, c_rows = ma.shape[1], mb.shape[1], mc.shape[1]
    assert a_rows == b_rows and c_rows == a_rows + b_rows
    return pl.pallas_call(
        functools.partial(_mix_ffn_kernel, final=final),
        grid=(b, s // rows),
        in_specs=[
            pl.BlockSpec((1, rows, D_MODEL), lambda bi, i: (bi, i, 0)),
            pl.BlockSpec((1, a_rows, rows), lambda bi, i: (bi, 0, i)),
            pl.BlockSpec((1, b_rows, rows), lambda bi, i: (bi, 0, i)),
            pl.BlockSpec((1, c_rows, rows), lambda bi, i: (bi, 0, i)),
            _layer_resident(wo, layer, a_rows, 0),
            _layer_resident(wo, layer, b_rows, 1),
            _layer_resident(wo, layer, c_rows, 1),
            _layer_resident(g, layer),
            _resident(wg.shape), _resident(wu.shape), _resident(wd.shape),
            _resident((1, D_MODEL)),
        ],
        out_specs=pl.BlockSpec((1, rows, D_MODEL), lambda bi, i: (bi, i, 0)),
        out_shape=jax.ShapeDtypeStruct(x.shape, F32),
        compiler_params=pltpu.CompilerParams(
            dimension_semantics=("arbitrary", "arbitrary"), vmem_limit_bytes=VMEM_LIMIT),
        name="mix_ffn2_final" if final else "mix_ffn2",
    )(x, ma, mb, mc, wo, wo, wo, g, wg, wu, wd, gf)


def _split3(x):
    x1 = x.astype(BF16).astype(F32)
    r = x - x1
    x2 = r.astype(BF16).astype(F32)
    x3 = (r - x2).astype(BF16).astype(F32)
    return [x1, x2, x3]


def _bias_rows_kernel(raw_ref, o_ref):
    s = raw_ref.shape[2]
    row = lax.broadcasted_iota(jnp.int32, (16, s), 0)
    lead = AUG_BIAS_ROW - HEAD_DIM
    for hh in range(raw_ref.shape[0]):
        raw = raw_ref[hh]
        pieces = _split3(raw[0:1]) + 2 * _split3(raw[1:2])
        blk = jnp.zeros((16, s), F32)
        for r, piece in enumerate(pieces):
            blk = jnp.where(row == r, piece, blk)
        o_ref[hh] = jnp.concatenate(
            [jnp.zeros((lead, s), F32), blk, jnp.zeros((HEAD_DIM - lead - 16, s), F32)], axis=0)


def _operand_tables(s):
    slopes = _alibi_slopes()
    head_slopes = np.concatenate([
        slopes[SWA_Q_HEADS + DIFF_HEADS:],
        slopes[SWA_Q_HEADS:SWA_Q_HEADS + DIFF_HEADS],
        slopes[:SWA_Q_HEADS],
    ]).astype(np.float32)
    pos = jnp.arange(s, dtype=F32)
    sl = jnp.asarray(head_slopes * np.float32(LOG2E))[:, None]
    raw = jnp.zeros((N_Q_HEADS, 8, s), F32)
    raw = raw.at[:, 0, :].set(-sl * pos[None, :])
    raw = raw.at[:, 1, :].set(jnp.broadcast_to(sl, (N_Q_HEADS, s)))
    augq = pl.pallas_call(
        _bias_rows_kernel,
        out_shape=jax.ShapeDtypeStruct((N_Q_HEADS, HEAD_DIM, s), F32),
        name="bias_rows",
    )(raw)
    blk = jnp.arange(s, dtype=jnp.int32) // MOBA_BLOCK
    onehot = (blk[:, None] == jnp.arange(8, dtype=jnp.int32)[None, :]).astype(F32)
    ones = jnp.ones((s, 3), F32)
    base = jnp.broadcast_to((blk * MOBA_BLOCK).astype(F32)[:, None], (s, 3))
    off = jnp.broadcast_to((jnp.arange(s, dtype=jnp.int32) % MOBA_BLOCK).astype(F32)[:, None], (s, 3))
    augk = jnp.concatenate([
        jnp.zeros((s, HEAD_DIM), F32), onehot, ones, base, off,
        jnp.zeros((s, OPERAND_WIDTH - HEAD_DIM - 17), F32)], axis=1)
    return augq, augk


def kernel(x, norm_ffn1, w1_gate, w1_up, w1_down, norm_mix, w_in, lam_q1, lam_k1, lam_q2, lam_k2,
           diff_subln, sinks, w_out, norm_ffn2, w2_gate, w2_up, w2_down, final_norm):
    b, s, d = x.shape
    depth = w_in.shape[0]
    assert d == D_MODEL and s % ATTN_TILE == 0 and (b * s) % FFN_ROWS == 0
    augq, augk = _operand_tables(s)
    as_rows = lambda p: p.reshape(depth, 1, -1)
    g1, gm, g2 = as_rows(norm_ffn1), as_rows(norm_mix), as_rows(norm_ffn2)
    ffn1_f32, ffn2_f32 = (w1_gate, w1_up, w1_down), (w2_gate, w2_up, w2_down)
    wt, wo = jnp.swapaxes(w_in, 1, 2).astype(BF16), w_out.astype(BF16)
    lq1, lk1, lq2, lk2 = as_rows(lam_q1), as_rows(lam_k1), as_rows(lam_q2), as_rows(lam_k2)
    lam_init = jnp.asarray([0.8 - 0.6 * math.exp(-0.3 * l) for l in range(depth)],
                           F32).reshape(depth, 1, 1)
    subln = diff_subln.reshape(depth, -1, 1)
    sink_rows = jnp.repeat(sinks.reshape(depth, SWA_KV_HEADS, 1, SWA_GROUP), SWA_BLOCK, axis=3)
    gf = final_norm.reshape(1, d)
    ffn1_w = tuple(w[0].astype(BF16) for w in ffn1_f32)
    for l in range(depth):
        x = _ffn_call(x.reshape(b * s, d), g1, *ffn1_w, l).reshape(b, s, d)
        (qa, qb, qc, ka, kb, kc, va, vb, vc), ffn2_w = _proj_call(
            x, gm, wt, augk, augq, ffn2_f32, l)
        ma, ffn1_w = _moba_call(qa, ka, va, ffn1_f32 if l + 1 < depth else (), l + 1)
        mb = _diff_call(qb, kb, vb, lq1, lk1, lq2, lk2, lam_init, subln, l)
        mc = _swa_call(qc, kc, vc, sink_rows, l)
        x = _mix_ffn_call(x, ma, mb, mc, wo, g2, *ffn2_w, gf, l, final=(l == depth - 1))
    return x
```

```python
import functools
import math

import numpy as np
import jax
import jax.numpy as jnp
from jax import lax
from jax.experimental import pallas as pl
from jax.experimental.pallas import tpu as pltpu

F32 = jnp.float32
BF16 = jnp.bfloat16

D_MODEL = 1024
D_FF = 2816
HEAD_DIM = 64
MOBA_HEADS = 4
MOBA_BLOCK = 256
MOBA_TOPK = 3
DIFF_HEADS = 4
DIFF_QK_DIM = 32
SWA_Q_HEADS = 8
SWA_KV_HEADS = 2
SWA_GROUP = SWA_Q_HEADS // SWA_KV_HEADS
SWA_WINDOW = 128
SWA_BLOCK = 128
N_ALIBI = MOBA_HEADS + DIFF_HEADS + SWA_Q_HEADS
RMS_EPS = 1e-6
NEG = -1e30

N_Q_HEADS = MOBA_HEADS + DIFF_HEADS + SWA_Q_HEADS
OPERAND_WIDTH = 128
VALUE_ROWS = HEAD_DIM + 16
LOG2E = math.log2(math.e)
AUG_FLAG_ROW = HEAD_DIM
AUG_BIAS_ROW = HEAD_DIM + 8
Q_TILE = 256
FFN_ROWS = 512
VMEM_LIMIT = 56 * 1024 * 1024


def _alibi_slopes():
    n = N_ALIBI
    return 2.0 ** (-8.0 * (np.arange(n, dtype=np.float32) + 1.0) / n)


def _rmsnorm(x, g):
    ms = jnp.mean(x * x, axis=-1, keepdims=True)
    return (x * lax.rsqrt(ms + RMS_EPS)) * g


def _swiglu_residual(x, g, wg_ref, wu_ref, wd_ref):
    h = _rmsnorm(x, g).astype(BF16)
    gate = jnp.dot(h, wg_ref[...], preferred_element_type=F32)
    up = jnp.dot(h, wu_ref[...], preferred_element_type=F32)
    act = (gate * jax.nn.sigmoid(gate) * up).astype(BF16)
    return x + 0.5 * jnp.dot(act, wd_ref[...], preferred_element_type=F32)


def _resident(shape):
    zeros = (0,) * len(shape)
    return pl.BlockSpec(shape, lambda *_: zeros, pipeline_mode=pl.Buffered(1))


def _layer_resident(stacked, layer, rows=None, row_block=0):
    shape = stacked.shape[1:] if rows is None else (rows,) + stacked.shape[2:]
    index = (layer, row_block) + (0,) * (len(shape) - 1)
    return pl.BlockSpec((None,) + shape, lambda *_: index, pipeline_mode=pl.Buffered(1))


def _convert_plan(w, layer, n_steps, step_index):
    rows, cols = w.shape[1:]
    n_blocks = max(d for d in range(1, n_steps + 1) if rows % d == 0 and (rows // d) % 16 == 0)
    block = lambda *ids: jnp.minimum(step_index(*ids), n_blocks - 1)
    in_spec = pl.BlockSpec((None, rows // n_blocks, cols), lambda *ids: (layer, block(*ids), 0))
    out_spec = pl.BlockSpec((rows // n_blocks, cols), lambda *ids: (block(*ids), 0))
    return in_spec, out_spec, jax.ShapeDtypeStruct((rows, cols), BF16)


def _convert_blocks(in_refs, out_refs):
    for src, dst in zip(in_refs, out_refs):
        dst[...] = src[...].astype(BF16)


def _ffn_kernel(x_ref, g_ref, wg_ref, wu_ref, wd_ref, *refs):
    n_conv = len(refs) // 2
    o_ref = refs[n_conv]
    _convert_blocks(refs[:n_conv], refs[n_conv + 1:])
    half = x_ref.shape[0] // 2
    for r in range(2):
        rows = slice(r * half, (r + 1) * half)
        o_ref[rows, :] = _swiglu_residual(x_ref[rows, :], g_ref[...], wg_ref, wu_ref, wd_ref)


def _ffn_call(x2, g, wg, wu, wd, layer, later_weights, later_layer):
    t = x2.shape[0]
    steps = t // FFN_ROWS
    conv = [_convert_plan(w, later_layer, steps, lambda i: i) for w in later_weights]
    outs = pl.pallas_call(
        _ffn_kernel,
        grid=(steps,),
        in_specs=[
            pl.BlockSpec((FFN_ROWS, D_MODEL), lambda i: (i, 0)),
            _layer_resident(g, layer),
            _resident(wg.shape), _resident(wu.shape), _resident(wd.shape),
        ] + [c[0] for c in conv],
        out_specs=(pl.BlockSpec((FFN_ROWS, D_MODEL), lambda i: (i, 0)),) + tuple(c[1] for c in conv),
        out_shape=(jax.ShapeDtypeStruct(x2.shape, F32),) + tuple(c[2] for c in conv),
        compiler_params=pltpu.CompilerParams(
            dimension_semantics=("arbitrary",), vmem_limit_bytes=VMEM_LIMIT),
        name="ffn1",
    )(x2, g, wg, wu, wd, *later_weights)
    return outs[0], outs[1:]


def _split_hi_lo(x):
    hi = x.astype(BF16)
    lo = (x - hi.astype(F32)).astype(BF16)
    return hi, lo


_AQ0 = 0
_AK0 = _AQ0 + MOBA_HEADS * HEAD_DIM
_AV0 = _AK0 + MOBA_HEADS * HEAD_DIM
_BQ0 = _AV0 + MOBA_HEADS * HEAD_DIM
_BK0 = _BQ0 + DIFF_HEADS * HEAD_DIM
_BV0 = _BK0 + DIFF_HEADS * HEAD_DIM
_CQ0 = _BV0 + DIFF_HEADS * HEAD_DIM
_CK0 = _CQ0 + SWA_Q_HEADS * HEAD_DIM
_CV0 = _CK0 + SWA_KV_HEADS * HEAD_DIM
PROJ_ROWS = 2 * Q_TILE


def _proj_kernel(x_ref, g_ref, wt_ref, augk_ref, augq_ref,
                 qa_ref, qb_ref, qc_ref, ka_ref, kb_ref, kc_ref, va_ref, vb_ref, vc_ref,
                 kmean_ref):
    i = pl.program_id(1)
    t = x_ref.shape[1]
    blocks_per_step = t // MOBA_BLOCK

    @pl.when(i == 0)
    def _():
        kmean_ref[...] = jnp.zeros_like(kmean_ref)

    h = _rmsnorm(x_ref[0], g_ref[...]).astype(BF16)
    p = lax.dot_general(wt_ref[...], h, (((1,), (1,)), ((), ())),
                        preferred_element_type=F32)
    augk = augk_ref[...]
    lane = lax.broadcasted_iota(jnp.int32, (t, OPERAND_WIDTH), 1)

    def key_pair(r0):
        kt = p[r0:r0 + 2 * HEAD_DIM].T
        even = jnp.where(lane < HEAD_DIM, kt, augk).astype(BF16)
        odd = jnp.where(lane < HEAD_DIM, pltpu.roll(kt, HEAD_DIM, 1), augk).astype(BF16)
        return kt, (even, odd)

    def store_tiles(ref, hh, slab, rows, axis):
        for n in range(slab.shape[axis] // rows):
            piece = slab[n * rows:(n + 1) * rows] if axis == 0 else slab[:, n * rows:(n + 1) * rows]
            ref[0, hh, n] = piece

    for pair in range(MOBA_HEADS // 2):
        kt, slabs = key_pair(_AK0 + pair * 2 * HEAD_DIM)
        for blk in range(blocks_per_step):
            kmean_ref[pair, pl.ds(blocks_per_step * i + blk, 1), :] = jnp.mean(
                kt[blk * MOBA_BLOCK:(blk + 1) * MOBA_BLOCK], axis=0, keepdims=True)
        for par in range(2):
            store_tiles(ka_ref, 2 * pair + par, slabs[par], Q_TILE, 0)
    for pair in range(DIFF_HEADS // 2):
        _, slabs = key_pair(_BK0 + pair * 2 * HEAD_DIM)
        for par in range(2):
            store_tiles(kb_ref, 2 * pair + par, slabs[par], Q_TILE, 0)
    for pair in range(SWA_KV_HEADS // 2):
        _, slabs = key_pair(_CK0 + pair * 2 * HEAD_DIM)
        for par in range(2):
            store_tiles(kc_ref, 2 * pair + par, slabs[par], SWA_BLOCK, 0)

    ones_rows = jnp.where(
        lax.broadcasted_iota(jnp.int32, (VALUE_ROWS - HEAD_DIM, t), 0) == 0, 1.0, 0.0)

    def v_slab(r0):
        return jnp.concatenate([p[r0:r0 + HEAD_DIM], ones_rows], axis=0).astype(BF16)

    for hh in range(MOBA_HEADS):
        store_tiles(va_ref, hh, v_slab(_AV0 + hh * HEAD_DIM), Q_TILE, 1)
    for hh in range(DIFF_HEADS):
        store_tiles(vb_ref, hh, v_slab(_BV0 + hh * HEAD_DIM), Q_TILE, 1)
    for hh in range(SWA_KV_HEADS):
        store_tiles(vc_ref, hh, v_slab(_CV0 + hh * HEAD_DIM), SWA_BLOCK, 1)

    n_blocks = kmean_ref.shape[1]
    jrow = lax.broadcasted_iota(jnp.int32, (n_blocks, t), 0)
    qcol = lax.broadcasted_iota(jnp.int32, (n_blocks, t), 1)
    qblk = blocks_per_step * i + qcol // MOBA_BLOCK
    zeros_rows = jnp.zeros((HEAD_DIM, t), F32)
    moba_scale = HEAD_DIM ** -0.5 * LOG2E
    for hh in range(MOBA_HEADS):
        qf = p[_AQ0 + hh * HEAD_DIM:_AQ0 + (hh + 1) * HEAD_DIM]
        q_pad = jnp.concatenate([qf, zeros_rows] if hh % 2 == 0 else [zeros_rows, qf], axis=0)
        km_hi, km_lo = _split_hi_lo(kmean_ref[hh // 2])
        q_hi, q_lo = _split_hi_lo(q_pad)
        gate = (jnp.dot(km_hi, q_hi, preferred_element_type=F32)
                + jnp.dot(km_hi, q_lo, preferred_element_type=F32)
                + jnp.dot(km_lo, q_hi, preferred_element_type=F32))
        rank = jnp.zeros((n_blocks, t), F32)
        for m in range(n_blocks):
            gm = gate[m:m + 1, :]
            beats = (gm > gate) | ((gm == gate) & (m < jrow))
            rank = rank + jnp.where(beats & (m < qblk), 1.0, 0.0)
        chosen = ((rank < MOBA_TOPK) & (jrow < qblk)) | (jrow == qblk)
        flags = jnp.where(chosen, 0.0, NEG)
        aug = augq_ref[hh] + jnp.concatenate(
            [flags, jnp.zeros((HEAD_DIM - n_blocks, t), F32)], axis=0)
        qa_ref[0, hh, :HEAD_DIM, :] = (qf * moba_scale).astype(BF16)
        qa_ref[0, hh, HEAD_DIM:, :] = aug.astype(BF16)
    diff_scale = DIFF_QK_DIM ** -0.5 * LOG2E
    for hh in range(DIFF_HEADS):
        qf = p[_BQ0 + hh * HEAD_DIM:_BQ0 + (hh + 1) * HEAD_DIM]
        qb_ref[0, hh, :HEAD_DIM, :] = (qf * diff_scale).astype(BF16)
        qb_ref[0, hh, HEAD_DIM:, :] = augq_ref[MOBA_HEADS + hh].astype(BF16)
    swa_scale = HEAD_DIM ** -0.5 * LOG2E
    for hh in range(SWA_Q_HEADS):
        qf = p[_CQ0 + hh * HEAD_DIM:_CQ0 + (hh + 1) * HEAD_DIM]
        qc_ref[0, hh, :HEAD_DIM, :] = (qf * swa_scale).astype(BF16)
        qc_ref[0, hh, HEAD_DIM:, :] = augq_ref[MOBA_HEADS + DIFF_HEADS + hh].astype(BF16)


def _proj_call(x, g, wt, augk, augq, layer):
    b, s, _ = x.shape
    nt = s // Q_TILE
    nsb = s // SWA_BLOCK
    rows = PROJ_ROWS
    tiles = rows // Q_TILE
    sblocks = rows // SWA_BLOCK
    out_shape = (
        jax.ShapeDtypeStruct((b, MOBA_HEADS, OPERAND_WIDTH, s), BF16),
        jax.ShapeDtypeStruct((b, DIFF_HEADS, OPERAND_WIDTH, s), BF16),
        jax.ShapeDtypeStruct((b, SWA_Q_HEADS, OPERAND_WIDTH, s), BF16),
        jax.ShapeDtypeStruct((b, MOBA_HEADS, nt, Q_TILE, OPERAND_WIDTH), BF16),
        jax.ShapeDtypeStruct((b, DIFF_HEADS, nt, Q_TILE, OPERAND_WIDTH), BF16),
        jax.ShapeDtypeStruct((b, SWA_KV_HEADS, nsb, SWA_BLOCK, OPERAND_WIDTH), BF16),
        jax.ShapeDtypeStruct((b, MOBA_HEADS, nt, VALUE_ROWS, Q_TILE), BF16),
        jax.ShapeDtypeStruct((b, DIFF_HEADS, nt, VALUE_ROWS, Q_TILE), BF16),
        jax.ShapeDtypeStruct((b, SWA_KV_HEADS, nsb, VALUE_ROWS, SWA_BLOCK), BF16),
    )
    out_specs = (
        pl.BlockSpec((1, MOBA_HEADS, OPERAND_WIDTH, rows), lambda bi, i: (bi, 0, 0, i)),
        pl.BlockSpec((1, DIFF_HEADS, OPERAND_WIDTH, rows), lambda bi, i: (bi, 0, 0, i)),
        pl.BlockSpec((1, SWA_Q_HEADS, OPERAND_WIDTH, rows), lambda bi, i: (bi, 0, 0, i)),
        pl.BlockSpec((1, MOBA_HEADS, tiles, Q_TILE, OPERAND_WIDTH), lambda bi, i: (bi, 0, i, 0, 0)),
        pl.BlockSpec((1, DIFF_HEADS, tiles, Q_TILE, OPERAND_WIDTH), lambda bi, i: (bi, 0, i, 0, 0)),
        pl.BlockSpec((1, SWA_KV_HEADS, sblocks, SWA_BLOCK, OPERAND_WIDTH),
                     lambda bi, i: (bi, 0, i, 0, 0)),
        pl.BlockSpec((1, MOBA_HEADS, tiles, VALUE_ROWS, Q_TILE), lambda bi, i: (bi, 0, i, 0, 0)),
        pl.BlockSpec((1, DIFF_HEADS, tiles, VALUE_ROWS, Q_TILE), lambda bi, i: (bi, 0, i, 0, 0)),
        pl.BlockSpec((1, SWA_KV_HEADS, sblocks, VALUE_ROWS, SWA_BLOCK),
                     lambda bi, i: (bi, 0, i, 0, 0)),
    )
    return pl.pallas_call(
        _proj_kernel,
        grid=(b, s // rows),
        in_specs=[
            pl.BlockSpec((1, rows, D_MODEL), lambda bi, i: (bi, i, 0)),
            _layer_resident(g, layer),
            _layer_resident(wt, layer),
            pl.BlockSpec((rows, OPERAND_WIDTH), lambda bi, i: (i, 0)),
            pl.BlockSpec((N_Q_HEADS, HEAD_DIM, rows), lambda bi, i: (0, 0, i)),
        ],
        out_specs=out_specs,
        out_shape=out_shape,
        scratch_shapes=[pltpu.VMEM((MOBA_HEADS // 2, s // MOBA_BLOCK, OPERAND_WIDTH), F32)],
        compiler_params=pltpu.CompilerParams(
            dimension_semantics=("arbitrary", "arbitrary"), vmem_limit_bytes=VMEM_LIMIT),
        name="proj",
    )(x, g, wt, augk, augq)


def _flash_unrolled(n_chains, q_of, k_of, v_of, tiles):
    def scores_of(t):
        j, _ = tiles[t]
        return [jnp.dot(k_of(c, j), q_of(c), preferred_element_type=F32)
                for c in range(n_chains)]

    m = [None] * n_chains
    acc = [None] * n_chains
    upcoming = scores_of(0)
    for t, (j, mask) in enumerate(tiles):
        scores = upcoming
        if t + 1 < len(tiles):
            upcoming = scores_of(t + 1)
        probs, alphas = [], []
        for c in range(n_chains):
            s = scores[c] if mask is None else jnp.where(mask, scores[c], NEG)
            m_new = jnp.max(s, axis=0, keepdims=True)
            if t > 0:
                m_new = jnp.maximum(m[c], m_new)
                alphas.append(jnp.exp2(m[c] - m_new))
            m[c] = m_new
            probs.append(jnp.exp2(s - m_new).astype(BF16))
        for c in range(n_chains):
            pv = jnp.dot(v_of(c, j), probs[c], preferred_element_type=F32)
            acc[c] = pv if t == 0 else alphas[c] * acc[c] + pv
    return acc


def _normalized(acc):
    return acc[:HEAD_DIM] / acc[HEAD_DIM:HEAD_DIM + 1]


def _attn_params():
    return pltpu.CompilerParams(
        dimension_semantics=("arbitrary", "arbitrary"), vmem_limit_bytes=VMEM_LIMIT)


ATTN_TILE = 2 * Q_TILE


def _tile_pair_getters(k_ref, v_ref):
    def k_of(c, j):
        return jnp.concatenate([k_ref[0, c, 2 * j], k_ref[0, c, 2 * j + 1]], axis=0)

    def v_of(c, j):
        return jnp.concatenate([v_ref[0, c, 2 * j], v_ref[0, c, 2 * j + 1]], axis=1)

    return k_of, v_of


def _causal_then_past(i_static, causal):
    return [(i_static, causal)] + [(j, None) for j in range(i_static)]


def _moba_kernel(q_ref, k_ref, v_ref, o_ref):
    nh = q_ref.shape[1]
    k_of, v_of = _tile_pair_getters(k_ref, v_ref)
    for i_static in range(k_ref.shape[2] * Q_TILE // ATTN_TILE):
        @pl.when(pl.program_id(1) == i_static)
        def _(i_static=i_static):
            kr = lax.broadcasted_iota(jnp.int32, (ATTN_TILE, ATTN_TILE), 0)
            qc = lax.broadcasted_iota(jnp.int32, (ATTN_TILE, ATTN_TILE), 1)
            acc = _flash_unrolled(nh, lambda c: q_ref[0, c], k_of, v_of,
                                  _causal_then_past(i_static, kr <= qc))
            for h in range(nh):
                o_ref[0, h * HEAD_DIM:(h + 1) * HEAD_DIM, :] = (
                    _normalized(acc[h]).astype(o_ref.dtype))


def _moba_call(qa, ka, va):
    b, nh, _, s = qa.shape
    nt = s // Q_TILE
    return pl.pallas_call(
        _moba_kernel,
        grid=(b, s // ATTN_TILE),
        in_specs=[
            pl.BlockSpec((1, nh, OPERAND_WIDTH, ATTN_TILE), lambda bi, i: (bi, 0, 0, i)),
            pl.BlockSpec((1, nh, nt, Q_TILE, OPERAND_WIDTH), lambda bi, i: (bi, 0, 0, 0, 0)),
            pl.BlockSpec((1, nh, nt, VALUE_ROWS, Q_TILE), lambda bi, i: (bi, 0, 0, 0, 0)),
        ],
        out_specs=pl.BlockSpec((1, nh * HEAD_DIM, ATTN_TILE), lambda bi, i: (bi, 0, i)),
        out_shape=jax.ShapeDtypeStruct((b, nh * HEAD_DIM, s), BF16),
        compiler_params=_attn_params(),
        name="moba",
    )(qa, ka, va)


def _diff_kernel(q_ref, k_ref, v_ref, lq1_ref, lk1_ref, lq2_ref, lk2_ref, linit_ref, g_ref, o_ref,
                 qm_ref):
    nh = q_ref.shape[1]
    row = lax.broadcasted_iota(jnp.int32, q_ref.shape[2:], 0)
    for h in range(nh):
        q = q_ref[0, h]
        zero = jnp.zeros_like(q)
        qm_ref[h, :, :ATTN_TILE] = jnp.where(
            (row < DIFF_QK_DIM) | (row >= 2 * DIFF_QK_DIM), q, zero)
        qm_ref[h, :, ATTN_TILE:] = jnp.where(row >= DIFF_QK_DIM, q, zero)
    k_of, v_of = _tile_pair_getters(k_ref, v_ref)
    lam_init = linit_ref[...]
    lam = (jnp.exp(jnp.sum(lq1_ref[...] * lk1_ref[...], axis=-1, keepdims=True))
           - jnp.exp(jnp.sum(lq2_ref[...] * lk2_ref[...], axis=-1, keepdims=True))
           + lam_init)
    for i_static in range(k_ref.shape[2] * Q_TILE // ATTN_TILE):
        @pl.when(pl.program_id(1) == i_static)
        def _(i_static=i_static):
            kr = lax.broadcasted_iota(jnp.int32, (ATTN_TILE, 2 * ATTN_TILE), 0)
            qc = lax.broadcasted_iota(jnp.int32, (ATTN_TILE, 2 * ATTN_TILE), 1)
            causal = kr <= jnp.where(qc >= ATTN_TILE, qc - ATTN_TILE, qc)
            acc = _flash_unrolled(nh, lambda c: qm_ref[c], k_of, v_of,
                                  _causal_then_past(i_static, causal))
            for h in range(nh):
                w = _normalized(acc[h])
                o = w[:, :ATTN_TILE] - lam * w[:, ATTN_TILE:]
                ms = jnp.mean(o * o, axis=0, keepdims=True)
                y = (o * lax.rsqrt(ms + RMS_EPS)) * g_ref[...]
                o_ref[0, h * HEAD_DIM:(h + 1) * HEAD_DIM, :] = (
                    (y * (1.0 - lam_init)).astype(o_ref.dtype))


def _diff_call(qb, kb, vb, lam_q1, lam_k1, lam_q2, lam_k2, lam_init, g_col, layer):
    b, nh, _, s = qb.shape
    nt = s // Q_TILE
    per_layer = lambda arr: pl.BlockSpec(
        (None,) + arr.shape[1:], lambda bi, i: (layer,) + (0,) * (arr.ndim - 1))
    return pl.pallas_call(
        _diff_kernel,
        grid=(b, s // ATTN_TILE),
        in_specs=[
            pl.BlockSpec((1, nh, OPERAND_WIDTH, ATTN_TILE), lambda bi, i: (bi, 0, 0, i)),
            pl.BlockSpec((1, nh, nt, Q_TILE, OPERAND_WIDTH), lambda bi, i: (bi, 0, 0, 0, 0)),
            pl.BlockSpec((1, nh, nt, VALUE_ROWS, Q_TILE), lambda bi, i: (bi, 0, 0, 0, 0)),
            per_layer(lam_q1), per_layer(lam_k1), per_layer(lam_q2), per_layer(lam_k2),
            per_layer(lam_init), per_layer(g_col),
        ],
        out_specs=pl.BlockSpec((1, nh * HEAD_DIM, ATTN_TILE), lambda bi, i: (bi, 0, i)),
        out_shape=jax.ShapeDtypeStruct((b, nh * HEAD_DIM, s), BF16),
        scratch_shapes=[pltpu.VMEM((nh, OPERAND_WIDTH, 2 * ATTN_TILE), BF16)],
        compiler_params=_attn_params(),
        name="diffattn",
    )(qb, kb, vb, lam_q1, lam_k1, lam_q2, lam_k2, lam_init, g_col)


SWA_BLOCKS_PER_STEP = 4


def _swa_kernel(q_ref, k_ref, v_ref, sink_ref, o_ref):
    i = pl.program_id(1)
    kr = lax.broadcasted_iota(jnp.int32, (2 * SWA_BLOCK, SWA_BLOCK), 0)
    qc = lax.broadcasted_iota(jnp.int32, (2 * SWA_BLOCK, SWA_BLOCK), 1)
    dist = qc + SWA_BLOCK - kr
    in_win = (dist >= 0) & (dist < SWA_WINDOW)
    chains = [(b2, kvh) for b2 in range(SWA_BLOCKS_PER_STEP) for kvh in range(SWA_KV_HEADS)]
    blocks, bands = [], []
    for b2 in range(SWA_BLOCKS_PER_STEP):
        n = SWA_BLOCKS_PER_STEP * i + b2
        blocks.append((jnp.maximum(n - 1, 0), n))
        first_row = jnp.where(n > 0, 0, SWA_BLOCK)
        band = jnp.where(in_win & (kr >= first_row), 0.0, NEG)
        bands.append(jnp.concatenate([band] * SWA_GROUP, axis=1))
    scores = []
    for b2, kvh in chains:
        prev, n = blocks[b2]
        cols = slice(b2 * SWA_BLOCK, (b2 + 1) * SWA_BLOCK)
        q = jnp.concatenate([q_ref[0, kvh * SWA_GROUP + g, :, cols]
                             for g in range(SWA_GROUP)], axis=1)
        k = jnp.concatenate([k_ref[0, kvh, prev], k_ref[0, kvh, n]], axis=0)
        scores.append(jnp.dot(k, q, preferred_element_type=F32))
    probs, sink_terms = [], []
    for (b2, kvh), s in zip(chains, scores):
        s = s + bands[b2]
        sink = sink_ref[kvh] * LOG2E
        m = jnp.maximum(jnp.max(s, axis=0, keepdims=True), sink)
        probs.append(jnp.exp2(s - m).astype(BF16))
        sink_terms.append(jnp.exp2(sink - m))
    for (b2, kvh), p, sink_term in zip(chains, probs, sink_terms):
        prev, n = blocks[b2]
        cols = slice(b2 * SWA_BLOCK, (b2 + 1) * SWA_BLOCK)
        v = jnp.concatenate([v_ref[0, kvh, prev], v_ref[0, kvh, n]], axis=1)
        acc = jnp.dot(v, p, preferred_element_type=F32)
        o = acc[:HEAD_DIM] / (acc[HEAD_DIM:HEAD_DIM + 1] + sink_term)
        for g in range(SWA_GROUP):
            r0 = (kvh * SWA_GROUP + g) * HEAD_DIM
            o_ref[0, r0:r0 + HEAD_DIM, cols] = (
                o[:, g * SWA_BLOCK:(g + 1) * SWA_BLOCK].astype(o_ref.dtype))


def _swa_call(qc, kc, vc, sink_rows, layer):
    b, nq, _, s = qc.shape
    nkv = kc.shape[1]
    nsb = s // SWA_BLOCK
    step = SWA_BLOCKS_PER_STEP * SWA_BLOCK
    return pl.pallas_call(
        _swa_kernel,
        grid=(b, s // step),
        in_specs=[
            pl.BlockSpec((1, nq, OPERAND_WIDTH, step), lambda bi, i: (bi, 0, 0, i)),
            pl.BlockSpec((1, nkv, nsb, SWA_BLOCK, OPERAND_WIDTH), lambda bi, i: (bi, 0, 0, 0, 0)),
            pl.BlockSpec((1, nkv, nsb, VALUE_ROWS, SWA_BLOCK), lambda bi, i: (bi, 0, 0, 0, 0)),
            pl.BlockSpec((None,) + sink_rows.shape[1:], lambda bi, i: (layer, 0, 0, 0)),
        ],
        out_specs=pl.BlockSpec((1, nq * HEAD_DIM, step), lambda bi, i: (bi, 0, i)),
        out_shape=jax.ShapeDtypeStruct((b, nq * HEAD_DIM, s), BF16),
        compiler_params=_attn_params(),
        name="swa",
    )(qc, kc, vc, sink_rows)


def _mix_ffn_kernel(x_ref, ma_ref, mb_ref, mc_ref, woa_ref, wob_ref, woc_ref,
                    g_ref, wg_ref, wu_ref, wd_ref, gf_ref, *refs, final):
    n_conv = len(refs) // 2
    o_ref = refs[n_conv]
    _convert_blocks(refs[:n_conv], refs[n_conv + 1:])
    tdims = (((0,), (0,)), ((), ()))
    x = x_ref[0]
    x = x + lax.dot_general(ma_ref[0], woa_ref[...], tdims, preferred_element_type=F32)
    x = x + lax.dot_general(mb_ref[0], wob_ref[...], tdims, preferred_element_type=F32)
    x = x + lax.dot_general(mc_ref[0], woc_ref[...], tdims, preferred_element_type=F32)
    x = _swiglu_residual(x, g_ref[...], wg_ref, wu_ref, wd_ref)
    if final:
        x = _rmsnorm(x, gf_ref[...])
    o_ref[0] = x


def _mix_ffn_call(x, ma, mb, mc, wo, g, wg, wu, wd, gf, layer, final, later_weights, later_layer):
    b, s, _ = x.shape
    rows = FFN_ROWS
    steps = s // rows
    a_rows, b_rows, c_rows = ma.shape[1], mb.shape[1], mc.shape[1]
    assert a_rows == b_rows and c_rows == a_rows + b_rows
    conv = [_convert_plan(w, later_layer, b * steps, lambda bi, i: bi * steps + i)
            for w in later_weights]
    outs = pl.pallas_call(
        functools.partial(_mix_ffn_kernel, final=final),
        grid=(b, steps),
        in_specs=[
            pl.BlockSpec((1, rows, D_MODEL), lambda bi, i: (bi, i, 0)),
            pl.BlockSpec((1, a_rows, rows), lambda bi, i: (bi, 0, i)),
            pl.BlockSpec((1, b_rows, rows), lambda bi, i: (bi, 0, i)),
            pl.BlockSpec((1, c_rows, rows), lambda bi, i: (bi, 0, i)),
            _layer_resident(wo, layer, a_rows, 0),
            _layer_resident(wo, layer, b_rows, 1),
            _layer_resident(wo, layer, c_rows, 1),
            _layer_resident(g, layer),
            _resident(wg.shape), _resident(wu.shape), _resident(wd.shape),
            _resident((1, D_MODEL)),
        ] + [c[0] for c in conv],
        out_specs=(pl.BlockSpec((1, rows, D_MODEL), lambda bi, i: (bi, i, 0)),)
        + tuple(c[1] for c in conv),
        out_shape=(jax.ShapeDtypeStruct(x.shape, F32),) + tuple(c[2] for c in conv),
        compiler_params=pltpu.CompilerParams(
            dimension_semantics=("arbitrary", "arbitrary"), vmem_limit_bytes=VMEM_LIMIT),
        name="mix_ffn2_final" if final else "mix_ffn2",
    )(x, ma, mb, mc, wo, wo, wo, g, wg, wu, wd, gf, *later_weights)
    return outs[0], outs[1:]


def _split3(x):
    x1 = x.astype(BF16).astype(F32)
    r = x - x1
    x2 = r.astype(BF16).astype(F32)
    x3 = (r - x2).astype(BF16).astype(F32)
    return [x1, x2, x3]


def _bias_rows_kernel(raw_ref, o_ref):
    s = raw_ref.shape[2]
    row = lax.broadcasted_iota(jnp.int32, (16, s), 0)
    lead = AUG_BIAS_ROW - HEAD_DIM
    for hh in range(raw_ref.shape[0]):
        raw = raw_ref[hh]
        pieces = _split3(raw[0:1]) + 2 * _split3(raw[1:2])
        blk = jnp.zeros((16, s), F32)
        for r, piece in enumerate(pieces):
            blk = jnp.where(row == r, piece, blk)
        o_ref[hh] = jnp.concatenate(
            [jnp.zeros((lead, s), F32), blk, jnp.zeros((HEAD_DIM - lead - 16, s), F32)], axis=0)


def _operand_tables(s):
    slopes = _alibi_slopes()
    head_slopes = np.concatenate([
        slopes[SWA_Q_HEADS + DIFF_HEADS:],
        slopes[SWA_Q_HEADS:SWA_Q_HEADS + DIFF_HEADS],
        slopes[:SWA_Q_HEADS],
    ]).astype(np.float32)
    pos = jnp.arange(s, dtype=F32)
    sl = jnp.asarray(head_slopes * np.float32(LOG2E))[:, None]
    raw = jnp.zeros((N_Q_HEADS, 8, s), F32)
    raw = raw.at[:, 0, :].set(-sl * pos[None, :])
    raw = raw.at[:, 1, :].set(jnp.broadcast_to(sl, (N_Q_HEADS, s)))
    augq = pl.pallas_call(
        _bias_rows_kernel,
        out_shape=jax.ShapeDtypeStruct((N_Q_HEADS, HEAD_DIM, s), F32),
        name="bias_rows",
    )(raw)
    blk = jnp.arange(s, dtype=jnp.int32) // MOBA_BLOCK
    onehot = (blk[:, None] == jnp.arange(8, dtype=jnp.int32)[None, :]).astype(F32)
    ones = jnp.ones((s, 3), F32)
    base = jnp.broadcast_to((blk * MOBA_BLOCK).astype(F32)[:, None], (s, 3))
    off = jnp.broadcast_to((jnp.arange(s, dtype=jnp.int32) % MOBA_BLOCK).astype(F32)[:, None], (s, 3))
    augk = jnp.concatenate([
        jnp.zeros((s, HEAD_DIM), F32), onehot, ones, base, off,
        jnp.zeros((s, OPERAND_WIDTH - HEAD_DIM - 17), F32)], axis=1)
    return augq, augk


def kernel(x, norm_ffn1, w1_gate, w1_up, w1_down, norm_mix, w_in, lam_q1, lam_k1, lam_q2, lam_k2,
           diff_subln, sinks, w_out, norm_ffn2, w2_gate, w2_up, w2_down, final_norm):
    b, s, d = x.shape
    depth = w_in.shape[0]
    assert d == D_MODEL and s % ATTN_TILE == 0 and (b * s) % FFN_ROWS == 0
    augq, augk = _operand_tables(s)
    as_rows = lambda p: p.reshape(depth, 1, -1)
    g1, gm, g2 = as_rows(norm_ffn1), as_rows(norm_mix), as_rows(norm_ffn2)
    ffn1_f32, ffn2_f32 = (w1_gate, w1_up, w1_down), (w2_gate, w2_up, w2_down)
    wt, wo = jnp.swapaxes(w_in.astype(BF16), 1, 2), w_out.astype(BF16)
    lq1, lk1, lq2, lk2 = as_rows(lam_q1), as_rows(lam_k1), as_rows(lam_q2), as_rows(lam_k2)
    lam_init = jnp.asarray([0.8 - 0.6 * math.exp(-0.3 * l) for l in range(depth)],
                           F32).reshape(depth, 1, 1)
    subln = diff_subln.reshape(depth, -1, 1)
    sink_rows = jnp.repeat(sinks.reshape(depth, SWA_KV_HEADS, 1, SWA_GROUP), SWA_BLOCK, axis=3)
    gf = final_norm.reshape(1, d)
    ffn1_w = tuple(w[0].astype(BF16) for w in ffn1_f32)
    for l in range(depth):
        x2, ffn2_w = _ffn_call(x.reshape(b * s, d), g1, *ffn1_w, l, ffn2_f32, l)
        x = x2.reshape(b, s, d)
        qa, qb, qc, ka, kb, kc, va, vb, vc = _proj_call(x, gm, wt, augk, augq, l)
        ma = _moba_call(qa, ka, va)
        mb = _diff_call(qb, kb, vb, lq1, lk1, lq2, lk2, lam_init, subln, l)
        mc = _swa_call(qc, kc, vc, sink_rows, l)
        last = l == depth - 1
        x, ffn1_w = _mix_ffn_call(x, ma, mb, mc, wo, g2, *ffn2_w, gf, l, last,
                                  () if last else ffn1_f32, l + 1)
    return x
```

```python
import functools
import math

import numpy as np
import jax
import jax.numpy as jnp
from jax import lax
from jax.experimental import pallas as pl
from jax.experimental.pallas import tpu as pltpu

F32 = jnp.float32
BF16 = jnp.bfloat16

D_MODEL = 1024
D_FF = 2816
HEAD_DIM = 64
MOBA_HEADS = 4
MOBA_BLOCK = 256
MOBA_TOPK = 3
DIFF_HEADS = 4
DIFF_QK_DIM = 32
SWA_Q_HEADS = 8
SWA_KV_HEADS = 2
SWA_GROUP = SWA_Q_HEADS // SWA_KV_HEADS
SWA_WINDOW = 128
SWA_BLOCK = 128
N_ALIBI = MOBA_HEADS + DIFF_HEADS + SWA_Q_HEADS
RMS_EPS = 1e-6
NEG = -1e30

N_Q_HEADS = MOBA_HEADS + DIFF_HEADS + SWA_Q_HEADS
OPERAND_WIDTH = 128
VALUE_ROWS = HEAD_DIM + 16
LOG2E = math.log2(math.e)
AUG_FLAG_ROW = HEAD_DIM
AUG_BIAS_ROW = HEAD_DIM + 8
Q_TILE = 256
FFN_ROWS = 512
FFN1_ROWS = 1024
FFN_CHUNK = 256
VMEM_LIMIT = 56 * 1024 * 1024


def _alibi_slopes():
    n = N_ALIBI
    return 2.0 ** (-8.0 * (np.arange(n, dtype=np.float32) + 1.0) / n)


def _rmsnorm(x, g):
    ms = jnp.mean(x * x, axis=-1, keepdims=True)
    return (x * lax.rsqrt(ms + RMS_EPS)) * g


def _swiglu_residual(x, g, wg_ref, wu_ref, wd_ref):
    h = _rmsnorm(x, g).astype(BF16)
    gate = jnp.dot(h, wg_ref[...], preferred_element_type=F32)
    up = jnp.dot(h, wu_ref[...], preferred_element_type=F32)
    act = (gate * jax.nn.sigmoid(gate) * up).astype(BF16)
    return x + 0.5 * jnp.dot(act, wd_ref[...], preferred_element_type=F32)


def _resident(shape):
    zeros = (0,) * len(shape)
    return pl.BlockSpec(shape, lambda *_: zeros, pipeline_mode=pl.Buffered(1))


def _layer_resident(stacked, layer, rows=None, row_block=0):
    shape = stacked.shape[1:] if rows is None else (rows,) + stacked.shape[2:]
    index = (layer, row_block) + (0,) * (len(shape) - 1)
    return pl.BlockSpec((None,) + shape, lambda *_: index, pipeline_mode=pl.Buffered(1))


def _convert_plan(w, layer, n_steps, step_index):
    rows, cols = w.shape[1:]
    n_blocks = max(d for d in range(1, n_steps + 1) if rows % d == 0 and (rows // d) % 16 == 0)
    block = lambda *ids: jnp.minimum(step_index(*ids), n_blocks - 1)
    in_spec = pl.BlockSpec((None, rows // n_blocks, cols), lambda *ids: (layer, block(*ids), 0))
    out_spec = pl.BlockSpec((rows // n_blocks, cols), lambda *ids: (block(*ids), 0))
    return in_spec, out_spec, jax.ShapeDtypeStruct((rows, cols), BF16)


def _convert_blocks(in_refs, out_refs):
    for src, dst in zip(in_refs, out_refs):
        dst[...] = src[...].astype(BF16)


def _ffn_kernel(x_ref, g_ref, wg_ref, wu_ref, wd_ref, *refs):
    n_conv = len(refs) // 2
    o_ref = refs[n_conv]
    _convert_blocks(refs[:n_conv], refs[n_conv + 1:])
    for r in range(x_ref.shape[0] // FFN_CHUNK):
        rows = slice(r * FFN_CHUNK, (r + 1) * FFN_CHUNK)
        o_ref[rows, :] = _swiglu_residual(x_ref[rows, :], g_ref[...], wg_ref, wu_ref, wd_ref)


def _ffn_call(x2, g, wg, wu, wd, layer, later_weights, later_layer):
    t = x2.shape[0]
    steps = t // FFN1_ROWS
    conv = [_convert_plan(w, later_layer, steps, lambda i: i) for w in later_weights]
    outs = pl.pallas_call(
        _ffn_kernel,
        grid=(steps,),
        in_specs=[
            pl.BlockSpec((FFN1_ROWS, D_MODEL), lambda i: (i, 0)),
            _layer_resident(g, layer),
            _resident(wg.shape), _resident(wu.shape), _resident(wd.shape),
        ] + [c[0] for c in conv],
        out_specs=(pl.BlockSpec((FFN1_ROWS, D_MODEL), lambda i: (i, 0)),) + tuple(c[1] for c in conv),
        out_shape=(jax.ShapeDtypeStruct(x2.shape, F32),) + tuple(c[2] for c in conv),
        compiler_params=pltpu.CompilerParams(
            dimension_semantics=("arbitrary",), vmem_limit_bytes=VMEM_LIMIT),
        name="ffn1",
    )(x2, g, wg, wu, wd, *later_weights)
    return outs[0], outs[1:]


def _split_hi_lo(x):
    hi = x.astype(BF16)
    lo = (x - hi.astype(F32)).astype(BF16)
    return hi, lo


_AQ0 = 0
_AK0 = _AQ0 + MOBA_HEADS * HEAD_DIM
_AV0 = _AK0 + MOBA_HEADS * HEAD_DIM
_BQ0 = _AV0 + MOBA_HEADS * HEAD_DIM
_BK0 = _BQ0 + DIFF_HEADS * HEAD_DIM
_BV0 = _BK0 + DIFF_HEADS * HEAD_DIM
_CQ0 = _BV0 + DIFF_HEADS * HEAD_DIM
_CK0 = _CQ0 + SWA_Q_HEADS * HEAD_DIM
_CV0 = _CK0 + SWA_KV_HEADS * HEAD_DIM
PROJ_ROWS = 2 * Q_TILE


def _proj_kernel(x_ref, g_ref, wt_ref, augk_ref, augq_ref,
                 qa_ref, qb_ref, qc_ref, ka_ref, kb_ref, kc_ref, va_ref, vb_ref, vc_ref,
                 kmean_ref):
    i = pl.program_id(1)
    t = x_ref.shape[1]
    blocks_per_step = t // MOBA_BLOCK

    @pl.when(i == 0)
    def _():
        kmean_ref[...] = jnp.zeros_like(kmean_ref)

    h = _rmsnorm(x_ref[0], g_ref[...]).astype(BF16)
    p = lax.dot_general(wt_ref[...], h, (((1,), (1,)), ((), ())),
                        preferred_element_type=F32)
    augk = augk_ref[...]
    lane = lax.broadcasted_iota(jnp.int32, (t, OPERAND_WIDTH), 1)

    def key_pair(r0):
        kt = p[r0:r0 + 2 * HEAD_DIM].T
        even = jnp.where(lane < HEAD_DIM, kt, augk).astype(BF16)
        odd = jnp.where(lane < HEAD_DIM, pltpu.roll(kt, HEAD_DIM, 1), augk).astype(BF16)
        return kt, (even, odd)

    def store_tiles(ref, hh, slab, rows, axis):
        for n in range(slab.shape[axis] // rows):
            piece = slab[n * rows:(n + 1) * rows] if axis == 0 else slab[:, n * rows:(n + 1) * rows]
            ref[0, hh, n] = piece

    for pair in range(MOBA_HEADS // 2):
        kt, slabs = key_pair(_AK0 + pair * 2 * HEAD_DIM)
        for blk in range(blocks_per_step):
            kmean_ref[pair, pl.ds(blocks_per_step * i + blk, 1), :] = jnp.mean(
                kt[blk * MOBA_BLOCK:(blk + 1) * MOBA_BLOCK], axis=0, keepdims=True)
        for par in range(2):
            store_tiles(ka_ref, 2 * pair + par, slabs[par], Q_TILE, 0)
    for pair in range(DIFF_HEADS // 2):
        _, slabs = key_pair(_BK0 + pair * 2 * HEAD_DIM)
        for par in range(2):
            store_tiles(kb_ref, 2 * pair + par, slabs[par], Q_TILE, 0)
    for pair in range(SWA_KV_HEADS // 2):
        _, slabs = key_pair(_CK0 + pair * 2 * HEAD_DIM)
        for par in range(2):
            store_tiles(kc_ref, 2 * pair + par, slabs[par], SWA_BLOCK, 0)

    ones_rows = jnp.where(
        lax.broadcasted_iota(jnp.int32, (VALUE_ROWS - HEAD_DIM, t), 0) == 0, 1.0, 0.0)

    def v_slab(r0):
        return jnp.concatenate([p[r0:r0 + HEAD_DIM], ones_rows], axis=0).astype(BF16)

    for hh in range(MOBA_HEADS):
        store_tiles(va_ref, hh, v_slab(_AV0 + hh * HEAD_DIM), Q_TILE, 1)
    for hh in range(DIFF_HEADS):
        store_tiles(vb_ref, hh, v_slab(_BV0 + hh * HEAD_DIM), Q_TILE, 1)
    for hh in range(SWA_KV_HEADS):
        store_tiles(vc_ref, hh, v_slab(_CV0 + hh * HEAD_DIM), SWA_BLOCK, 1)

    n_blocks = kmean_ref.shape[1]
    jrow = lax.broadcasted_iota(jnp.int32, (n_blocks, t), 0)
    qcol = lax.broadcasted_iota(jnp.int32, (n_blocks, t), 1)
    qblk = blocks_per_step * i + qcol // MOBA_BLOCK
    zeros_rows = jnp.zeros((HEAD_DIM, t), F32)
    moba_scale = HEAD_DIM ** -0.5 * LOG2E
    for hh in range(MOBA_HEADS):
        qf = p[_AQ0 + hh * HEAD_DIM:_AQ0 + (hh + 1) * HEAD_DIM]
        q_pad = jnp.concatenate([qf, zeros_rows] if hh % 2 == 0 else [zeros_rows, qf], axis=0)
        km_hi, km_lo = _split_hi_lo(kmean_ref[hh // 2])
        q_hi, q_lo = _split_hi_lo(q_pad)
        gate = (jnp.dot(km_hi, q_hi, preferred_element_type=F32)
                + jnp.dot(km_hi, q_lo, preferred_element_type=F32)
                + jnp.dot(km_lo, q_hi, preferred_element_type=F32))
        rank = jnp.zeros((n_blocks, t), F32)
        for m in range(n_blocks):
            gm = gate[m:m + 1, :]
            beats = (gm > gate) | ((gm == gate) & (m < jrow))
            rank = rank + jnp.where(beats & (m < qblk), 1.0, 0.0)
        chosen = ((rank < MOBA_TOPK) & (jrow < qblk)) | (jrow == qblk)
        flags = jnp.where(chosen, 0.0, NEG)
        aug = augq_ref[hh] + jnp.concatenate(
            [flags, jnp.zeros((HEAD_DIM - n_blocks, t), F32)], axis=0)
        qa_ref[0, hh, :HEAD_DIM, :] = (qf * moba_scale).astype(BF16)
        qa_ref[0, hh, HEAD_DIM:, :] = aug.astype(BF16)
    diff_scale = DIFF_QK_DIM ** -0.5 * LOG2E
    for hh in range(DIFF_HEADS):
        qf = p[_BQ0 + hh * HEAD_DIM:_BQ0 + (hh + 1) * HEAD_DIM]
        qb_ref[0, hh, :HEAD_DIM, :] = (qf * diff_scale).astype(BF16)
        qb_ref[0, hh, HEAD_DIM:, :] = augq_ref[MOBA_HEADS + hh].astype(BF16)
    swa_scale = HEAD_DIM ** -0.5 * LOG2E
    for hh in range(SWA_Q_HEADS):
        qf = p[_CQ0 + hh * HEAD_DIM:_CQ0 + (hh + 1) * HEAD_DIM]
        qc_ref[0, hh, :HEAD_DIM, :] = (qf * swa_scale).astype(BF16)
        qc_ref[0, hh, HEAD_DIM:, :] = augq_ref[MOBA_HEADS + DIFF_HEADS + hh].astype(BF16)


def _proj_call(x, g, wt, augk, augq, layer):
    b, s, _ = x.shape
    nt = s // Q_TILE
    nsb = s // SWA_BLOCK
    rows = PROJ_ROWS
    tiles = rows // Q_TILE
    sblocks = rows // SWA_BLOCK
    out_shape = (
        jax.ShapeDtypeStruct((b, MOBA_HEADS, OPERAND_WIDTH, s), BF16),
        jax.ShapeDtypeStruct((b, DIFF_HEADS, OPERAND_WIDTH, s), BF16),
        jax.ShapeDtypeStruct((b, SWA_Q_HEADS, OPERAND_WIDTH, s), BF16),
        jax.ShapeDtypeStruct((b, MOBA_HEADS, nt, Q_TILE, OPERAND_WIDTH), BF16),
        jax.ShapeDtypeStruct((b, DIFF_HEADS, nt, Q_TILE, OPERAND_WIDTH), BF16),
        jax.ShapeDtypeStruct((b, SWA_KV_HEADS, nsb, SWA_BLOCK, OPERAND_WIDTH), BF16),
        jax.ShapeDtypeStruct((b, MOBA_HEADS, nt, VALUE_ROWS, Q_TILE), BF16),
        jax.ShapeDtypeStruct((b, DIFF_HEADS, nt, VALUE_ROWS, Q_TILE), BF16),
        jax.ShapeDtypeStruct((b, SWA_KV_HEADS, nsb, VALUE_ROWS, SWA_BLOCK), BF16),
    )
    out_specs = (
        pl.BlockSpec((1, MOBA_HEADS, OPERAND_WIDTH, rows), lambda bi, i: (bi, 0, 0, i)),
        pl.BlockSpec((1, DIFF_HEADS, OPERAND_WIDTH, rows), lambda bi, i: (bi, 0, 0, i)),
        pl.BlockSpec((1, SWA_Q_HEADS, OPERAND_WIDTH, rows), lambda bi, i: (bi, 0, 0, i)),
        pl.BlockSpec((1, MOBA_HEADS, tiles, Q_TILE, OPERAND_WIDTH), lambda bi, i: (bi, 0, i, 0, 0)),
        pl.BlockSpec((1, DIFF_HEADS, tiles, Q_TILE, OPERAND_WIDTH), lambda bi, i: (bi, 0, i, 0, 0)),
        pl.BlockSpec((1, SWA_KV_HEADS, sblocks, SWA_BLOCK, OPERAND_WIDTH),
                     lambda bi, i: (bi, 0, i, 0, 0)),
        pl.BlockSpec((1, MOBA_HEADS, tiles, VALUE_ROWS, Q_TILE), lambda bi, i: (bi, 0, i, 0, 0)),
        pl.BlockSpec((1, DIFF_HEADS, tiles, VALUE_ROWS, Q_TILE), lambda bi, i: (bi, 0, i, 0, 0)),
        pl.BlockSpec((1, SWA_KV_HEADS, sblocks, VALUE_ROWS, SWA_BLOCK),
                     lambda bi, i: (bi, 0, i, 0, 0)),
    )
    return pl.pallas_call(
        _proj_kernel,
        grid=(b, s // rows),
        in_specs=[
            pl.BlockSpec((1, rows, D_MODEL), lambda bi, i: (bi, i, 0)),
            _layer_resident(g, layer),
            _layer_resident(wt, layer),
            pl.BlockSpec((rows, OPERAND_WIDTH), lambda bi, i: (i, 0)),
            pl.BlockSpec((N_Q_HEADS, HEAD_DIM, rows), lambda bi, i: (0, 0, i)),
        ],
        out_specs=out_specs,
        out_shape=out_shape,
        scratch_shapes=[pltpu.VMEM((MOBA_HEADS // 2, s // MOBA_BLOCK, OPERAND_WIDTH), F32)],
        compiler_params=pltpu.CompilerParams(
            dimension_semantics=("arbitrary", "arbitrary"), vmem_limit_bytes=VMEM_LIMIT),
        name="proj",
    )(x, g, wt, augk, augq)


def _flash_unrolled(n_chains, q_of, k_of, v_of, tiles):
    def scores_of(t):
        j, _ = tiles[t]
        return [jnp.dot(k_of(c, j), q_of(c), preferred_element_type=F32)
                for c in range(n_chains)]

    m = [None] * n_chains
    acc = [None] * n_chains
    upcoming = scores_of(0)
    for t, (j, mask) in enumerate(tiles):
        scores = upcoming
        if t + 1 < len(tiles):
            upcoming = scores_of(t + 1)
        probs, alphas = [], []
        for c in range(n_chains):
            s = scores[c] if mask is None else jnp.where(mask, scores[c], NEG)
            m_new = jnp.max(s, axis=0, keepdims=True)
            if t > 0:
                m_new = jnp.maximum(m[c], m_new)
                alphas.append(jnp.exp2(m[c] - m_new))
            m[c] = m_new
            probs.append(jnp.exp2(s - m_new).astype(BF16))
        for c in range(n_chains):
            pv = jnp.dot(v_of(c, j), probs[c], preferred_element_type=F32)
            acc[c] = pv if t == 0 else alphas[c] * acc[c] + pv
    return acc


def _normalized(acc):
    return acc[:HEAD_DIM] / acc[HEAD_DIM:HEAD_DIM + 1]


def _attn_params():
    return pltpu.CompilerParams(
        dimension_semantics=("arbitrary", "arbitrary"), vmem_limit_bytes=VMEM_LIMIT)


ATTN_TILE = 2 * Q_TILE


def _tile_pair_getters(k_ref, v_ref):
    def k_of(c, j):
        return jnp.concatenate([k_ref[0, c, 2 * j], k_ref[0, c, 2 * j + 1]], axis=0)

    def v_of(c, j):
        return jnp.concatenate([v_ref[0, c, 2 * j], v_ref[0, c, 2 * j + 1]], axis=1)

    return k_of, v_of


def _causal_then_past(i_static, causal):
    return [(i_static, causal)] + [(j, None) for j in range(i_static)]


def _moba_kernel(q_ref, k_ref, v_ref, o_ref):
    nh = q_ref.shape[1]
    k_of, v_of = _tile_pair_getters(k_ref, v_ref)
    for i_static in range(k_ref.shape[2] * Q_TILE // ATTN_TILE):
        @pl.when(pl.program_id(1) == i_static)
        def _(i_static=i_static):
            kr = lax.broadcasted_iota(jnp.int32, (ATTN_TILE, ATTN_TILE), 0)
            qc = lax.broadcasted_iota(jnp.int32, (ATTN_TILE, ATTN_TILE), 1)
            acc = _flash_unrolled(nh, lambda c: q_ref[0, c], k_of, v_of,
                                  _causal_then_past(i_static, kr <= qc))
            for h in range(nh):
                o_ref[0, h * HEAD_DIM:(h + 1) * HEAD_DIM, :] = (
                    _normalized(acc[h]).astype(o_ref.dtype))


def _moba_part(qa, ka, va):
    b, nh, _, s = qa.shape
    nt = s // Q_TILE
    in_specs = [
        pl.BlockSpec((1, nh, OPERAND_WIDTH, ATTN_TILE), lambda bi, i: (bi, 0, 0, i)),
        pl.BlockSpec((1, nh, nt, Q_TILE, OPERAND_WIDTH), lambda bi, i: (bi, 0, 0, 0, 0)),
        pl.BlockSpec((1, nh, nt, VALUE_ROWS, Q_TILE), lambda bi, i: (bi, 0, 0, 0, 0)),
    ]
    out_spec = pl.BlockSpec((1, nh * HEAD_DIM, ATTN_TILE), lambda bi, i: (bi, 0, i))
    return (qa, ka, va), in_specs, out_spec, jax.ShapeDtypeStruct((b, nh * HEAD_DIM, s), BF16), []


def _diff_kernel(q_ref, k_ref, v_ref, lq1_ref, lk1_ref, lq2_ref, lk2_ref, linit_ref, g_ref, o_ref,
                 qm_ref):
    nh = q_ref.shape[1]
    row = lax.broadcasted_iota(jnp.int32, q_ref.shape[2:], 0)
    for h in range(nh):
        q = q_ref[0, h]
        zero = jnp.zeros_like(q)
        qm_ref[h, :, :ATTN_TILE] = jnp.where(
            (row < DIFF_QK_DIM) | (row >= 2 * DIFF_QK_DIM), q, zero)
        qm_ref[h, :, ATTN_TILE:] = jnp.where(row >= DIFF_QK_DIM, q, zero)
    k_of, v_of = _tile_pair_getters(k_ref, v_ref)
    lam_init = linit_ref[...]
    lam = (jnp.exp(jnp.sum(lq1_ref[...] * lk1_ref[...], axis=-1, keepdims=True))
           - jnp.exp(jnp.sum(lq2_ref[...] * lk2_ref[...], axis=-1, keepdims=True))
           + lam_init)
    for i_static in range(k_ref.shape[2] * Q_TILE // ATTN_TILE):
        @pl.when(pl.program_id(1) == i_static)
        def _(i_static=i_static):
            kr = lax.broadcasted_iota(jnp.int32, (ATTN_TILE, 2 * ATTN_TILE), 0)
            qc = lax.broadcasted_iota(jnp.int32, (ATTN_TILE, 2 * ATTN_TILE), 1)
            causal = kr <= jnp.where(qc >= ATTN_TILE, qc - ATTN_TILE, qc)
            acc = _flash_unrolled(nh, lambda c: qm_ref[c], k_of, v_of,
                                  _causal_then_past(i_static, causal))
            for h in range(nh):
                w = _normalized(acc[h])
                o = w[:, :ATTN_TILE] - lam * w[:, ATTN_TILE:]
                ms = jnp.mean(o * o, axis=0, keepdims=True)
                y = (o * lax.rsqrt(ms + RMS_EPS)) * g_ref[...]
                o_ref[0, h * HEAD_DIM:(h + 1) * HEAD_DIM, :] = (
                    (y * (1.0 - lam_init)).astype(o_ref.dtype))


def _diff_part(qb, kb, vb, lam_q1, lam_k1, lam_q2, lam_k2, lam_init, g_col, layer):
    b, nh, _, s = qb.shape
    nt = s // Q_TILE
    per_layer = lambda arr: pl.BlockSpec(
        (None,) + arr.shape[1:], lambda bi, i: (layer,) + (0,) * (arr.ndim - 1))
    in_specs = [
        pl.BlockSpec((1, nh, OPERAND_WIDTH, ATTN_TILE), lambda bi, i: (bi, 0, 0, i)),
        pl.BlockSpec((1, nh, nt, Q_TILE, OPERAND_WIDTH), lambda bi, i: (bi, 0, 0, 0, 0)),
        pl.BlockSpec((1, nh, nt, VALUE_ROWS, Q_TILE), lambda bi, i: (bi, 0, 0, 0, 0)),
        per_layer(lam_q1), per_layer(lam_k1), per_layer(lam_q2), per_layer(lam_k2),
        per_layer(lam_init), per_layer(g_col),
    ]
    out_spec = pl.BlockSpec((1, nh * HEAD_DIM, ATTN_TILE), lambda bi, i: (bi, 0, i))
    scratch = [pltpu.VMEM((nh, OPERAND_WIDTH, 2 * ATTN_TILE), BF16)]
    return ((qb, kb, vb, lam_q1, lam_k1, lam_q2, lam_k2, lam_init, g_col), in_specs, out_spec,
            jax.ShapeDtypeStruct((b, nh * HEAD_DIM, s), BF16), scratch)


SWA_BLOCKS_PER_STEP = 4


def _swa_kernel(q_ref, k_ref, v_ref, sink_ref, o_ref):
    i = pl.program_id(1)
    kr = lax.broadcasted_iota(jnp.int32, (2 * SWA_BLOCK, SWA_BLOCK), 0)
    qc = lax.broadcasted_iota(jnp.int32, (2 * SWA_BLOCK, SWA_BLOCK), 1)
    dist = qc + SWA_BLOCK - kr
    in_win = (dist >= 0) & (dist < SWA_WINDOW)
    chains = [(b2, kvh) for b2 in range(SWA_BLOCKS_PER_STEP) for kvh in range(SWA_KV_HEADS)]
    blocks, bands = [], []
    for b2 in range(SWA_BLOCKS_PER_STEP):
        n = SWA_BLOCKS_PER_STEP * i + b2
        blocks.append((jnp.maximum(n - 1, 0), n))
        first_row = jnp.where(n > 0, 0, SWA_BLOCK)
        band = jnp.where(in_win & (kr >= first_row), 0.0, NEG)
        bands.append(jnp.concatenate([band] * SWA_GROUP, axis=1))
    scores = []
    for b2, kvh in chains:
        prev, n = blocks[b2]
        cols = slice(b2 * SWA_BLOCK, (b2 + 1) * SWA_BLOCK)
        q = jnp.concatenate([q_ref[0, kvh * SWA_GROUP + g, :, cols]
                             for g in range(SWA_GROUP)], axis=1)
        k = jnp.concatenate([k_ref[0, kvh, prev], k_ref[0, kvh, n]], axis=0)
        scores.append(jnp.dot(k, q, preferred_element_type=F32))
    probs, sink_terms = [], []
    for (b2, kvh), s in zip(chains, scores):
        s = s + bands[b2]
        sink = sink_ref[kvh] * LOG2E
        m = jnp.maximum(jnp.max(s, axis=0, keepdims=True), sink)
        probs.append(jnp.exp2(s - m).astype(BF16))
        sink_terms.append(jnp.exp2(sink - m))
    for (b2, kvh), p, sink_term in zip(chains, probs, sink_terms):
        prev, n = blocks[b2]
        cols = slice(b2 * SWA_BLOCK, (b2 + 1) * SWA_BLOCK)
        v = jnp.concatenate([v_ref[0, kvh, prev], v_ref[0, kvh, n]], axis=1)
        acc = jnp.dot(v, p, preferred_element_type=F32)
        o = acc[:HEAD_DIM] / (acc[HEAD_DIM:HEAD_DIM + 1] + sink_term)
        for g in range(SWA_GROUP):
            r0 = (kvh * SWA_GROUP + g) * HEAD_DIM
            o_ref[0, r0:r0 + HEAD_DIM, cols] = (
                o[:, g * SWA_BLOCK:(g + 1) * SWA_BLOCK].astype(o_ref.dtype))


def _swa_part(qc, kc, vc, sink_rows, layer):
    b, nq, _, s = qc.shape
    nkv = kc.shape[1]
    nsb = s // SWA_BLOCK
    step = SWA_BLOCKS_PER_STEP * SWA_BLOCK
    assert step == ATTN_TILE
    in_specs = [
        pl.BlockSpec((1, nq, OPERAND_WIDTH, step), lambda bi, i: (bi, 0, 0, i)),
        pl.BlockSpec((1, nkv, nsb, SWA_BLOCK, OPERAND_WIDTH), lambda bi, i: (bi, 0, 0, 0, 0)),
        pl.BlockSpec((1, nkv, nsb, VALUE_ROWS, SWA_BLOCK), lambda bi, i: (bi, 0, 0, 0, 0)),
        pl.BlockSpec((None,) + sink_rows.shape[1:], lambda bi, i: (layer, 0, 0, 0)),
    ]
    out_spec = pl.BlockSpec((1, nq * HEAD_DIM, step), lambda bi, i: (bi, 0, i))
    return ((qc, kc, vc, sink_rows), in_specs, out_spec,
            jax.ShapeDtypeStruct((b, nq * HEAD_DIM, s), BF16), [])


def _attention_kernel(*refs, n_in):
    ins = [refs[sum(n_in[:k]):sum(n_in[:k + 1])] for k in range(3)]
    n_all = sum(n_in)
    oa_ref, ob_ref, oc_ref = refs[n_all:n_all + 3]
    qm_ref = refs[n_all + 3]
    _moba_kernel(*ins[0], oa_ref)
    _diff_kernel(*ins[1], ob_ref, qm_ref)
    _swa_kernel(*ins[2], oc_ref)


def _attention_call(moba_part, diff_part, swa_part):
    parts = (moba_part, diff_part, swa_part)
    operands = [a for part in parts for a in part[0]]
    b, _, _, s = moba_part[0][0].shape
    return pl.pallas_call(
        functools.partial(_attention_kernel, n_in=tuple(len(part[0]) for part in parts)),
        grid=(b, s // ATTN_TILE),
        in_specs=[spec for part in parts for spec in part[1]],
        out_specs=tuple(part[2] for part in parts),
        out_shape=tuple(part[3] for part in parts),
        scratch_shapes=[sc for part in parts for sc in part[4]],
        compiler_params=_attn_params(),
        name="attention",
    )(*operands)


def _mix_ffn_kernel(x_ref, ma_ref, mb_ref, mc_ref, woa_ref, wob_ref, woc_ref,
                    g_ref, wg_ref, wu_ref, wd_ref, gf_ref, *refs, final):
    n_conv = len(refs) // 2
    o_ref = refs[n_conv]
    _convert_blocks(refs[:n_conv], refs[n_conv + 1:])
    tdims = (((0,), (0,)), ((), ()))
    x = x_ref[0]
    x = x + lax.dot_general(ma_ref[0], woa_ref[...], tdims, preferred_element_type=F32)
    x = x + lax.dot_general(mb_ref[0], wob_ref[...], tdims, preferred_element_type=F32)
    x = x + lax.dot_general(mc_ref[0], woc_ref[...], tdims, preferred_element_type=F32)
    x = _swiglu_residual(x, g_ref[...], wg_ref, wu_ref, wd_ref)
    if final:
        x = _rmsnorm(x, gf_ref[...])
    o_ref[0] = x


def _mix_ffn_call(x, ma, mb, mc, wo, g, wg, wu, wd, gf, layer, final, later_weights, later_layer):
    b, s, _ = x.shape
    rows = FFN_ROWS
    steps = s // rows
    a_rows, b_rows, c_rows = ma.shape[1], mb.shape[1], mc.shape[1]
    assert a_rows == b_rows and c_rows == a_rows + b_rows
    conv = [_convert_plan(w, later_layer, b * steps, lambda bi, i: bi * steps + i)
            for w in later_weights]
    outs = pl.pallas_call(
        functools.partial(_mix_ffn_kernel, final=final),
        grid=(b, steps),
        in_specs=[
            pl.BlockSpec((1, rows, D_MODEL), lambda bi, i: (bi, i, 0)),
            pl.BlockSpec((1, a_rows, rows), lambda bi, i: (bi, 0, i)),
            pl.BlockSpec((1, b_rows, rows), lambda bi, i: (bi, 0, i)),
            pl.BlockSpec((1, c_rows, rows), lambda bi, i: (bi, 0, i)),
            _layer_resident(wo, layer, a_rows, 0),
            _layer_resident(wo, layer, b_rows, 1),
            _layer_resident(wo, layer, c_rows, 1),
            _layer_resident(g, layer),
            _resident(wg.shape), _resident(wu.shape), _resident(wd.shape),
            _resident((1, D_MODEL)),
        ] + [c[0] for c in conv],
        out_specs=(pl.BlockSpec((1, rows, D_MODEL), lambda bi, i: (bi, i, 0)),)
        + tuple(c[1] for c in conv),
        out_shape=(jax.ShapeDtypeStruct(x.shape, F32),) + tuple(c[2] for c in conv),
        compiler_params=pltpu.CompilerParams(
            dimension_semantics=("arbitrary", "arbitrary"), vmem_limit_bytes=VMEM_LIMIT),
        name="mix_ffn2_final" if final else "mix_ffn2",
    )(x, ma, mb, mc, wo, wo, wo, g, wg, wu, wd, gf, *later_weights)
    return outs[0], outs[1:]


def _split3(x):
    x1 = x.astype(BF16).astype(F32)
    r = x - x1
    x2 = r.astype(BF16).astype(F32)
    x3 = (r - x2).astype(BF16).astype(F32)
    return [x1, x2, x3]


def _bias_rows_kernel(raw_ref, o_ref):
    s = raw_ref.shape[2]
    row = lax.broadcasted_iota(jnp.int32, (16, s), 0)
    lead = AUG_BIAS_ROW - HEAD_DIM
    for hh in range(raw_ref.shape[0]):
        raw = raw_ref[hh]
        pieces = _split3(raw[0:1]) + 2 * _split3(raw[1:2])
        blk = jnp.zeros((16, s), F32)
        for r, piece in enumerate(pieces):
            blk = jnp.where(row == r, piece, blk)
        o_ref[hh] = jnp.concatenate(
            [jnp.zeros((lead, s), F32), blk, jnp.zeros((HEAD_DIM - lead - 16, s), F32)], axis=0)


def _operand_tables(s):
    slopes = _alibi_slopes()
    head_slopes = np.concatenate([
        slopes[SWA_Q_HEADS + DIFF_HEADS:],
        slopes[SWA_Q_HEADS:SWA_Q_HEADS + DIFF_HEADS],
        slopes[:SWA_Q_HEADS],
    ]).astype(np.float32)
    pos = jnp.arange(s, dtype=F32)
    sl = jnp.asarray(head_slopes * np.float32(LOG2E))[:, None]
    raw = jnp.zeros((N_Q_HEADS, 8, s), F32)
    raw = raw.at[:, 0, :].set(-sl * pos[None, :])
    raw = raw.at[:, 1, :].set(jnp.broadcast_to(sl, (N_Q_HEADS, s)))
    augq = pl.pallas_call(
        _bias_rows_kernel,
        out_shape=jax.ShapeDtypeStruct((N_Q_HEADS, HEAD_DIM, s), F32),
        name="bias_rows",
    )(raw)
    blk = jnp.arange(s, dtype=jnp.int32) // MOBA_BLOCK
    onehot = (blk[:, None] == jnp.arange(8, dtype=jnp.int32)[None, :]).astype(F32)
    ones = jnp.ones((s, 3), F32)
    base = jnp.broadcast_to((blk * MOBA_BLOCK).astype(F32)[:, None], (s, 3))
    off = jnp.broadcast_to((jnp.arange(s, dtype=jnp.int32) % MOBA_BLOCK).astype(F32)[:, None], (s, 3))
    augk = jnp.concatenate([
        jnp.zeros((s, HEAD_DIM), F32), onehot, ones, base, off,
        jnp.zeros((s, OPERAND_WIDTH - HEAD_DIM - 17), F32)], axis=1)
    return augq, augk


def kernel(x, norm_ffn1, w1_gate, w1_up, w1_down, norm_mix, w_in, lam_q1, lam_k1, lam_q2, lam_k2,
           diff_subln, sinks, w_out, norm_ffn2, w2_gate, w2_up, w2_down, final_norm):
    b, s, d = x.shape
    depth = w_in.shape[0]
    assert d == D_MODEL and s % ATTN_TILE == 0 and (b * s) % FFN1_ROWS == 0 and s % FFN_ROWS == 0
    augq, augk = _operand_tables(s)
    as_rows = lambda p: p.reshape(depth, 1, -1)
    g1, gm, g2 = as_rows(norm_ffn1), as_rows(norm_mix), as_rows(norm_ffn2)
    ffn1_f32, ffn2_f32 = (w1_gate, w1_up, w1_down), (w2_gate, w2_up, w2_down)
    wt, wo = jnp.swapaxes(w_in.astype(BF16), 1, 2), w_out.astype(BF16)
    lq1, lk1, lq2, lk2 = as_rows(lam_q1), as_rows(lam_k1), as_rows(lam_q2), as_rows(lam_k2)
    lam_init = jnp.asarray([0.8 - 0.6 * math.exp(-0.3 * l) for l in range(depth)],
                           F32).reshape(depth, 1, 1)
    subln = diff_subln.reshape(depth, -1, 1)
    sink_rows = jnp.repeat(sinks.reshape(depth, SWA_KV_HEADS, 1, SWA_GROUP), SWA_BLOCK, axis=3)
    gf = final_norm.reshape(1, d)
    ffn1_w = tuple(w[0].astype(BF16) for w in ffn1_f32)
    for l in range(depth):
        x2, ffn2_w = _ffn_call(x.reshape(b * s, d), g1, *ffn1_w, l, ffn2_f32, l)
        x = x2.reshape(b, s, d)
        qa, qb, qc, ka, kb, kc, va, vb, vc = _proj_call(x, gm, wt, augk, augq, l)
        ma, mb, mc = _attention_call(
            _moba_part(qa, ka, va),
            _diff_part(qb, kb, vb, lq1, lk1, lq2, lk2, lam_init, subln, l),
            _swa_part(qc, kc, vc, sink_rows, l))
        last = l == depth - 1
        x, ffn1_w = _mix_ffn_call(x, ma, mb, mc, wo, g2, *ffn2_w, gf, l, last,
                                  () if last else ffn1_f32, l + 1)
    return x
```

```python
import functools
import math

import numpy as np
import jax
import jax.numpy as jnp
from jax import lax
from jax.experimental import pallas as pl
from jax.experimental.pallas import tpu as pltpu

F32 = jnp.float32
BF16 = jnp.bfloat16

D_MODEL = 1024
D_FF = 2816
HEAD_DIM = 64
MOBA_HEADS = 4
MOBA_BLOCK = 256
MOBA_TOPK = 3
DIFF_HEADS = 4
DIFF_QK_DIM = 32
SWA_Q_HEADS = 8
SWA_KV_HEADS = 2
SWA_GROUP = SWA_Q_HEADS // SWA_KV_HEADS
SWA_WINDOW = 128
SWA_BLOCK = 128
N_ALIBI = MOBA_HEADS + DIFF_HEADS + SWA_Q_HEADS
RMS_EPS = 1e-6
NEG = -1e30

N_Q_HEADS = MOBA_HEADS + DIFF_HEADS + SWA_Q_HEADS
OPERAND_WIDTH = 128
VALUE_ROWS = HEAD_DIM + 16
LOG2E = math.log2(math.e)
AUG_FLAG_ROW = HEAD_DIM
AUG_BIAS_ROW = HEAD_DIM + 8
Q_TILE = 256
FFN_ROWS = 1024
FFN1_ROWS = 1024
FFN_CHUNK = 256
VMEM_LIMIT = 56 * 1024 * 1024


def _alibi_slopes():
    n = N_ALIBI
    return 2.0 ** (-8.0 * (np.arange(n, dtype=np.float32) + 1.0) / n)


def _rmsnorm(x, g):
    ms = jnp.mean(x * x, axis=-1, keepdims=True)
    return (x * lax.rsqrt(ms + RMS_EPS)) * g


def _swiglu_residual(x, g, wg_ref, wu_ref, wd_ref):
    h = _rmsnorm(x, g).astype(BF16)
    gate = jnp.dot(h, wg_ref[...], preferred_element_type=F32)
    up = jnp.dot(h, wu_ref[...], preferred_element_type=F32)
    act = (gate * jax.nn.sigmoid(gate) * up).astype(BF16)
    return x + 0.5 * jnp.dot(act, wd_ref[...], preferred_element_type=F32)


def _resident(shape):
    zeros = (0,) * len(shape)
    return pl.BlockSpec(shape, lambda *_: zeros, pipeline_mode=pl.Buffered(1))


def _layer_resident(stacked, layer, rows=None, row_block=0):
    shape = stacked.shape[1:] if rows is None else (rows,) + stacked.shape[2:]
    index = (layer, row_block) + (0,) * (len(shape) - 1)
    return pl.BlockSpec((None,) + shape, lambda *_: index, pipeline_mode=pl.Buffered(1))


def _convert_plan(w, layer, n_steps, step_index):
    rows, cols = w.shape[1:]
    n_blocks = max(d for d in range(1, n_steps + 1) if rows % d == 0 and (rows // d) % 16 == 0)
    block = lambda *ids: jnp.minimum(step_index(*ids), n_blocks - 1)
    in_spec = pl.BlockSpec((None, rows // n_blocks, cols), lambda *ids: (layer, block(*ids), 0))
    out_spec = pl.BlockSpec((rows // n_blocks, cols), lambda *ids: (block(*ids), 0))
    return in_spec, out_spec, jax.ShapeDtypeStruct((rows, cols), BF16)


def _convert_blocks(in_refs, out_refs):
    for src, dst in zip(in_refs, out_refs):
        dst[...] = src[...].astype(BF16)


def _ffn_kernel(x_ref, g_ref, wg_ref, wu_ref, wd_ref, *refs):
    n_conv = len(refs) // 2
    o_ref = refs[n_conv]
    _convert_blocks(refs[:n_conv], refs[n_conv + 1:])
    for r in range(x_ref.shape[0] // FFN_CHUNK):
        rows = slice(r * FFN_CHUNK, (r + 1) * FFN_CHUNK)
        o_ref[rows, :] = _swiglu_residual(x_ref[rows, :], g_ref[...], wg_ref, wu_ref, wd_ref)


def _ffn_call(x2, g, wg, wu, wd, layer, later_weights, later_layer):
    t = x2.shape[0]
    steps = t // FFN1_ROWS
    conv = [_convert_plan(w, later_layer, steps, lambda i: i) for w in later_weights]
    outs = pl.pallas_call(
        _ffn_kernel,
        grid=(steps,),
        in_specs=[
            pl.BlockSpec((FFN1_ROWS, D_MODEL), lambda i: (i, 0)),
            _layer_resident(g, layer),
            _resident(wg.shape), _resident(wu.shape), _resident(wd.shape),
        ] + [c[0] for c in conv],
        out_specs=(pl.BlockSpec((FFN1_ROWS, D_MODEL), lambda i: (i, 0)),) + tuple(c[1] for c in conv),
        out_shape=(jax.ShapeDtypeStruct(x2.shape, F32),) + tuple(c[2] for c in conv),
        compiler_params=pltpu.CompilerParams(
            dimension_semantics=("arbitrary",), vmem_limit_bytes=VMEM_LIMIT),
        name="ffn1",
    )(x2, g, wg, wu, wd, *later_weights)
    return outs[0], outs[1:]


def _split_hi_lo(x):
    hi = x.astype(BF16)
    lo = (x - hi.astype(F32)).astype(BF16)
    return hi, lo


_AQ0 = 0
_AK0 = _AQ0 + MOBA_HEADS * HEAD_DIM
_AV0 = _AK0 + MOBA_HEADS * HEAD_DIM
_BQ0 = _AV0 + MOBA_HEADS * HEAD_DIM
_BK0 = _BQ0 + DIFF_HEADS * HEAD_DIM
_BV0 = _BK0 + DIFF_HEADS * HEAD_DIM
_CQ0 = _BV0 + DIFF_HEADS * HEAD_DIM
_CK0 = _CQ0 + SWA_Q_HEADS * HEAD_DIM
_CV0 = _CK0 + SWA_KV_HEADS * HEAD_DIM
PROJ_ROWS = 4 * Q_TILE


def _proj_kernel(x_ref, g_ref, wt_ref, augk_ref, augq_ref,
                 qa_ref, qb_ref, qc_ref, ka_ref, kb_ref, kc_ref, va_ref, vb_ref, vc_ref,
                 kmean_ref):
    i = pl.program_id(1)
    t = x_ref.shape[1]
    blocks_per_step = t // MOBA_BLOCK

    @pl.when(i == 0)
    def _():
        kmean_ref[...] = jnp.zeros_like(kmean_ref)

    h = _rmsnorm(x_ref[0], g_ref[...]).astype(BF16)
    p = lax.dot_general(wt_ref[...], h, (((1,), (1,)), ((), ())),
                        preferred_element_type=F32)
    augk = augk_ref[...]
    lane = lax.broadcasted_iota(jnp.int32, (t, OPERAND_WIDTH), 1)

    def key_pair(r0):
        kt = p[r0:r0 + 2 * HEAD_DIM].T
        even = jnp.where(lane < HEAD_DIM, kt, augk).astype(BF16)
        odd = jnp.where(lane < HEAD_DIM, pltpu.roll(kt, HEAD_DIM, 1), augk).astype(BF16)
        return kt, (even, odd)

    def store_tiles(ref, hh, slab, rows, axis):
        for n in range(slab.shape[axis] // rows):
            piece = slab[n * rows:(n + 1) * rows] if axis == 0 else slab[:, n * rows:(n + 1) * rows]
            ref[0, hh, n] = piece

    for pair in range(MOBA_HEADS // 2):
        kt, slabs = key_pair(_AK0 + pair * 2 * HEAD_DIM)
        for blk in range(blocks_per_step):
            kmean_ref[pair, pl.ds(blocks_per_step * i + blk, 1), :] = jnp.mean(
                kt[blk * MOBA_BLOCK:(blk + 1) * MOBA_BLOCK], axis=0, keepdims=True)
        for par in range(2):
            store_tiles(ka_ref, 2 * pair + par, slabs[par], Q_TILE, 0)
    for pair in range(DIFF_HEADS // 2):
        _, slabs = key_pair(_BK0 + pair * 2 * HEAD_DIM)
        for par in range(2):
            store_tiles(kb_ref, 2 * pair + par, slabs[par], Q_TILE, 0)
    for pair in range(SWA_KV_HEADS // 2):
        _, slabs = key_pair(_CK0 + pair * 2 * HEAD_DIM)
        for par in range(2):
            store_tiles(kc_ref, 2 * pair + par, slabs[par], SWA_BLOCK, 0)

    ones_rows = jnp.where(
        lax.broadcasted_iota(jnp.int32, (VALUE_ROWS - HEAD_DIM, t), 0) == 0, 1.0, 0.0)

    def v_slab(r0):
        return jnp.concatenate([p[r0:r0 + HEAD_DIM], ones_rows], axis=0).astype(BF16)

    for hh in range(MOBA_HEADS):
        store_tiles(va_ref, hh, v_slab(_AV0 + hh * HEAD_DIM), Q_TILE, 1)
    for hh in range(DIFF_HEADS):
        store_tiles(vb_ref, hh, v_slab(_BV0 + hh * HEAD_DIM), Q_TILE, 1)
    for hh in range(SWA_KV_HEADS):
        store_tiles(vc_ref, hh, v_slab(_CV0 + hh * HEAD_DIM), SWA_BLOCK, 1)

    n_blocks = kmean_ref.shape[1]
    jrow = lax.broadcasted_iota(jnp.int32, (n_blocks, t), 0)
    qcol = lax.broadcasted_iota(jnp.int32, (n_blocks, t), 1)
    qblk = blocks_per_step * i + qcol // MOBA_BLOCK
    zeros_rows = jnp.zeros((HEAD_DIM, t), F32)
    moba_scale = HEAD_DIM ** -0.5 * LOG2E
    for hh in range(MOBA_HEADS):
        qf = p[_AQ0 + hh * HEAD_DIM:_AQ0 + (hh + 1) * HEAD_DIM]
        q_pad = jnp.concatenate([qf, zeros_rows] if hh % 2 == 0 else [zeros_rows, qf], axis=0)
        km_hi, km_lo = _split_hi_lo(kmean_ref[hh // 2])
        q_hi, q_lo = _split_hi_lo(q_pad)
        gate = (jnp.dot(km_hi, q_hi, preferred_element_type=F32)
                + jnp.dot(km_hi, q_lo, preferred_element_type=F32)
                + jnp.dot(km_lo, q_hi, preferred_element_type=F32))
        rank = jnp.zeros((n_blocks, t), F32)
        for m in range(n_blocks):
            gm = gate[m:m + 1, :]
            beats = (gm > gate) | ((gm == gate) & (m < jrow))
            rank = rank + jnp.where(beats & (m < qblk), 1.0, 0.0)
        chosen = ((rank < MOBA_TOPK) & (jrow < qblk)) | (jrow == qblk)
        flags = jnp.where(chosen, 0.0, NEG)
        aug = augq_ref[hh] + jnp.concatenate(
            [flags, jnp.zeros((HEAD_DIM - n_blocks, t), F32)], axis=0)
        qa_ref[0, hh, :HEAD_DIM, :] = (qf * moba_scale).astype(BF16)
        qa_ref[0, hh, HEAD_DIM:, :] = aug.astype(BF16)
    diff_scale = DIFF_QK_DIM ** -0.5 * LOG2E
    for hh in range(DIFF_HEADS):
        qf = p[_BQ0 + hh * HEAD_DIM:_BQ0 + (hh + 1) * HEAD_DIM]
        qb_ref[0, hh, :HEAD_DIM, :] = (qf * diff_scale).astype(BF16)
        qb_ref[0, hh, HEAD_DIM:, :] = augq_ref[MOBA_HEADS + hh].astype(BF16)
    swa_scale = HEAD_DIM ** -0.5 * LOG2E
    for hh in range(SWA_Q_HEADS):
        qf = p[_CQ0 + hh * HEAD_DIM:_CQ0 + (hh + 1) * HEAD_DIM]
        qc_ref[0, hh, :HEAD_DIM, :] = (qf * swa_scale).astype(BF16)
        qc_ref[0, hh, HEAD_DIM:, :] = augq_ref[MOBA_HEADS + DIFF_HEADS + hh].astype(BF16)


def _proj_call(x, g, wt, augk, augq, layer):
    b, s, _ = x.shape
    nt = s // Q_TILE
    nsb = s // SWA_BLOCK
    rows = PROJ_ROWS
    tiles = rows // Q_TILE
    sblocks = rows // SWA_BLOCK
    out_shape = (
        jax.ShapeDtypeStruct((b, MOBA_HEADS, OPERAND_WIDTH, s), BF16),
        jax.ShapeDtypeStruct((b, DIFF_HEADS, OPERAND_WIDTH, s), BF16),
        jax.ShapeDtypeStruct((b, SWA_Q_HEADS, OPERAND_WIDTH, s), BF16),
        jax.ShapeDtypeStruct((b, MOBA_HEADS, nt, Q_TILE, OPERAND_WIDTH), BF16),
        jax.ShapeDtypeStruct((b, DIFF_HEADS, nt, Q_TILE, OPERAND_WIDTH), BF16),
        jax.ShapeDtypeStruct((b, SWA_KV_HEADS, nsb, SWA_BLOCK, OPERAND_WIDTH), BF16),
        jax.ShapeDtypeStruct((b, MOBA_HEADS, nt, VALUE_ROWS, Q_TILE), BF16),
        jax.ShapeDtypeStruct((b, DIFF_HEADS, nt, VALUE_ROWS, Q_TILE), BF16),
        jax.ShapeDtypeStruct((b, SWA_KV_HEADS, nsb, VALUE_ROWS, SWA_BLOCK), BF16),
    )
    out_specs = (
        pl.BlockSpec((1, MOBA_HEADS, OPERAND_WIDTH, rows), lambda bi, i: (bi, 0, 0, i)),
        pl.BlockSpec((1, DIFF_HEADS, OPERAND_WIDTH, rows), lambda bi, i: (bi, 0, 0, i)),
        pl.BlockSpec((1, SWA_Q_HEADS, OPERAND_WIDTH, rows), lambda bi, i: (bi, 0, 0, i)),
        pl.BlockSpec((1, MOBA_HEADS, tiles, Q_TILE, OPERAND_WIDTH), lambda bi, i: (bi, 0, i, 0, 0)),
        pl.BlockSpec((1, DIFF_HEADS, tiles, Q_TILE, OPERAND_WIDTH), lambda bi, i: (bi, 0, i, 0, 0)),
        pl.BlockSpec((1, SWA_KV_HEADS, sblocks, SWA_BLOCK, OPERAND_WIDTH),
                     lambda bi, i: (bi, 0, i, 0, 0)),
        pl.BlockSpec((1, MOBA_HEADS, tiles, VALUE_ROWS, Q_TILE), lambda bi, i: (bi, 0, i, 0, 0)),
        pl.BlockSpec((1, DIFF_HEADS, tiles, VALUE_ROWS, Q_TILE), lambda bi, i: (bi, 0, i, 0, 0)),
        pl.BlockSpec((1, SWA_KV_HEADS, sblocks, VALUE_ROWS, SWA_BLOCK),
                     lambda bi, i: (bi, 0, i, 0, 0)),
    )
    return pl.pallas_call(
        _proj_kernel,
        grid=(b, s // rows),
        in_specs=[
            pl.BlockSpec((1, rows, D_MODEL), lambda bi, i: (bi, i, 0)),
            _layer_resident(g, layer),
            _layer_resident(wt, layer),
            pl.BlockSpec((rows, OPERAND_WIDTH), lambda bi, i: (i, 0)),
            pl.BlockSpec((N_Q_HEADS, HEAD_DIM, rows), lambda bi, i: (0, 0, i)),
        ],
        out_specs=out_specs,
        out_shape=out_shape,
        scratch_shapes=[pltpu.VMEM((MOBA_HEADS // 2, s // MOBA_BLOCK, OPERAND_WIDTH), F32)],
        compiler_params=pltpu.CompilerParams(
            dimension_semantics=("arbitrary", "arbitrary"), vmem_limit_bytes=VMEM_LIMIT),
        name="proj",
    )(x, g, wt, augk, augq)


def _flash_unrolled(n_chains, q_of, k_of, v_of, tiles):
    def scores_of(t):
        j, _ = tiles[t]
        return [jnp.dot(k_of(c, j), q_of(c), preferred_element_type=F32)
                for c in range(n_chains)]

    m = [None] * n_chains
    acc = [None] * n_chains
    upcoming = scores_of(0)
    for t, (j, mask) in enumerate(tiles):
        scores = upcoming
        if t + 1 < len(tiles):
            upcoming = scores_of(t + 1)
        probs, alphas = [], []
        for c in range(n_chains):
            s = scores[c] if mask is None else jnp.where(mask, scores[c], NEG)
            m_new = jnp.max(s, axis=0, keepdims=True)
            if t > 0:
                m_new = jnp.maximum(m[c], m_new)
                alphas.append(jnp.exp2(m[c] - m_new))
            m[c] = m_new
            probs.append(jnp.exp2(s - m_new).astype(BF16))
        for c in range(n_chains):
            pv = jnp.dot(v_of(c, j), probs[c], preferred_element_type=F32)
            acc[c] = pv if t == 0 else alphas[c] * acc[c] + pv
    return acc


def _normalized(acc):
    return acc[:HEAD_DIM] / acc[HEAD_DIM:HEAD_DIM + 1]


def _attn_params():
    return pltpu.CompilerParams(
        dimension_semantics=("arbitrary", "arbitrary"), vmem_limit_bytes=VMEM_LIMIT)


ATTN_TILE = 2 * Q_TILE


def _tile_pair_getters(k_ref, v_ref):
    def k_of(c, j):
        return jnp.concatenate([k_ref[0, c, 2 * j], k_ref[0, c, 2 * j + 1]], axis=0)

    def v_of(c, j):
        return jnp.concatenate([v_ref[0, c, 2 * j], v_ref[0, c, 2 * j + 1]], axis=1)

    return k_of, v_of


def _causal_then_past(i_static, causal):
    return [(i_static, causal)] + [(j, None) for j in range(i_static)]


def _moba_kernel(q_ref, k_ref, v_ref, o_ref):
    nh = q_ref.shape[1]
    k_of, v_of = _tile_pair_getters(k_ref, v_ref)
    for i_static in range(k_ref.shape[2] * Q_TILE // ATTN_TILE):
        @pl.when(pl.program_id(1) == i_static)
        def _(i_static=i_static):
            kr = lax.broadcasted_iota(jnp.int32, (ATTN_TILE, ATTN_TILE), 0)
            qc = lax.broadcasted_iota(jnp.int32, (ATTN_TILE, ATTN_TILE), 1)
            acc = _flash_unrolled(nh, lambda c: q_ref[0, c], k_of, v_of,
                                  _causal_then_past(i_static, kr <= qc))
            for h in range(nh):
                o_ref[0, h * HEAD_DIM:(h + 1) * HEAD_DIM, :] = (
                    _normalized(acc[h]).astype(o_ref.dtype))


def _moba_part(qa, ka, va):
    b, nh, _, s = qa.shape
    nt = s // Q_TILE
    in_specs = [
        pl.BlockSpec((1, nh, OPERAND_WIDTH, ATTN_TILE), lambda bi, i: (bi, 0, 0, i)),
        pl.BlockSpec((1, nh, nt, Q_TILE, OPERAND_WIDTH), lambda bi, i: (bi, 0, 0, 0, 0)),
        pl.BlockSpec((1, nh, nt, VALUE_ROWS, Q_TILE), lambda bi, i: (bi, 0, 0, 0, 0)),
    ]
    out_spec = pl.BlockSpec((1, nh * HEAD_DIM, ATTN_TILE), lambda bi, i: (bi, 0, i))
    return (qa, ka, va), in_specs, out_spec, jax.ShapeDtypeStruct((b, nh * HEAD_DIM, s), BF16), []


def _diff_kernel(q_ref, k_ref, v_ref, lq1_ref, lk1_ref, lq2_ref, lk2_ref, linit_ref, g_ref, o_ref,
                 qm_ref):
    nh = q_ref.shape[1]
    row = lax.broadcasted_iota(jnp.int32, q_ref.shape[2:], 0)
    for h in range(nh):
        q = q_ref[0, h]
        zero = jnp.zeros_like(q)
        qm_ref[h, :, :ATTN_TILE] = jnp.where(
            (row < DIFF_QK_DIM) | (row >= 2 * DIFF_QK_DIM), q, zero)
        qm_ref[h, :, ATTN_TILE:] = jnp.where(row >= DIFF_QK_DIM, q, zero)
    k_of, v_of = _tile_pair_getters(k_ref, v_ref)
    lam_init = linit_ref[...]
    lam = (jnp.exp(jnp.sum(lq1_ref[...] * lk1_ref[...], axis=-1, keepdims=True))
           - jnp.exp(jnp.sum(lq2_ref[...] * lk2_ref[...], axis=-1, keepdims=True))
           + lam_init)
    for i_static in range(k_ref.shape[2] * Q_TILE // ATTN_TILE):
        @pl.when(pl.program_id(1) == i_static)
        def _(i_static=i_static):
            kr = lax.broadcasted_iota(jnp.int32, (ATTN_TILE, 2 * ATTN_TILE), 0)
            qc = lax.broadcasted_iota(jnp.int32, (ATTN_TILE, 2 * ATTN_TILE), 1)
            causal = kr <= jnp.where(qc >= ATTN_TILE, qc - ATTN_TILE, qc)
            acc = _flash_unrolled(nh, lambda c: qm_ref[c], k_of, v_of,
                                  _causal_then_past(i_static, causal))
            for h in range(nh):
                w = _normalized(acc[h])
                o = w[:, :ATTN_TILE] - lam * w[:, ATTN_TILE:]
                ms = jnp.mean(o * o, axis=0, keepdims=True)
                y = (o * lax.rsqrt(ms + RMS_EPS)) * g_ref[...]
                o_ref[0, h * HEAD_DIM:(h + 1) * HEAD_DIM, :] = (
                    (y * (1.0 - lam_init)).astype(o_ref.dtype))


def _diff_part(qb, kb, vb, lam_q1, lam_k1, lam_q2, lam_k2, lam_init, g_col, layer):
    b, nh, _, s = qb.shape
    nt = s // Q_TILE
    per_layer = lambda arr: pl.BlockSpec(
        (None,) + arr.shape[1:], lambda bi, i: (layer,) + (0,) * (arr.ndim - 1))
    in_specs = [
        pl.BlockSpec((1, nh, OPERAND_WIDTH, ATTN_TILE), lambda bi, i: (bi, 0, 0, i)),
        pl.BlockSpec((1, nh, nt, Q_TILE, OPERAND_WIDTH), lambda bi, i: (bi, 0, 0, 0, 0)),
        pl.BlockSpec((1, nh, nt, VALUE_ROWS, Q_TILE), lambda bi, i: (bi, 0, 0, 0, 0)),
        per_layer(lam_q1), per_layer(lam_k1), per_layer(lam_q2), per_layer(lam_k2),
        per_layer(lam_init), per_layer(g_col),
    ]
    out_spec = pl.BlockSpec((1, nh * HEAD_DIM, ATTN_TILE), lambda bi, i: (bi, 0, i))
    scratch = [pltpu.VMEM((nh, OPERAND_WIDTH, 2 * ATTN_TILE), BF16)]
    return ((qb, kb, vb, lam_q1, lam_k1, lam_q2, lam_k2, lam_init, g_col), in_specs, out_spec,
            jax.ShapeDtypeStruct((b, nh * HEAD_DIM, s), BF16), scratch)


SWA_BLOCKS_PER_STEP = 4


def _swa_kernel(q_ref, k_ref, v_ref, sink_ref, o_ref):
    i = pl.program_id(1)
    kr = lax.broadcasted_iota(jnp.int32, (2 * SWA_BLOCK, SWA_BLOCK), 0)
    qc = lax.broadcasted_iota(jnp.int32, (2 * SWA_BLOCK, SWA_BLOCK), 1)
    dist = qc + SWA_BLOCK - kr
    in_win = (dist >= 0) & (dist < SWA_WINDOW)
    chains = [(b2, kvh) for b2 in range(SWA_BLOCKS_PER_STEP) for kvh in range(SWA_KV_HEADS)]
    blocks, bands = [], []
    for b2 in range(SWA_BLOCKS_PER_STEP):
        n = SWA_BLOCKS_PER_STEP * i + b2
        blocks.append((jnp.maximum(n - 1, 0), n))
        first_row = jnp.where(n > 0, 0, SWA_BLOCK)
        band = jnp.where(in_win & (kr >= first_row), 0.0, NEG)
        bands.append(jnp.concatenate([band] * SWA_GROUP, axis=1))
    scores = []
    for b2, kvh in chains:
        prev, n = blocks[b2]
        cols = slice(b2 * SWA_BLOCK, (b2 + 1) * SWA_BLOCK)
        q = jnp.concatenate([q_ref[0, kvh * SWA_GROUP + g, :, cols]
                             for g in range(SWA_GROUP)], axis=1)
        k = jnp.concatenate([k_ref[0, kvh, prev], k_ref[0, kvh, n]], axis=0)
        scores.append(jnp.dot(k, q, preferred_element_type=F32))
    probs, sink_terms = [], []
    for (b2, kvh), s in zip(chains, scores):
        s = s + bands[b2]
        sink = sink_ref[kvh] * LOG2E
        m = jnp.maximum(jnp.max(s, axis=0, keepdims=True), sink)
        probs.append(jnp.exp2(s - m).astype(BF16))
        sink_terms.append(jnp.exp2(sink - m))
    for (b2, kvh), p, sink_term in zip(chains, probs, sink_terms):
        prev, n = blocks[b2]
        cols = slice(b2 * SWA_BLOCK, (b2 + 1) * SWA_BLOCK)
        v = jnp.concatenate([v_ref[0, kvh, prev], v_ref[0, kvh, n]], axis=1)
        acc = jnp.dot(v, p, preferred_element_type=F32)
        o = acc[:HEAD_DIM] / (acc[HEAD_DIM:HEAD_DIM + 1] + sink_term)
        for g in range(SWA_GROUP):
            r0 = (kvh * SWA_GROUP + g) * HEAD_DIM
            o_ref[0, r0:r0 + HEAD_DIM, cols] = (
                o[:, g * SWA_BLOCK:(g + 1) * SWA_BLOCK].astype(o_ref.dtype))


def _swa_part(qc, kc, vc, sink_rows, layer):
    b, nq, _, s = qc.shape
    nkv = kc.shape[1]
    nsb = s // SWA_BLOCK
    step = SWA_BLOCKS_PER_STEP * SWA_BLOCK
    assert step == ATTN_TILE
    in_specs = [
        pl.BlockSpec((1, nq, OPERAND_WIDTH, step), lambda bi, i: (bi, 0, 0, i)),
        pl.BlockSpec((1, nkv, nsb, SWA_BLOCK, OPERAND_WIDTH), lambda bi, i: (bi, 0, 0, 0, 0)),
        pl.BlockSpec((1, nkv, nsb, VALUE_ROWS, SWA_BLOCK), lambda bi, i: (bi, 0, 0, 0, 0)),
        pl.BlockSpec((None,) + sink_rows.shape[1:], lambda bi, i: (layer, 0, 0, 0)),
    ]
    out_spec = pl.BlockSpec((1, nq * HEAD_DIM, step), lambda bi, i: (bi, 0, i))
    return ((qc, kc, vc, sink_rows), in_specs, out_spec,
            jax.ShapeDtypeStruct((b, nq * HEAD_DIM, s), BF16), [])


def _attention_kernel(*refs, n_in):
    ins = [refs[sum(n_in[:k]):sum(n_in[:k + 1])] for k in range(3)]
    n_all = sum(n_in)
    oa_ref, ob_ref, oc_ref = refs[n_all:n_all + 3]
    qm_ref = refs[n_all + 3]
    _moba_kernel(*ins[0], oa_ref)
    _diff_kernel(*ins[1], ob_ref, qm_ref)
    _swa_kernel(*ins[2], oc_ref)


def _attention_call(moba_part, diff_part, swa_part):
    parts = (moba_part, diff_part, swa_part)
    operands = [a for part in parts for a in part[0]]
    b, _, _, s = moba_part[0][0].shape
    return pl.pallas_call(
        functools.partial(_attention_kernel, n_in=tuple(len(part[0]) for part in parts)),
        grid=(b, s // ATTN_TILE),
        in_specs=[spec for part in parts for spec in part[1]],
        out_specs=tuple(part[2] for part in parts),
        out_shape=tuple(part[3] for part in parts),
        scratch_shapes=[sc for part in parts for sc in part[4]],
        compiler_params=_attn_params(),
        name="attention",
    )(*operands)


def _mix_ffn_kernel(x_ref, ma_ref, mb_ref, mc_ref, woa_ref, wob_ref, woc_ref,
                    g_ref, wg_ref, wu_ref, wd_ref, gf_ref, *refs, final):
    n_conv = len(refs) // 2
    o_ref = refs[n_conv]
    _convert_blocks(refs[:n_conv], refs[n_conv + 1:])
    tdims = (((0,), (0,)), ((), ()))
    for r in range(x_ref.shape[1] // FFN_CHUNK):
        rows = slice(r * FFN_CHUNK, (r + 1) * FFN_CHUNK)
        x = x_ref[0, rows, :]
        x = x + lax.dot_general(ma_ref[0, :, rows], woa_ref[...], tdims, preferred_element_type=F32)
        x = x + lax.dot_general(mb_ref[0, :, rows], wob_ref[...], tdims, preferred_element_type=F32)
        x = x + lax.dot_general(mc_ref[0, :, rows], woc_ref[...], tdims, preferred_element_type=F32)
        x = _swiglu_residual(x, g_ref[...], wg_ref, wu_ref, wd_ref)
        if final:
            x = _rmsnorm(x, gf_ref[...])
        o_ref[0, rows, :] = x


def _mix_ffn_call(x, ma, mb, mc, wo, g, wg, wu, wd, gf, layer, final, later_weights, later_layer):
    b, s, _ = x.shape
    rows = FFN_ROWS
    steps = s // rows
    a_rows, b_rows, c_rows = ma.shape[1], mb.shape[1], mc.shape[1]
    assert a_rows == b_rows and c_rows == a_rows + b_rows
    conv = [_convert_plan(w, later_layer, b * steps, lambda bi, i: bi * steps + i)
            for w in later_weights]
    outs = pl.pallas_call(
        functools.partial(_mix_ffn_kernel, final=final),
        grid=(b, steps),
        in_specs=[
            pl.BlockSpec((1, rows, D_MODEL), lambda bi, i: (bi, i, 0)),
            pl.BlockSpec((1, a_rows, rows), lambda bi, i: (bi, 0, i)),
            pl.BlockSpec((1, b_rows, rows), lambda bi, i: (bi, 0, i)),
            pl.BlockSpec((1, c_rows, rows), lambda bi, i: (bi, 0, i)),
            _layer_resident(wo, layer, a_rows, 0),
            _layer_resident(wo, layer, b_rows, 1),
            _layer_resident(wo, layer, c_rows, 1),
            _layer_resident(g, layer),
            _resident(wg.shape), _resident(wu.shape), _resident(wd.shape),
            _resident((1, D_MODEL)),
        ] + [c[0] for c in conv],
        out_specs=(pl.BlockSpec((1, rows, D_MODEL), lambda bi, i: (bi, i, 0)),)
        + tuple(c[1] for c in conv),
        out_shape=(jax.ShapeDtypeStruct(x.shape, F32),) + tuple(c[2] for c in conv),
        compiler_params=pltpu.CompilerParams(
            dimension_semantics=("arbitrary", "arbitrary"), vmem_limit_bytes=VMEM_LIMIT),
        name="mix_ffn2_final" if final else "mix_ffn2",
    )(x, ma, mb, mc, wo, wo, wo, g, wg, wu, wd, gf, *later_weights)
    return outs[0], outs[1:]


def _split3(x):
    x1 = x.astype(BF16).astype(F32)
    r = x - x1
    x2 = r.astype(BF16).astype(F32)
    x3 = (r - x2).astype(BF16).astype(F32)
    return [x1, x2, x3]


def _bias_rows_kernel(raw_ref, o_ref):
    s = raw_ref.shape[2]
    row = lax.broadcasted_iota(jnp.int32, (16, s), 0)
    lead = AUG_BIAS_ROW - HEAD_DIM
    for hh in range(raw_ref.shape[0]):
        raw = raw_ref[hh]
        pieces = _split3(raw[0:1]) + 2 * _split3(raw[1:2])
        blk = jnp.zeros((16, s), F32)
        for r, piece in enumerate(pieces):
            blk = jnp.where(row == r, piece, blk)
        o_ref[hh] = jnp.concatenate(
            [jnp.zeros((lead, s), F32), blk, jnp.zeros((HEAD_DIM - lead - 16, s), F32)], axis=0)


def _operand_tables(s):
    slopes = _alibi_slopes()
    head_slopes = np.concatenate([
        slopes[SWA_Q_HEADS + DIFF_HEADS:],
        slopes[SWA_Q_HEADS:SWA_Q_HEADS + DIFF_HEADS],
        slopes[:SWA_Q_HEADS],
    ]).astype(np.float32)
    pos = jnp.arange(s, dtype=F32)
    sl = jnp.asarray(head_slopes * np.float32(LOG2E))[:, None]
    raw = jnp.zeros((N_Q_HEADS, 8, s), F32)
    raw = raw.at[:, 0, :].set(-sl * pos[None, :])
    raw = raw.at[:, 1, :].set(jnp.broadcast_to(sl, (N_Q_HEADS, s)))
    augq = pl.pallas_call(
        _bias_rows_kernel,
        out_shape=jax.ShapeDtypeStruct((N_Q_HEADS, HEAD_DIM, s), F32),
        name="bias_rows",
    )(raw)
    blk = jnp.arange(s, dtype=jnp.int32) // MOBA_BLOCK
    onehot = (blk[:, None] == jnp.arange(8, dtype=jnp.int32)[None, :]).astype(F32)
    ones = jnp.ones((s, 3), F32)
    base = jnp.broadcast_to((blk * MOBA_BLOCK).astype(F32)[:, None], (s, 3))
    off = jnp.broadcast_to((jnp.arange(s, dtype=jnp.int32) % MOBA_BLOCK).astype(F32)[:, None], (s, 3))
    augk = jnp.concatenate([
        jnp.zeros((s, HEAD_DIM), F32), onehot, ones, base, off,
        jnp.zeros((s, OPERAND_WIDTH - HEAD_DIM - 17), F32)], axis=1)
    return augq, augk


def kernel(x, norm_ffn1, w1_gate, w1_up, w1_down, norm_mix, w_in, lam_q1, lam_k1, lam_q2, lam_k2,
           diff_subln, sinks, w_out, norm_ffn2, w2_gate, w2_up, w2_down, final_norm):
    b, s, d = x.shape
    depth = w_in.shape[0]
    assert d == D_MODEL and s % ATTN_TILE == 0 and (b * s) % FFN1_ROWS == 0 and s % FFN_ROWS == 0
    augq, augk = _operand_tables(s)
    as_rows = lambda p: p.reshape(depth, 1, -1)
    g1, gm, g2 = as_rows(norm_ffn1), as_rows(norm_mix), as_rows(norm_ffn2)
    ffn1_f32, ffn2_f32 = (w1_gate, w1_up, w1_down), (w2_gate, w2_up, w2_down)
    wt, wo = jnp.swapaxes(w_in.astype(BF16), 1, 2), w_out.astype(BF16)
    lq1, lk1, lq2, lk2 = as_rows(lam_q1), as_rows(lam_k1), as_rows(lam_q2), as_rows(lam_k2)
    lam_init = jnp.asarray([0.8 - 0.6 * math.exp(-0.3 * l) for l in range(depth)],
                           F32).reshape(depth, 1, 1)
    subln = diff_subln.reshape(depth, -1, 1)
    sink_rows = jnp.repeat(sinks.reshape(depth, SWA_KV_HEADS, 1, SWA_GROUP), SWA_BLOCK, axis=3)
    gf = final_norm.reshape(1, d)
    ffn1_w = tuple(w[0].astype(BF16) for w in ffn1_f32)
    for l in range(depth):
        x2, ffn2_w = _ffn_call(x.reshape(b * s, d), g1, *ffn1_w, l, ffn2_f32, l)
        x = x2.reshape(b, s, d)
        qa, qb, qc, ka, kb, kc, va, vb, vc = _proj_call(x, gm, wt, augk, augq, l)
        ma, mb, mc = _attention_call(
            _moba_part(qa, ka, va),
            _diff_part(qb, kb, vb, lq1, lk1, lq2, lk2, lam_init, subln, l),
            _swa_part(qc, kc, vc, sink_rows, l))
        last = l == depth - 1
        x, ffn1_w = _mix_ffn_call(x, ma, mb, mc, wo, g2, *ffn2_w, gf, l, last,
                                  () if last else ffn1_f32, l + 1)
    return x
```

```python
import functools
import math

import numpy as np
import jax
import jax.numpy as jnp
from jax import lax
from jax.experimental import pallas as pl
from jax.experimental.pallas import tpu as pltpu

F32 = jnp.float32
BF16 = jnp.bfloat16

D_MODEL = 1024
D_FF = 2816
HEAD_DIM = 64
MOBA_HEADS = 4
MOBA_BLOCK = 256
MOBA_TOPK = 3
DIFF_HEADS = 4
DIFF_QK_DIM = 32
SWA_Q_HEADS = 8
SWA_KV_HEADS = 2
SWA_GROUP = SWA_Q_HEADS // SWA_KV_HEADS
SWA_WINDOW = 128
SWA_BLOCK = 128
N_ALIBI = MOBA_HEADS + DIFF_HEADS + SWA_Q_HEADS
RMS_EPS = 1e-6
NEG = -1e30

N_Q_HEADS = MOBA_HEADS + DIFF_HEADS + SWA_Q_HEADS
OPERAND_WIDTH = 128
VALUE_ROWS = HEAD_DIM + 16
LOG2E = math.log2(math.e)
AUG_FLAG_ROW = HEAD_DIM
AUG_BIAS_ROW = HEAD_DIM + 8
Q_TILE = 256
FFN_ROWS = 1024
FFN1_ROWS = 1024
FFN_CHUNK = 256
VMEM_LIMIT = 56 * 1024 * 1024


def _alibi_slopes():
    n = N_ALIBI
    return 2.0 ** (-8.0 * (np.arange(n, dtype=np.float32) + 1.0) / n)


def _rmsnorm(x, g):
    ms = jnp.mean(x * x, axis=-1, keepdims=True)
    return (x * lax.rsqrt(ms + RMS_EPS)) * g


def _swiglu_residual(x, g, wg_ref, wu_ref, wd_ref):
    h = _rmsnorm(x, g).astype(BF16)
    gate = jnp.dot(h, wg_ref[...], preferred_element_type=F32)
    up = jnp.dot(h, wu_ref[...], preferred_element_type=F32)
    act = (gate * jax.nn.sigmoid(gate) * up).astype(BF16)
    return x + 0.5 * jnp.dot(act, wd_ref[...], preferred_element_type=F32)


def _resident(shape):
    zeros = (0,) * len(shape)
    return pl.BlockSpec(shape, lambda *_: zeros, pipeline_mode=pl.Buffered(1))


def _layer_resident(stacked, layer, rows=None, row_block=0):
    shape = stacked.shape[1:] if rows is None else (rows,) + stacked.shape[2:]
    index = (layer, row_block) + (0,) * (len(shape) - 1)
    return pl.BlockSpec((None,) + shape, lambda *_: index, pipeline_mode=pl.Buffered(1))


def _convert_plan(w, layer, n_steps, step_index):
    rows, cols = w.shape[1:]
    n_blocks = max(d for d in range(1, n_steps + 1) if rows % d == 0 and (rows // d) % 16 == 0)
    block = lambda *ids: jnp.minimum(step_index(*ids), n_blocks - 1)
    in_spec = pl.BlockSpec((None, rows // n_blocks, cols), lambda *ids: (layer, block(*ids), 0))
    out_spec = pl.BlockSpec((rows // n_blocks, cols), lambda *ids: (block(*ids), 0))
    return in_spec, out_spec, jax.ShapeDtypeStruct((rows, cols), BF16)


def _convert_blocks(in_refs, out_refs):
    for src, dst in zip(in_refs, out_refs):
        dst[...] = src[...].astype(BF16)


def _ffn_kernel(x_ref, g_ref, wg_ref, wu_ref, wd_ref, *refs):
    n_conv = len(refs) // 2
    o_ref = refs[n_conv]
    _convert_blocks(refs[:n_conv], refs[n_conv + 1:])
    for r in range(x_ref.shape[0] // FFN_CHUNK):
        rows = slice(r * FFN_CHUNK, (r + 1) * FFN_CHUNK)
        o_ref[rows, :] = _swiglu_residual(x_ref[rows, :], g_ref[...], wg_ref, wu_ref, wd_ref)


def _ffn_call(x2, g, wg, wu, wd, layer, later_weights, later_layer):
    t = x2.shape[0]
    steps = t // FFN1_ROWS
    conv = [_convert_plan(w, later_layer, steps, lambda i: i) for w in later_weights]
    outs = pl.pallas_call(
        _ffn_kernel,
        grid=(steps,),
        in_specs=[
            pl.BlockSpec((FFN1_ROWS, D_MODEL), lambda i: (i, 0)),
            _layer_resident(g, layer),
            _resident(wg.shape), _resident(wu.shape), _resident(wd.shape),
        ] + [c[0] for c in conv],
        out_specs=(pl.BlockSpec((FFN1_ROWS, D_MODEL), lambda i: (i, 0)),) + tuple(c[1] for c in conv),
        out_shape=(jax.ShapeDtypeStruct(x2.shape, F32),) + tuple(c[2] for c in conv),
        compiler_params=pltpu.CompilerParams(
            dimension_semantics=("arbitrary",), vmem_limit_bytes=VMEM_LIMIT),
        name="ffn1",
    )(x2, g, wg, wu, wd, *later_weights)
    return outs[0], outs[1:]


def _split_hi_lo(x):
    hi = x.astype(BF16)
    lo = (x - hi.astype(F32)).astype(BF16)
    return hi, lo


_AQ0 = 0
_AK0 = _AQ0 + MOBA_HEADS * HEAD_DIM
_AV0 = _AK0 + MOBA_HEADS * HEAD_DIM
_BQ0 = _AV0 + MOBA_HEADS * HEAD_DIM
_BK0 = _BQ0 + DIFF_HEADS * HEAD_DIM
_BV0 = _BK0 + DIFF_HEADS * HEAD_DIM
_CQ0 = _BV0 + DIFF_HEADS * HEAD_DIM
_CK0 = _CQ0 + SWA_Q_HEADS * HEAD_DIM
_CV0 = _CK0 + SWA_KV_HEADS * HEAD_DIM
PROJ_ROWS = 4 * Q_TILE


def _proj_kernel(x_ref, g_ref, wt_ref, augk_ref, augq_ref,
                 qa_ref, qb_ref, qc_ref, ka_ref, kb_ref, kc_ref, va_ref, vb_ref, vc_ref,
                 kmean_ref):
    i = pl.program_id(1)
    t = x_ref.shape[1]
    blocks_per_step = t // MOBA_BLOCK

    @pl.when(i == 0)
    def _():
        kmean_ref[...] = jnp.zeros_like(kmean_ref)

    h = _rmsnorm(x_ref[0], g_ref[...]).astype(BF16)
    p = lax.dot_general(wt_ref[...], h, (((1,), (1,)), ((), ())),
                        preferred_element_type=F32)
    augk = augk_ref[...]
    lane = lax.broadcasted_iota(jnp.int32, (t, OPERAND_WIDTH), 1)

    def key_pair(r0):
        kt = p[r0:r0 + 2 * HEAD_DIM].T
        even = jnp.where(lane < HEAD_DIM, kt, augk).astype(BF16)
        odd = jnp.where(lane < HEAD_DIM, pltpu.roll(kt, HEAD_DIM, 1), augk).astype(BF16)
        return kt, (even, odd)

    def store_tiles(ref, hh, slab, rows, axis):
        for n in range(slab.shape[axis] // rows):
            piece = slab[n * rows:(n + 1) * rows] if axis == 0 else slab[:, n * rows:(n + 1) * rows]
            ref[0, hh, n] = piece

    for pair in range(MOBA_HEADS // 2):
        kt, slabs = key_pair(_AK0 + pair * 2 * HEAD_DIM)
        for blk in range(blocks_per_step):
            kmean_ref[pair, pl.ds(blocks_per_step * i + blk, 1), :] = jnp.mean(
                kt[blk * MOBA_BLOCK:(blk + 1) * MOBA_BLOCK], axis=0, keepdims=True)
        for par in range(2):
            store_tiles(ka_ref, 2 * pair + par, slabs[par], Q_TILE, 0)
    for pair in range(DIFF_HEADS // 2):
        _, slabs = key_pair(_BK0 + pair * 2 * HEAD_DIM)
        for par in range(2):
            store_tiles(kb_ref, 2 * pair + par, slabs[par], Q_TILE, 0)
    for pair in range(SWA_KV_HEADS // 2):
        _, slabs = key_pair(_CK0 + pair * 2 * HEAD_DIM)
        for par in range(2):
            store_tiles(kc_ref, 2 * pair + par, slabs[par], SWA_BLOCK, 0)

    ones_rows = jnp.where(
        lax.broadcasted_iota(jnp.int32, (VALUE_ROWS - HEAD_DIM, t), 0) == 0, 1.0, 0.0)

    def v_slab(r0):
        return jnp.concatenate([p[r0:r0 + HEAD_DIM], ones_rows], axis=0).astype(BF16)

    for hh in range(MOBA_HEADS):
        store_tiles(va_ref, hh, v_slab(_AV0 + hh * HEAD_DIM), Q_TILE, 1)
    for hh in range(DIFF_HEADS):
        store_tiles(vb_ref, hh, v_slab(_BV0 + hh * HEAD_DIM), Q_TILE, 1)
    for hh in range(SWA_KV_HEADS):
        store_tiles(vc_ref, hh, v_slab(_CV0 + hh * HEAD_DIM), SWA_BLOCK, 1)

    n_blocks = kmean_ref.shape[1]
    jrow = lax.broadcasted_iota(jnp.int32, (n_blocks, t), 0)
    qcol = lax.broadcasted_iota(jnp.int32, (n_blocks, t), 1)
    qblk = blocks_per_step * i + qcol // MOBA_BLOCK
    zeros_rows = jnp.zeros((HEAD_DIM, t), F32)
    moba_scale = HEAD_DIM ** -0.5 * LOG2E
    for hh in range(MOBA_HEADS):
        qf = p[_AQ0 + hh * HEAD_DIM:_AQ0 + (hh + 1) * HEAD_DIM]
        q_pad = jnp.concatenate([qf, zeros_rows] if hh % 2 == 0 else [zeros_rows, qf], axis=0)
        km_hi, km_lo = _split_hi_lo(kmean_ref[hh // 2])
        q_hi, q_lo = _split_hi_lo(q_pad)
        gate = (jnp.dot(km_hi, q_hi, preferred_element_type=F32)
                + jnp.dot(km_hi, q_lo, preferred_element_type=F32)
                + jnp.dot(km_lo, q_hi, preferred_element_type=F32))
        rank = jnp.zeros((n_blocks, t), F32)
        for m in range(n_blocks):
            gm = gate[m:m + 1, :]
            beats = (gm > gate) | ((gm == gate) & (m < jrow))
            rank = rank + jnp.where(beats & (m < qblk), 1.0, 0.0)
        chosen = ((rank < MOBA_TOPK) & (jrow < qblk)) | (jrow == qblk)
        flags = jnp.where(chosen, 0.0, NEG)
        aug = augq_ref[hh] + jnp.concatenate(
            [flags, jnp.zeros((HEAD_DIM - n_blocks, t), F32)], axis=0)
        qa_ref[0, hh, :HEAD_DIM, :] = (qf * moba_scale).astype(BF16)
        qa_ref[0, hh, HEAD_DIM:, :] = aug.astype(BF16)
    diff_scale = DIFF_QK_DIM ** -0.5 * LOG2E
    for hh in range(DIFF_HEADS):
        qf = p[_BQ0 + hh * HEAD_DIM:_BQ0 + (hh + 1) * HEAD_DIM]
        qb_ref[0, hh, :HEAD_DIM, :] = (qf * diff_scale).astype(BF16)
        qb_ref[0, hh, HEAD_DIM:, :] = augq_ref[MOBA_HEADS + hh].astype(BF16)
    swa_scale = HEAD_DIM ** -0.5 * LOG2E
    for hh in range(SWA_Q_HEADS):
        qf = p[_CQ0 + hh * HEAD_DIM:_CQ0 + (hh + 1) * HEAD_DIM]
        qc_ref[0, hh, :HEAD_DIM, :] = (qf * swa_scale).astype(BF16)
        qc_ref[0, hh, HEAD_DIM:, :] = augq_ref[MOBA_HEADS + DIFF_HEADS + hh].astype(BF16)


def _proj_call(x, g, wt, augk, augq, layer):
    b, s, _ = x.shape
    nt = s // Q_TILE
    nsb = s // SWA_BLOCK
    rows = PROJ_ROWS
    tiles = rows // Q_TILE
    sblocks = rows // SWA_BLOCK
    out_shape = (
        jax.ShapeDtypeStruct((b, MOBA_HEADS, OPERAND_WIDTH, s), BF16),
        jax.ShapeDtypeStruct((b, DIFF_HEADS, OPERAND_WIDTH, s), BF16),
        jax.ShapeDtypeStruct((b, SWA_Q_HEADS, OPERAND_WIDTH, s), BF16),
        jax.ShapeDtypeStruct((b, MOBA_HEADS, nt, Q_TILE, OPERAND_WIDTH), BF16),
        jax.ShapeDtypeStruct((b, DIFF_HEADS, nt, Q_TILE, OPERAND_WIDTH), BF16),
        jax.ShapeDtypeStruct((b, SWA_KV_HEADS, nsb, SWA_BLOCK, OPERAND_WIDTH), BF16),
        jax.ShapeDtypeStruct((b, MOBA_HEADS, nt, VALUE_ROWS, Q_TILE), BF16),
        jax.ShapeDtypeStruct((b, DIFF_HEADS, nt, VALUE_ROWS, Q_TILE), BF16),
        jax.ShapeDtypeStruct((b, SWA_KV_HEADS, nsb, VALUE_ROWS, SWA_BLOCK), BF16),
    )
    out_specs = (
        pl.BlockSpec((1, MOBA_HEADS, OPERAND_WIDTH, rows), lambda bi, i: (bi, 0, 0, i)),
        pl.BlockSpec((1, DIFF_HEADS, OPERAND_WIDTH, rows), lambda bi, i: (bi, 0, 0, i)),
        pl.BlockSpec((1, SWA_Q_HEADS, OPERAND_WIDTH, rows), lambda bi, i: (bi, 0, 0, i)),
        pl.BlockSpec((1, MOBA_HEADS, tiles, Q_TILE, OPERAND_WIDTH), lambda bi, i: (bi, 0, i, 0, 0)),
        pl.BlockSpec((1, DIFF_HEADS, tiles, Q_TILE, OPERAND_WIDTH), lambda bi, i: (bi, 0, i, 0, 0)),
        pl.BlockSpec((1, SWA_KV_HEADS, sblocks, SWA_BLOCK, OPERAND_WIDTH),
                     lambda bi, i: (bi, 0, i, 0, 0)),
        pl.BlockSpec((1, MOBA_HEADS, tiles, VALUE_ROWS, Q_TILE), lambda bi, i: (bi, 0, i, 0, 0)),
        pl.BlockSpec((1, DIFF_HEADS, tiles, VALUE_ROWS, Q_TILE), lambda bi, i: (bi, 0, i, 0, 0)),
        pl.BlockSpec((1, SWA_KV_HEADS, sblocks, VALUE_ROWS, SWA_BLOCK),
                     lambda bi, i: (bi, 0, i, 0, 0)),
    )
    return pl.pallas_call(
        _proj_kernel,
        grid=(b, s // rows),
        in_specs=[
            pl.BlockSpec((1, rows, D_MODEL), lambda bi, i: (bi, i, 0)),
            _layer_resident(g, layer),
            _layer_resident(wt, layer),
            pl.BlockSpec((rows, OPERAND_WIDTH), lambda bi, i: (i, 0)),
            pl.BlockSpec((N_Q_HEADS, HEAD_DIM, rows), lambda bi, i: (0, 0, i)),
        ],
        out_specs=out_specs,
        out_shape=out_shape,
        scratch_shapes=[pltpu.VMEM((MOBA_HEADS // 2, s // MOBA_BLOCK, OPERAND_WIDTH), F32)],
        compiler_params=pltpu.CompilerParams(
            dimension_semantics=("arbitrary", "arbitrary"), vmem_limit_bytes=VMEM_LIMIT),
        name="proj",
    )(x, g, wt, augk, augq)


def _take_cols(x, cols):
    return x if cols is None else jnp.concatenate([x[:, lo:hi] for lo, hi in cols], axis=1)


def _put_cols(full, part, cols):
    if cols is None:
        return part
    pieces, pos, off = [], 0, 0
    for lo, hi in cols:
        if lo > pos:
            pieces.append(full[:, pos:lo])
        pieces.append(part[:, off:off + hi - lo])
        off, pos = off + hi - lo, hi
    if pos < full.shape[1]:
        pieces.append(full[:, pos:])
    return jnp.concatenate(pieces, axis=1)


def _flash_unrolled(n_chains, q_of, tiles):
    def scores_of(t):
        keys, _, _, cols = tiles[t]
        return [jnp.dot(keys(c), _take_cols(q_of(c), cols), preferred_element_type=F32)
                for c in range(n_chains)]

    m = [None] * n_chains
    acc = [None] * n_chains
    upcoming = scores_of(0)
    for t, (_, values, mask, cols) in enumerate(tiles):
        scores = upcoming
        if t + 1 < len(tiles):
            upcoming = scores_of(t + 1)
        probs, alphas = [], []
        for c in range(n_chains):
            s = scores[c] if mask is None else jnp.where(mask, scores[c], NEG)
            m_new = jnp.max(s, axis=0, keepdims=True)
            if t > 0:
                m_old = _take_cols(m[c], cols)
                m_new = jnp.maximum(m_old, m_new)
                alphas.append(jnp.exp2(m_old - m_new))
                m[c] = _put_cols(m[c], m_new, cols)
            else:
                m[c] = m_new
            probs.append(jnp.exp2(s - m_new).astype(BF16))
        for c in range(n_chains):
            pv = jnp.dot(values(c), probs[c], preferred_element_type=F32)
            if t > 0:
                pv = _put_cols(acc[c], alphas[c] * _take_cols(acc[c], cols) + pv, cols)
            acc[c] = pv
    return acc


def _query_tile_keys(k_ref, v_ref, i_static, first_mask, second_mask, second_cols):
    def half(j, mask, cols):
        return (lambda c: k_ref[0, c, j], lambda c: v_ref[0, c, j], mask, cols)

    def past(j):
        return (lambda c: jnp.concatenate([k_ref[0, c, 2 * j], k_ref[0, c, 2 * j + 1]], axis=0),
                lambda c: jnp.concatenate([v_ref[0, c, 2 * j], v_ref[0, c, 2 * j + 1]], axis=1),
                None, None)

    return ([half(2 * i_static, first_mask, None), half(2 * i_static + 1, second_mask, second_cols)]
            + [past(j) for j in range(i_static)])


def _normalized(acc):
    return acc[:HEAD_DIM] / acc[HEAD_DIM:HEAD_DIM + 1]


def _attn_params():
    return pltpu.CompilerParams(
        dimension_semantics=("arbitrary", "arbitrary"), vmem_limit_bytes=VMEM_LIMIT)


ATTN_TILE = 2 * Q_TILE


def _causal(rows, cols, period):
    kr = lax.broadcasted_iota(jnp.int32, (rows, cols), 0)
    qc = lax.broadcasted_iota(jnp.int32, (rows, cols), 1)
    for k in range(cols // period - 1, 0, -1):
        qc = jnp.where(qc >= k * period, qc - period, qc)
    return kr <= qc


def _moba_kernel(q_ref, k_ref, v_ref, o_ref):
    nh = q_ref.shape[1]
    for i_static in range(k_ref.shape[2] * Q_TILE // ATTN_TILE):
        @pl.when(pl.program_id(1) == i_static)
        def _(i_static=i_static):
            tiles = _query_tile_keys(
                k_ref, v_ref, i_static, _causal(Q_TILE, ATTN_TILE, ATTN_TILE),
                _causal(Q_TILE, Q_TILE, Q_TILE), [(Q_TILE, ATTN_TILE)])
            acc = _flash_unrolled(nh, lambda c: q_ref[0, c], tiles)
            for h in range(nh):
                o_ref[0, h * HEAD_DIM:(h + 1) * HEAD_DIM, :] = (
                    _normalized(acc[h]).astype(o_ref.dtype))


def _moba_part(qa, ka, va):
    b, nh, _, s = qa.shape
    nt = s // Q_TILE
    in_specs = [
        pl.BlockSpec((1, nh, OPERAND_WIDTH, ATTN_TILE), lambda bi, i: (bi, 0, 0, i)),
        pl.BlockSpec((1, nh, nt, Q_TILE, OPERAND_WIDTH), lambda bi, i: (bi, 0, 0, 0, 0)),
        pl.BlockSpec((1, nh, nt, VALUE_ROWS, Q_TILE), lambda bi, i: (bi, 0, 0, 0, 0)),
    ]
    out_spec = pl.BlockSpec((1, nh * HEAD_DIM, ATTN_TILE), lambda bi, i: (bi, 0, i))
    return (qa, ka, va), in_specs, out_spec, jax.ShapeDtypeStruct((b, nh * HEAD_DIM, s), BF16), []


def _diff_kernel(q_ref, k_ref, v_ref, lq1_ref, lk1_ref, lq2_ref, lk2_ref, linit_ref, g_ref, o_ref,
                 qm_ref):
    nh = q_ref.shape[1]
    row = lax.broadcasted_iota(jnp.int32, q_ref.shape[2:], 0)
    for h in range(nh):
        q = q_ref[0, h]
        zero = jnp.zeros_like(q)
        qm_ref[h, :, :ATTN_TILE] = jnp.where(
            (row < DIFF_QK_DIM) | (row >= 2 * DIFF_QK_DIM), q, zero)
        qm_ref[h, :, ATTN_TILE:] = jnp.where(row >= DIFF_QK_DIM, q, zero)
    lam_init = linit_ref[...]
    lam = (jnp.exp(jnp.sum(lq1_ref[...] * lk1_ref[...], axis=-1, keepdims=True))
           - jnp.exp(jnp.sum(lq2_ref[...] * lk2_ref[...], axis=-1, keepdims=True))
           + lam_init)
    for i_static in range(k_ref.shape[2] * Q_TILE // ATTN_TILE):
        @pl.when(pl.program_id(1) == i_static)
        def _(i_static=i_static):
            tiles = _query_tile_keys(
                k_ref, v_ref, i_static, _causal(Q_TILE, 2 * ATTN_TILE, ATTN_TILE),
                _causal(Q_TILE, ATTN_TILE, Q_TILE),
                [(Q_TILE, ATTN_TILE), (ATTN_TILE + Q_TILE, 2 * ATTN_TILE)])
            acc = _flash_unrolled(nh, lambda c: qm_ref[c], tiles)
            for h in range(nh):
                w = _normalized(acc[h])
                o = w[:, :ATTN_TILE] - lam * w[:, ATTN_TILE:]
                ms = jnp.mean(o * o, axis=0, keepdims=True)
                y = (o * lax.rsqrt(ms + RMS_EPS)) * g_ref[...]
                o_ref[0, h * HEAD_DIM:(h + 1) * HEAD_DIM, :] = (
                    (y * (1.0 - lam_init)).astype(o_ref.dtype))


def _diff_part(qb, kb, vb, lam_q1, lam_k1, lam_q2, lam_k2, lam_init, g_col, layer):
    b, nh, _, s = qb.shape
    nt = s // Q_TILE
    per_layer = lambda arr: pl.BlockSpec(
        (None,) + arr.shape[1:], lambda bi, i: (layer,) + (0,) * (arr.ndim - 1))
    in_specs = [
        pl.BlockSpec((1, nh, OPERAND_WIDTH, ATTN_TILE), lambda bi, i: (bi, 0, 0, i)),
        pl.BlockSpec((1, nh, nt, Q_TILE, OPERAND_WIDTH), lambda bi, i: (bi, 0, 0, 0, 0)),
        pl.BlockSpec((1, nh, nt, VALUE_ROWS, Q_TILE), lambda bi, i: (bi, 0, 0, 0, 0)),
        per_layer(lam_q1), per_layer(lam_k1), per_layer(lam_q2), per_layer(lam_k2),
        per_layer(lam_init), per_layer(g_col),
    ]
    out_spec = pl.BlockSpec((1, nh * HEAD_DIM, ATTN_TILE), lambda bi, i: (bi, 0, i))
    scratch = [pltpu.VMEM((nh, OPERAND_WIDTH, 2 * ATTN_TILE), BF16)]
    return ((qb, kb, vb, lam_q1, lam_k1, lam_q2, lam_k2, lam_init, g_col), in_specs, out_spec,
            jax.ShapeDtypeStruct((b, nh * HEAD_DIM, s), BF16), scratch)


SWA_BLOCKS_PER_STEP = 4


def _swa_kernel(q_ref, k_ref, v_ref, sink_ref, o_ref):
    i = pl.program_id(1)
    kr = lax.broadcasted_iota(jnp.int32, (2 * SWA_BLOCK, SWA_BLOCK), 0)
    qc = lax.broadcasted_iota(jnp.int32, (2 * SWA_BLOCK, SWA_BLOCK), 1)
    dist = qc + SWA_BLOCK - kr
    in_win = (dist >= 0) & (dist < SWA_WINDOW)
    chains = [(b2, kvh) for b2 in range(SWA_BLOCKS_PER_STEP) for kvh in range(SWA_KV_HEADS)]
    blocks, bands = [], []
    for b2 in range(SWA_BLOCKS_PER_STEP):
        n = SWA_BLOCKS_PER_STEP * i + b2
        blocks.append((jnp.maximum(n - 1, 0), n))
        first_row = jnp.where(n > 0, 0, SWA_BLOCK)
        band = jnp.where(in_win & (kr >= first_row), 0.0, NEG)
        bands.append(jnp.concatenate([band] * SWA_GROUP, axis=1))
    scores = []
    for b2, kvh in chains:
        prev, n = blocks[b2]
        cols = slice(b2 * SWA_BLOCK, (b2 + 1) * SWA_BLOCK)
        q = jnp.concatenate([q_ref[0, kvh * SWA_GROUP + g, :, cols]
                             for g in range(SWA_GROUP)], axis=1)
        k = jnp.concatenate([k_ref[0, kvh, prev], k_ref[0, kvh, n]], axis=0)
        scores.append(jnp.dot(k, q, preferred_element_type=F32))
    probs, sink_terms = [], []
    for (b2, kvh), s in zip(chains, scores):
        s = s + bands[b2]
        sink = sink_ref[kvh] * LOG2E
        m = jnp.maximum(jnp.max(s, axis=0, keepdims=True), sink)
        probs.append(jnp.exp2(s - m).astype(BF16))
        sink_terms.append(jnp.exp2(sink - m))
    for (b2, kvh), p, sink_term in zip(chains, probs, sink_terms):
        prev, n = blocks[b2]
        cols = slice(b2 * SWA_BLOCK, (b2 + 1) * SWA_BLOCK)
        v = jnp.concatenate([v_ref[0, kvh, prev], v_ref[0, kvh, n]], axis=1)
        acc = jnp.dot(v, p, preferred_element_type=F32)
        o = acc[:HEAD_DIM] / (acc[HEAD_DIM:HEAD_DIM + 1] + sink_term)
        for g in range(SWA_GROUP):
            r0 = (kvh * SWA_GROUP + g) * HEAD_DIM
            o_ref[0, r0:r0 + HEAD_DIM, cols] = (
                o[:, g * SWA_BLOCK:(g + 1) * SWA_BLOCK].astype(o_ref.dtype))


def _swa_part(qc, kc, vc, sink_rows, layer):
    b, nq, _, s = qc.shape
    nkv = kc.shape[1]
    nsb = s // SWA_BLOCK
    step = SWA_BLOCKS_PER_STEP * SWA_BLOCK
    assert step == ATTN_TILE
    in_specs = [
        pl.BlockSpec((1, nq, OPERAND_WIDTH, step), lambda bi, i: (bi, 0, 0, i)),
        pl.BlockSpec((1, nkv, nsb, SWA_BLOCK, OPERAND_WIDTH), lambda bi, i: (bi, 0, 0, 0, 0)),
        pl.BlockSpec((1, nkv, nsb, VALUE_ROWS, SWA_BLOCK), lambda bi, i: (bi, 0, 0, 0, 0)),
        pl.BlockSpec((None,) + sink_rows.shape[1:], lambda bi, i: (layer, 0, 0, 0)),
    ]
    out_spec = pl.BlockSpec((1, nq * HEAD_DIM, step), lambda bi, i: (bi, 0, i))
    return ((qc, kc, vc, sink_rows), in_specs, out_spec,
            jax.ShapeDtypeStruct((b, nq * HEAD_DIM, s), BF16), [])


def _attention_kernel(*refs, n_in):
    ins = [refs[sum(n_in[:k]):sum(n_in[:k + 1])] for k in range(3)]
    n_all = sum(n_in)
    oa_ref, ob_ref, oc_ref = refs[n_all:n_all + 3]
    qm_ref = refs[n_all + 3]
    _moba_kernel(*ins[0], oa_ref)
    _diff_kernel(*ins[1], ob_ref, qm_ref)
    _swa_kernel(*ins[2], oc_ref)


def _attention_call(moba_part, diff_part, swa_part):
    parts = (moba_part, diff_part, swa_part)
    operands = [a for part in parts for a in part[0]]
    b, _, _, s = moba_part[0][0].shape
    return pl.pallas_call(
        functools.partial(_attention_kernel, n_in=tuple(len(part[0]) for part in parts)),
        grid=(b, s // ATTN_TILE),
        in_specs=[spec for part in parts for spec in part[1]],
        out_specs=tuple(part[2] for part in parts),
        out_shape=tuple(part[3] for part in parts),
        scratch_shapes=[sc for part in parts for sc in part[4]],
        compiler_params=_attn_params(),
        name="attention",
    )(*operands)


def _mix_ffn_kernel(x_ref, ma_ref, mb_ref, mc_ref, woa_ref, wob_ref, woc_ref,
                    g_ref, wg_ref, wu_ref, wd_ref, gf_ref, *refs, final):
    n_conv = len(refs) // 2
    o_ref = refs[n_conv]
    _convert_blocks(refs[:n_conv], refs[n_conv + 1:])
    tdims = (((0,), (0,)), ((), ()))
    for r in range(x_ref.shape[1] // FFN_CHUNK):
        rows = slice(r * FFN_CHUNK, (r + 1) * FFN_CHUNK)
        x = x_ref[0, rows, :]
        x = x + lax.dot_general(ma_ref[0, :, rows], woa_ref[...], tdims, preferred_element_type=F32)
        x = x + lax.dot_general(mb_ref[0, :, rows], wob_ref[...], tdims, preferred_element_type=F32)
        x = x + lax.dot_general(mc_ref[0, :, rows], woc_ref[...], tdims, preferred_element_type=F32)
        x = _swiglu_residual(x, g_ref[...], wg_ref, wu_ref, wd_ref)
        if final:
            x = _rmsnorm(x, gf_ref[...])
        o_ref[0, rows, :] = x


def _mix_ffn_call(x, ma, mb, mc, wo, g, wg, wu, wd, gf, layer, final, later_weights, later_layer):
    b, s, _ = x.shape
    rows = FFN_ROWS
    steps = s // rows
    a_rows, b_rows, c_rows = ma.shape[1], mb.shape[1], mc.shape[1]
    assert a_rows == b_rows and c_rows == a_rows + b_rows
    conv = [_convert_plan(w, later_layer, b * steps, lambda bi, i: bi * steps + i)
            for w in later_weights]
    outs = pl.pallas_call(
        functools.partial(_mix_ffn_kernel, final=final),
        grid=(b, steps),
        in_specs=[
            pl.BlockSpec((1, rows, D_MODEL), lambda bi, i: (bi, i, 0)),
            pl.BlockSpec((1, a_rows, rows), lambda bi, i: (bi, 0, i)),
            pl.BlockSpec((1, b_rows, rows), lambda bi, i: (bi, 0, i)),
            pl.BlockSpec((1, c_rows, rows), lambda bi, i: (bi, 0, i)),
            _layer_resident(wo, layer, a_rows, 0),
            _layer_resident(wo, layer, b_rows, 1),
            _layer_resident(wo, layer, c_rows, 1),
            _layer_resident(g, layer),
            _resident(wg.shape), _resident(wu.shape), _resident(wd.shape),
            _resident((1, D_MODEL)),
        ] + [c[0] for c in conv],
        out_specs=(pl.BlockSpec((1, rows, D_MODEL), lambda bi, i: (bi, i, 0)),)
        + tuple(c[1] for c in conv),
        out_shape=(jax.ShapeDtypeStruct(x.shape, F32),) + tuple(c[2] for c in conv),
        compiler_params=pltpu.CompilerParams(
            dimension_semantics=("arbitrary", "arbitrary"), vmem_limit_bytes=VMEM_LIMIT),
        name="mix_ffn2_final" if final else "mix_ffn2",
    )(x, ma, mb, mc, wo, wo, wo, g, wg, wu, wd, gf, *later_weights)
    return outs[0], outs[1:]


def _split3(x):
    x1 = x.astype(BF16).astype(F32)
    r = x - x1
    x2 = r.astype(BF16).astype(F32)
    x3 = (r - x2).astype(BF16).astype(F32)
    return [x1, x2, x3]


def _bias_rows_kernel(raw_ref, o_ref):
    s = raw_ref.shape[2]
    row = lax.broadcasted_iota(jnp.int32, (16, s), 0)
    lead = AUG_BIAS_ROW - HEAD_DIM
    for hh in range(raw_ref.shape[0]):
        raw = raw_ref[hh]
        pieces = _split3(raw[0:1]) + 2 * _split3(raw[1:2])
        blk = jnp.zeros((16, s), F32)
        for r, piece in enumerate(pieces):
            blk = jnp.where(row == r, piece, blk)
        o_ref[hh] = jnp.concatenate(
            [jnp.zeros((lead, s), F32), blk, jnp.zeros((HEAD_DIM - lead - 16, s), F32)], axis=0)


def _operand_tables(s):
    slopes = _alibi_slopes()
    head_slopes = np.concatenate([
        slopes[SWA_Q_HEADS + DIFF_HEADS:],
        slopes[SWA_Q_HEADS:SWA_Q_HEADS + DIFF_HEADS],
        slopes[:SWA_Q_HEADS],
    ]).astype(np.float32)
    pos = jnp.arange(s, dtype=F32)
    sl = jnp.asarray(head_slopes * np.float32(LOG2E))[:, None]
    raw = jnp.zeros((N_Q_HEADS, 8, s), F32)
    raw = raw.at[:, 0, :].set(-sl * pos[None, :])
    raw = raw.at[:, 1, :].set(jnp.broadcast_to(sl, (N_Q_HEADS, s)))
    augq = pl.pallas_call(
        _bias_rows_kernel,
        out_shape=jax.ShapeDtypeStruct((N_Q_HEADS, HEAD_DIM, s), F32),
        name="bias_rows",
    )(raw)
    blk = jnp.arange(s, dtype=jnp.int32) // MOBA_BLOCK
    onehot = (blk[:, None] == jnp.arange(8, dtype=jnp.int32)[None, :]).astype(F32)
    ones = jnp.ones((s, 3), F32)
    base = jnp.broadcast_to((blk * MOBA_BLOCK).astype(F32)[:, None], (s, 3))
    off = jnp.broadcast_to((jnp.arange(s, dtype=jnp.int32) % MOBA_BLOCK).astype(F32)[:, None], (s, 3))
    augk = jnp.concatenate([
        jnp.zeros((s, HEAD_DIM), F32), onehot, ones, base, off,
        jnp.zeros((s, OPERAND_WIDTH - HEAD_DIM - 17), F32)], axis=1)
    return augq, augk


def kernel(x, norm_ffn1, w1_gate, w1_up, w1_down, norm_mix, w_in, lam_q1, lam_k1, lam_q2, lam_k2,
           diff_subln, sinks, w_out, norm_ffn2, w2_gate, w2_up, w2_down, final_norm):
    b, s, d = x.shape
    depth = w_in.shape[0]
    assert d == D_MODEL and s % ATTN_TILE == 0 and (b * s) % FFN1_ROWS == 0 and s % FFN_ROWS == 0
    augq, augk = _operand_tables(s)
    as_rows = lambda p: p.reshape(depth, 1, -1)
    g1, gm, g2 = as_rows(norm_ffn1), as_rows(norm_mix), as_rows(norm_ffn2)
    ffn1_f32, ffn2_f32 = (w1_gate, w1_up, w1_down), (w2_gate, w2_up, w2_down)
    wt, wo = jnp.swapaxes(w_in.astype(BF16), 1, 2), w_out.astype(BF16)
    lq1, lk1, lq2, lk2 = as_rows(lam_q1), as_rows(lam_k1), as_rows(lam_q2), as_rows(lam_k2)
    lam_init = jnp.asarray([0.8 - 0.6 * math.exp(-0.3 * l) for l in range(depth)],
                           F32).reshape(depth, 1, 1)
    subln = diff_subln.reshape(depth, -1, 1)
    sink_rows = jnp.repeat(sinks.reshape(depth, SWA_KV_HEADS, 1, SWA_GROUP), SWA_BLOCK, axis=3)
    gf = final_norm.reshape(1, d)
    ffn1_w = tuple(w[0].astype(BF16) for w in ffn1_f32)
    for l in range(depth):
        x2, ffn2_w = _ffn_call(x.reshape(b * s, d), g1, *ffn1_w, l, ffn2_f32, l)
        x = x2.reshape(b, s, d)
        qa, qb, qc, ka, kb, kc, va, vb, vc = _proj_call(x, gm, wt, augk, augq, l)
        ma, mb, mc = _attention_call(
            _moba_part(qa, ka, va),
            _diff_part(qb, kb, vb, lq1, lk1, lq2, lk2, lam_init, subln, l),
            _swa_part(qc, kc, vc, sink_rows, l))
        last = l == depth - 1
        x, ffn1_w = _mix_ffn_call(x, ma, mb, mc, wo, g2, *ffn2_w, gf, l, last,
                                  () if last else ffn1_f32, l + 1)
    return x
```

```python
import functools
import math

import numpy as np
import jax
import jax.numpy as jnp
from jax import lax
from jax.experimental import pallas as pl
from jax.experimental.pallas import tpu as pltpu

F32 = jnp.float32
BF16 = jnp.bfloat16

D_MODEL = 1024
D_FF = 2816
HEAD_DIM = 64
MOBA_HEADS = 4
MOBA_BLOCK = 256
MOBA_TOPK = 3
DIFF_HEADS = 4
DIFF_QK_DIM = 32
SWA_Q_HEADS = 8
SWA_KV_HEADS = 2
SWA_GROUP = SWA_Q_HEADS // SWA_KV_HEADS
SWA_WINDOW = 128
SWA_BLOCK = 128
N_ALIBI = MOBA_HEADS + DIFF_HEADS + SWA_Q_HEADS
RMS_EPS = 1e-6
NEG = -1e30

N_Q_HEADS = MOBA_HEADS + DIFF_HEADS + SWA_Q_HEADS
OPERAND_WIDTH = 128
VALUE_ROWS = HEAD_DIM + 16
LOG2E = math.log2(math.e)
AUG_FLAG_ROW = HEAD_DIM
AUG_BIAS_ROW = HEAD_DIM + 8
Q_TILE = 256
FFN_ROWS = 512
FFN1_ROWS = 1024
FFN_CHUNK = 256
VMEM_LIMIT = 56 * 1024 * 1024


def _alibi_slopes():
    n = N_ALIBI
    return 2.0 ** (-8.0 * (np.arange(n, dtype=np.float32) + 1.0) / n)


def _rmsnorm(x, g):
    ms = jnp.mean(x * x, axis=-1, keepdims=True)
    return (x * lax.rsqrt(ms + RMS_EPS)) * g


def _swiglu_residual(x, g, wg_ref, wu_ref, wd_ref):
    h = _rmsnorm(x, g).astype(BF16)
    gate = jnp.dot(h, wg_ref[...], preferred_element_type=F32)
    up = jnp.dot(h, wu_ref[...], preferred_element_type=F32)
    act = (gate * jax.nn.sigmoid(gate) * up).astype(BF16)
    return x + 0.5 * jnp.dot(act, wd_ref[...], preferred_element_type=F32)


def _resident(shape):
    zeros = (0,) * len(shape)
    return pl.BlockSpec(shape, lambda *_: zeros, pipeline_mode=pl.Buffered(1))


def _layer_resident(stacked, layer, rows=None, row_block=0):
    shape = stacked.shape[1:] if rows is None else (rows,) + stacked.shape[2:]
    index = (layer, row_block) + (0,) * (len(shape) - 1)
    return pl.BlockSpec((None,) + shape, lambda *_: index, pipeline_mode=pl.Buffered(1))


def _convert_plan(w, layer, n_steps, step_index):
    rows, cols = w.shape[1:]
    n_blocks = max(d for d in range(1, n_steps + 1) if rows % d == 0 and (rows // d) % 16 == 0)
    block = lambda *ids: jnp.minimum(step_index(*ids), n_blocks - 1)
    in_spec = pl.BlockSpec((None, rows // n_blocks, cols), lambda *ids: (layer, block(*ids), 0))
    out_spec = pl.BlockSpec((rows // n_blocks, cols), lambda *ids: (block(*ids), 0))
    return in_spec, out_spec, jax.ShapeDtypeStruct((rows, cols), BF16)


def _convert_blocks(in_refs, out_refs):
    for src, dst in zip(in_refs, out_refs):
        dst[...] = src[...].astype(BF16)


def _ffn_kernel(x_ref, g_ref, wg_ref, wu_ref, wd_ref, *refs):
    n_conv = len(refs) // 2
    o_ref = refs[n_conv]
    _convert_blocks(refs[:n_conv], refs[n_conv + 1:])
    for r in range(x_ref.shape[0] // FFN_CHUNK):
        rows = slice(r * FFN_CHUNK, (r + 1) * FFN_CHUNK)
        o_ref[rows, :] = _swiglu_residual(x_ref[rows, :], g_ref[...], wg_ref, wu_ref, wd_ref)


def _ffn_call(x2, g, wg, wu, wd, layer, later_weights, later_layer):
    t = x2.shape[0]
    steps = t // FFN1_ROWS
    conv = [_convert_plan(w, later_layer, steps, lambda i: i) for w in later_weights]
    outs = pl.pallas_call(
        _ffn_kernel,
        grid=(steps,),
        in_specs=[
            pl.BlockSpec((FFN1_ROWS, D_MODEL), lambda i: (i, 0)),
            _layer_resident(g, layer),
            _resident(wg.shape), _resident(wu.shape), _resident(wd.shape),
        ] + [c[0] for c in conv],
        out_specs=(pl.BlockSpec((FFN1_ROWS, D_MODEL), lambda i: (i, 0)),) + tuple(c[1] for c in conv),
        out_shape=(jax.ShapeDtypeStruct(x2.shape, F32),) + tuple(c[2] for c in conv),
        compiler_params=pltpu.CompilerParams(
            dimension_semantics=("arbitrary",), vmem_limit_bytes=VMEM_LIMIT),
        name="ffn1",
    )(x2, g, wg, wu, wd, *later_weights)
    return outs[0], outs[1:]


def _split_hi_lo(x):
    hi = x.astype(BF16)
    lo = (x - hi.astype(F32)).astype(BF16)
    return hi, lo


_AQ0 = 0
_AK0 = _AQ0 + MOBA_HEADS * HEAD_DIM
_AV0 = _AK0 + MOBA_HEADS * HEAD_DIM
_BQ0 = _AV0 + MOBA_HEADS * HEAD_DIM
_BK0 = _BQ0 + DIFF_HEADS * HEAD_DIM
_BV0 = _BK0 + DIFF_HEADS * HEAD_DIM
_CQ0 = _BV0 + DIFF_HEADS * HEAD_DIM
_CK0 = _CQ0 + SWA_Q_HEADS * HEAD_DIM
_CV0 = _CK0 + SWA_KV_HEADS * HEAD_DIM
PROJ_ROWS = 2 * Q_TILE


def _proj_kernel(x_ref, g_ref, wt_ref, augk_ref, augq_ref,
                 qa_ref, qb_ref, qc_ref, ka_ref, kb_ref, kc_ref, va_ref, vb_ref, vc_ref,
                 kmean_ref):
    i = pl.program_id(1)
    t = x_ref.shape[1]
    blocks_per_step = t // MOBA_BLOCK

    @pl.when(i == 0)
    def _():
        kmean_ref[...] = jnp.zeros_like(kmean_ref)

    h = _rmsnorm(x_ref[0], g_ref[...]).astype(BF16)
    p = lax.dot_general(wt_ref[...], h, (((1,), (1,)), ((), ())),
                        preferred_element_type=F32)
    augk = augk_ref[...]
    lane = lax.broadcasted_iota(jnp.int32, (t, OPERAND_WIDTH), 1)

    def key_pair(r0):
        kt = p[r0:r0 + 2 * HEAD_DIM].T
        even = jnp.where(lane < HEAD_DIM, kt, augk).astype(BF16)
        odd = jnp.where(lane < HEAD_DIM, pltpu.roll(kt, HEAD_DIM, 1), augk).astype(BF16)
        return kt, (even, odd)

    def store_tiles(ref, hh, slab, rows, axis):
        for n in range(slab.shape[axis] // rows):
            piece = slab[n * rows:(n + 1) * rows] if axis == 0 else slab[:, n * rows:(n + 1) * rows]
            ref[0, hh, n] = piece

    for pair in range(MOBA_HEADS // 2):
        kt, slabs = key_pair(_AK0 + pair * 2 * HEAD_DIM)
        for blk in range(blocks_per_step):
            kmean_ref[pair, pl.ds(blocks_per_step * i + blk, 1), :] = jnp.mean(
                kt[blk * MOBA_BLOCK:(blk + 1) * MOBA_BLOCK], axis=0, keepdims=True)
        for par in range(2):
            store_tiles(ka_ref, 2 * pair + par, slabs[par], Q_TILE, 0)
    for pair in range(DIFF_HEADS // 2):
        _, slabs = key_pair(_BK0 + pair * 2 * HEAD_DIM)
        for par in range(2):
            store_tiles(kb_ref, 2 * pair + par, slabs[par], Q_TILE, 0)
    for pair in range(SWA_KV_HEADS // 2):
        _, slabs = key_pair(_CK0 + pair * 2 * HEAD_DIM)
        for par in range(2):
            store_tiles(kc_ref, 2 * pair + par, slabs[par], SWA_BLOCK, 0)

    ones_rows = jnp.where(
        lax.broadcasted_iota(jnp.int32, (VALUE_ROWS - HEAD_DIM, t), 0) == 0, 1.0, 0.0)

    def v_slab(r0):
        return jnp.concatenate([p[r0:r0 + HEAD_DIM], ones_rows], axis=0).astype(BF16)

    for hh in range(MOBA_HEADS):
        store_tiles(va_ref, hh, v_slab(_AV0 + hh * HEAD_DIM), Q_TILE, 1)
    for hh in range(DIFF_HEADS):
        store_tiles(vb_ref, hh, v_slab(_BV0 + hh * HEAD_DIM), Q_TILE, 1)
    for hh in range(SWA_KV_HEADS):
        store_tiles(vc_ref, hh, v_slab(_CV0 + hh * HEAD_DIM), SWA_BLOCK, 1)

    n_blocks = kmean_ref.shape[1]
    jrow = lax.broadcasted_iota(jnp.int32, (n_blocks, t), 0)
    qcol = lax.broadcasted_iota(jnp.int32, (n_blocks, t), 1)
    qblk = blocks_per_step * i + qcol // MOBA_BLOCK
    zeros_rows = jnp.zeros((HEAD_DIM, t), F32)
    moba_scale = HEAD_DIM ** -0.5 * LOG2E
    for hh in range(MOBA_HEADS):
        qf = p[_AQ0 + hh * HEAD_DIM:_AQ0 + (hh + 1) * HEAD_DIM]
        q_pad = jnp.concatenate([qf, zeros_rows] if hh % 2 == 0 else [zeros_rows, qf], axis=0)
        km_hi, km_lo = _split_hi_lo(kmean_ref[hh // 2])
        q_hi, q_lo = _split_hi_lo(q_pad)
        gate = (jnp.dot(km_hi, q_hi, preferred_element_type=F32)
                + jnp.dot(km_hi, q_lo, preferred_element_type=F32)
                + jnp.dot(km_lo, q_hi, preferred_element_type=F32))
        rank = jnp.zeros((n_blocks, t), F32)
        for m in range(n_blocks):
            gm = gate[m:m + 1, :]
            beats = (gm > gate) | ((gm == gate) & (m < jrow))
            rank = rank + jnp.where(beats & (m < qblk), 1.0, 0.0)
        chosen = ((rank < MOBA_TOPK) & (jrow < qblk)) | (jrow == qblk)
        flags = jnp.where(chosen, 0.0, NEG)
        aug = augq_ref[hh] + jnp.concatenate(
            [flags, jnp.zeros((HEAD_DIM - n_blocks, t), F32)], axis=0)
        qa_ref[0, hh, :HEAD_DIM, :] = (qf * moba_scale).astype(BF16)
        qa_ref[0, hh, HEAD_DIM:, :] = aug.astype(BF16)
    diff_scale = DIFF_QK_DIM ** -0.5 * LOG2E
    for hh in range(DIFF_HEADS):
        qf = p[_BQ0 + hh * HEAD_DIM:_BQ0 + (hh + 1) * HEAD_DIM]
        qb_ref[0, hh, :HEAD_DIM, :] = (qf * diff_scale).astype(BF16)
        qb_ref[0, hh, HEAD_DIM:, :] = augq_ref[MOBA_HEADS + hh].astype(BF16)
    swa_scale = HEAD_DIM ** -0.5 * LOG2E
    for hh in range(SWA_Q_HEADS):
        qf = p[_CQ0 + hh * HEAD_DIM:_CQ0 + (hh + 1) * HEAD_DIM]
        qc_ref[0, hh, :HEAD_DIM, :] = (qf * swa_scale).astype(BF16)
        qc_ref[0, hh, HEAD_DIM:, :] = augq_ref[MOBA_HEADS + DIFF_HEADS + hh].astype(BF16)


def _proj_call(x, g, wt, augk, augq, layer):
    b, s, _ = x.shape
    nt = s // Q_TILE
    nsb = s // SWA_BLOCK
    rows = PROJ_ROWS
    tiles = rows // Q_TILE
    sblocks = rows // SWA_BLOCK
    out_shape = (
        jax.ShapeDtypeStruct((b, MOBA_HEADS, OPERAND_WIDTH, s), BF16),
        jax.ShapeDtypeStruct((b, DIFF_HEADS, OPERAND_WIDTH, s), BF16),
        jax.ShapeDtypeStruct((b, SWA_Q_HEADS, OPERAND_WIDTH, s), BF16),
        jax.ShapeDtypeStruct((b, MOBA_HEADS, nt, Q_TILE, OPERAND_WIDTH), BF16),
        jax.ShapeDtypeStruct((b, DIFF_HEADS, nt, Q_TILE, OPERAND_WIDTH), BF16),
        jax.ShapeDtypeStruct((b, SWA_KV_HEADS, nsb, SWA_BLOCK, OPERAND_WIDTH), BF16),
        jax.ShapeDtypeStruct((b, MOBA_HEADS, nt, VALUE_ROWS, Q_TILE), BF16),
        jax.ShapeDtypeStruct((b, DIFF_HEADS, nt, VALUE_ROWS, Q_TILE), BF16),
        jax.ShapeDtypeStruct((b, SWA_KV_HEADS, nsb, VALUE_ROWS, SWA_BLOCK), BF16),
    )
    out_specs = (
        pl.BlockSpec((1, MOBA_HEADS, OPERAND_WIDTH, rows), lambda bi, i: (bi, 0, 0, i)),
        pl.BlockSpec((1, DIFF_HEADS, OPERAND_WIDTH, rows), lambda bi, i: (bi, 0, 0, i)),
        pl.BlockSpec((1, SWA_Q_HEADS, OPERAND_WIDTH, rows), lambda bi, i: (bi, 0, 0, i)),
        pl.BlockSpec((1, MOBA_HEADS, tiles, Q_TILE, OPERAND_WIDTH), lambda bi, i: (bi, 0, i, 0, 0)),
        pl.BlockSpec((1, DIFF_HEADS, tiles, Q_TILE, OPERAND_WIDTH), lambda bi, i: (bi, 0, i, 0, 0)),
        pl.BlockSpec((1, SWA_KV_HEADS, sblocks, SWA_BLOCK, OPERAND_WIDTH),
                     lambda bi, i: (bi, 0, i, 0, 0)),
        pl.BlockSpec((1, MOBA_HEADS, tiles, VALUE_ROWS, Q_TILE), lambda bi, i: (bi, 0, i, 0, 0)),
        pl.BlockSpec((1, DIFF_HEADS, tiles, VALUE_ROWS, Q_TILE), lambda bi, i: (bi, 0, i, 0, 0)),
        pl.BlockSpec((1, SWA_KV_HEADS, sblocks, VALUE_ROWS, SWA_BLOCK),
                     lambda bi, i: (bi, 0, i, 0, 0)),
    )
    return pl.pallas_call(
        _proj_kernel,
        grid=(b, s // rows),
        in_specs=[
            pl.BlockSpec((1, rows, D_MODEL), lambda bi, i: (bi, i, 0)),
            _layer_resident(g, layer),
            _layer_resident(wt, layer),
            pl.BlockSpec((rows, OPERAND_WIDTH), lambda bi, i: (i, 0)),
            pl.BlockSpec((N_Q_HEADS, HEAD_DIM, rows), lambda bi, i: (0, 0, i)),
        ],
        out_specs=out_specs,
        out_shape=out_shape,
        scratch_shapes=[pltpu.VMEM((MOBA_HEADS // 2, s // MOBA_BLOCK, OPERAND_WIDTH), F32)],
        compiler_params=pltpu.CompilerParams(
            dimension_semantics=("arbitrary", "arbitrary"), vmem_limit_bytes=VMEM_LIMIT),
        name="proj",
    )(x, g, wt, augk, augq)


def _take_cols(x, cols):
    return x if cols is None else jnp.concatenate([x[:, lo:hi] for lo, hi in cols], axis=1)


def _put_cols(full, part, cols):
    if cols is None:
        return part
    pieces, pos, off = [], 0, 0
    for lo, hi in cols:
        if lo > pos:
            pieces.append(full[:, pos:lo])
        pieces.append(part[:, off:off + hi - lo])
        off, pos = off + hi - lo, hi
    if pos < full.shape[1]:
        pieces.append(full[:, pos:])
    return jnp.concatenate(pieces, axis=1)


def _flash_unrolled(n_chains, q_of, tiles):
    def scores_of(t):
        keys, _, _, cols = tiles[t]
        return [jnp.dot(keys(c), _take_cols(q_of(c), cols), preferred_element_type=F32)
                for c in range(n_chains)]

    m = [None] * n_chains
    acc = [None] * n_chains
    upcoming = scores_of(0)
    for t, (_, values, mask, cols) in enumerate(tiles):
        scores = upcoming
        if t + 1 < len(tiles):
            upcoming = scores_of(t + 1)
        probs, alphas = [], []
        for c in range(n_chains):
            s = scores[c] if mask is None else jnp.where(mask, scores[c], NEG)
            m_new = jnp.max(s, axis=0, keepdims=True)
            if t > 0:
                m_old = _take_cols(m[c], cols)
                m_new = jnp.maximum(m_old, m_new)
                alphas.append(jnp.exp2(m_old - m_new))
                m[c] = _put_cols(m[c], m_new, cols)
            else:
                m[c] = m_new
            probs.append(jnp.exp2(s - m_new).astype(BF16))
        for c in range(n_chains):
            pv = jnp.dot(values(c), probs[c], preferred_element_type=F32)
            if t > 0:
                pv = _put_cols(acc[c], alphas[c] * _take_cols(acc[c], cols) + pv, cols)
            acc[c] = pv
    return acc


def _query_tile_keys(k_ref, v_ref, i_static, first_mask, second_mask, second_cols):
    def half(j, mask, cols):
        return (lambda c: k_ref[0, c, j], lambda c: v_ref[0, c, j], mask, cols)

    def past(j):
        return (lambda c: jnp.concatenate([k_ref[0, c, 2 * j], k_ref[0, c, 2 * j + 1]], axis=0),
                lambda c: jnp.concatenate([v_ref[0, c, 2 * j], v_ref[0, c, 2 * j + 1]], axis=1),
                None, None)

    return ([half(2 * i_static, first_mask, None), half(2 * i_static + 1, second_mask, second_cols)]
            + [past(j) for j in range(i_static)])


def _normalized(acc):
    return acc[:HEAD_DIM] / acc[HEAD_DIM:HEAD_DIM + 1]


def _attn_params():
    return pltpu.CompilerParams(
        dimension_semantics=("arbitrary", "arbitrary"), vmem_limit_bytes=VMEM_LIMIT)


ATTN_TILE = 2 * Q_TILE


def _causal(rows, cols, period):
    kr = lax.broadcasted_iota(jnp.int32, (rows, cols), 0)
    qc = lax.broadcasted_iota(jnp.int32, (rows, cols), 1)
    for k in range(cols // period - 1, 0, -1):
        qc = jnp.where(qc >= k * period, qc - period, qc)
    return kr <= qc


def _moba_kernel(q_ref, k_ref, v_ref, o_ref):
    nh = q_ref.shape[1]
    for i_static in range(k_ref.shape[2] * Q_TILE // ATTN_TILE):
        @pl.when(pl.program_id(1) == i_static)
        def _(i_static=i_static):
            tiles = _query_tile_keys(
                k_ref, v_ref, i_static, _causal(Q_TILE, ATTN_TILE, ATTN_TILE),
                _causal(Q_TILE, Q_TILE, Q_TILE), [(Q_TILE, ATTN_TILE)])
            acc = _flash_unrolled(nh, lambda c: q_ref[0, c], tiles)
            for h in range(nh):
                o_ref[0, h * HEAD_DIM:(h + 1) * HEAD_DIM, :] = (
                    _normalized(acc[h]).astype(o_ref.dtype))


def _moba_part(qa, ka, va):
    b, nh, _, s = qa.shape
    nt = s // Q_TILE
    in_specs = [
        pl.BlockSpec((1, nh, OPERAND_WIDTH, ATTN_TILE), lambda bi, i: (bi, 0, 0, i)),
        pl.BlockSpec((1, nh, nt, Q_TILE, OPERAND_WIDTH), lambda bi, i: (bi, 0, 0, 0, 0)),
        pl.BlockSpec((1, nh, nt, VALUE_ROWS, Q_TILE), lambda bi, i: (bi, 0, 0, 0, 0)),
    ]
    out_spec = pl.BlockSpec((1, nh * HEAD_DIM, ATTN_TILE), lambda bi, i: (bi, 0, i))
    return (qa, ka, va), in_specs, out_spec, jax.ShapeDtypeStruct((b, nh * HEAD_DIM, s), BF16), []


def _diff_kernel(q_ref, k_ref, v_ref, lq1_ref, lk1_ref, lq2_ref, lk2_ref, linit_ref, g_ref, o_ref,
                 qm_ref):
    nh = q_ref.shape[1]
    row = lax.broadcasted_iota(jnp.int32, q_ref.shape[2:], 0)
    for h in range(nh):
        q = q_ref[0, h]
        zero = jnp.zeros_like(q)
        qm_ref[h, :, :ATTN_TILE] = jnp.where(
            (row < DIFF_QK_DIM) | (row >= 2 * DIFF_QK_DIM), q, zero)
        qm_ref[h, :, ATTN_TILE:] = jnp.where(row >= DIFF_QK_DIM, q, zero)
    lam_init = linit_ref[...]
    lam = (jnp.exp(jnp.sum(lq1_ref[...] * lk1_ref[...], axis=-1, keepdims=True))
           - jnp.exp(jnp.sum(lq2_ref[...] * lk2_ref[...], axis=-1, keepdims=True))
           + lam_init)
    for i_static in range(k_ref.shape[2] * Q_TILE // ATTN_TILE):
        @pl.when(pl.program_id(1) == i_static)
        def _(i_static=i_static):
            tiles = _query_tile_keys(
                k_ref, v_ref, i_static, _causal(Q_TILE, 2 * ATTN_TILE, ATTN_TILE),
                _causal(Q_TILE, ATTN_TILE, Q_TILE),
                [(Q_TILE, ATTN_TILE), (ATTN_TILE + Q_TILE, 2 * ATTN_TILE)])
            acc = _flash_unrolled(nh, lambda c: qm_ref[c], tiles)
            for h in range(nh):
                w = _normalized(acc[h])
                o = w[:, :ATTN_TILE] - lam * w[:, ATTN_TILE:]
                ms = jnp.mean(o * o, axis=0, keepdims=True)
                y = (o * lax.rsqrt(ms + RMS_EPS)) * g_ref[...]
                o_ref[0, h * HEAD_DIM:(h + 1) * HEAD_DIM, :] = (
                    (y * (1.0 - lam_init)).astype(o_ref.dtype))


def _diff_part(qb, kb, vb, lam_q1, lam_k1, lam_q2, lam_k2, lam_init, g_col, layer):
    b, nh, _, s = qb.shape
    nt = s // Q_TILE
    per_layer = lambda arr: pl.BlockSpec(
        (None,) + arr.shape[1:], lambda bi, i: (layer,) + (0,) * (arr.ndim - 1))
    in_specs = [
        pl.BlockSpec((1, nh, OPERAND_WIDTH, ATTN_TILE), lambda bi, i: (bi, 0, 0, i)),
        pl.BlockSpec((1, nh, nt, Q_TILE, OPERAND_WIDTH), lambda bi, i: (bi, 0, 0, 0, 0)),
        pl.BlockSpec((1, nh, nt, VALUE_ROWS, Q_TILE), lambda bi, i: (bi, 0, 0, 0, 0)),
        per_layer(lam_q1), per_layer(lam_k1), per_layer(lam_q2), per_layer(lam_k2),
        per_layer(lam_init), per_layer(g_col),
    ]
    out_spec = pl.BlockSpec((1, nh * HEAD_DIM, ATTN_TILE), lambda bi, i: (bi, 0, i))
    scratch = [pltpu.VMEM((nh, OPERAND_WIDTH, 2 * ATTN_TILE), BF16)]
    return ((qb, kb, vb, lam_q1, lam_k1, lam_q2, lam_k2, lam_init, g_col), in_specs, out_spec,
            jax.ShapeDtypeStruct((b, nh * HEAD_DIM, s), BF16), scratch)


SWA_BLOCKS_PER_STEP = 4


def _swa_kernel(q_ref, k_ref, v_ref, sink_ref, o_ref):
    i = pl.program_id(1)
    kr = lax.broadcasted_iota(jnp.int32, (2 * SWA_BLOCK, SWA_BLOCK), 0)
    qc = lax.broadcasted_iota(jnp.int32, (2 * SWA_BLOCK, SWA_BLOCK), 1)
    dist = qc + SWA_BLOCK - kr
    in_win = (dist >= 0) & (dist < SWA_WINDOW)
    chains = [(b2, kvh) for b2 in range(SWA_BLOCKS_PER_STEP) for kvh in range(SWA_KV_HEADS)]
    blocks, bands = [], []
    for b2 in range(SWA_BLOCKS_PER_STEP):
        n = SWA_BLOCKS_PER_STEP * i + b2
        blocks.append((jnp.maximum(n - 1, 0), n))
        first_row = jnp.where(n > 0, 0, SWA_BLOCK)
        band = jnp.where(in_win & (kr >= first_row), 0.0, NEG)
        bands.append(jnp.concatenate([band] * SWA_GROUP, axis=1))
    scores = []
    for b2, kvh in chains:
        prev, n = blocks[b2]
        cols = slice(b2 * SWA_BLOCK, (b2 + 1) * SWA_BLOCK)
        q = jnp.concatenate([q_ref[0, kvh * SWA_GROUP + g, :, cols]
                             for g in range(SWA_GROUP)], axis=1)
        k = jnp.concatenate([k_ref[0, kvh, prev], k_ref[0, kvh, n]], axis=0)
        scores.append(jnp.dot(k, q, preferred_element_type=F32))
    probs, sink_terms = [], []
    for (b2, kvh), s in zip(chains, scores):
        s = s + bands[b2]
        sink = sink_ref[kvh] * LOG2E
        m = jnp.maximum(jnp.max(s, axis=0, keepdims=True), sink)
        probs.append(jnp.exp2(s - m).astype(BF16))
        sink_terms.append(jnp.exp2(sink - m))
    for (b2, kvh), p, sink_term in zip(chains, probs, sink_terms):
        prev, n = blocks[b2]
        cols = slice(b2 * SWA_BLOCK, (b2 + 1) * SWA_BLOCK)
        v = jnp.concatenate([v_ref[0, kvh, prev], v_ref[0, kvh, n]], axis=1)
        acc = jnp.dot(v, p, preferred_element_type=F32)
        o = acc[:HEAD_DIM] / (acc[HEAD_DIM:HEAD_DIM + 1] + sink_term)
        for g in range(SWA_GROUP):
            r0 = (kvh * SWA_GROUP + g) * HEAD_DIM
            o_ref[0, r0:r0 + HEAD_DIM, cols] = (
                o[:, g * SWA_BLOCK:(g + 1) * SWA_BLOCK].astype(o_ref.dtype))


def _swa_part(qc, kc, vc, sink_rows, layer):
    b, nq, _, s = qc.shape
    nkv = kc.shape[1]
    nsb = s // SWA_BLOCK
    step = SWA_BLOCKS_PER_STEP * SWA_BLOCK
    assert step == ATTN_TILE
    in_specs = [
        pl.BlockSpec((1, nq, OPERAND_WIDTH, step), lambda bi, i: (bi, 0, 0, i)),
        pl.BlockSpec((1, nkv, nsb, SWA_BLOCK, OPERAND_WIDTH), lambda bi, i: (bi, 0, 0, 0, 0)),
        pl.BlockSpec((1, nkv, nsb, VALUE_ROWS, SWA_BLOCK), lambda bi, i: (bi, 0, 0, 0, 0)),
        pl.BlockSpec((None,) + sink_rows.shape[1:], lambda bi, i: (layer, 0, 0, 0)),
    ]
    out_spec = pl.BlockSpec((1, nq * HEAD_DIM, step), lambda bi, i: (bi, 0, i))
    return ((qc, kc, vc, sink_rows), in_specs, out_spec,
            jax.ShapeDtypeStruct((b, nq * HEAD_DIM, s), BF16), [])


def _attention_kernel(*refs, n_in):
    ins = [refs[sum(n_in[:k]):sum(n_in[:k + 1])] for k in range(3)]
    n_all = sum(n_in)
    oa_ref, ob_ref, oc_ref = refs[n_all:n_all + 3]
    qm_ref = refs[n_all + 3]
    _moba_kernel(*ins[0], oa_ref)
    _diff_kernel(*ins[1], ob_ref, qm_ref)
    _swa_kernel(*ins[2], oc_ref)


def _attention_call(moba_part, diff_part, swa_part):
    parts = (moba_part, diff_part, swa_part)
    operands = [a for part in parts for a in part[0]]
    b, _, _, s = moba_part[0][0].shape
    return pl.pallas_call(
        functools.partial(_attention_kernel, n_in=tuple(len(part[0]) for part in parts)),
        grid=(b, s // ATTN_TILE),
        in_specs=[spec for part in parts for spec in part[1]],
        out_specs=tuple(part[2] for part in parts),
        out_shape=tuple(part[3] for part in parts),
        scratch_shapes=[sc for part in parts for sc in part[4]],
        compiler_params=_attn_params(),
        name="attention",
    )(*operands)


def _mix_ffn_kernel(x_ref, ma_ref, mb_ref, mc_ref, woa_ref, wob_ref, woc_ref,
                    g_ref, wg_ref, wu_ref, wd_ref, gf_ref, *refs, final):
    n_conv = len(refs) // 2
    o_ref = refs[n_conv]
    _convert_blocks(refs[:n_conv], refs[n_conv + 1:])
    tdims = (((0,), (0,)), ((), ()))
    for r in range(x_ref.shape[1] // FFN_ROWS):
        rows = slice(r * FFN_ROWS, (r + 1) * FFN_ROWS)
        x = x_ref[0, rows, :]
        x = x + lax.dot_general(ma_ref[0, :, rows], woa_ref[...], tdims, preferred_element_type=F32)
        x = x + lax.dot_general(mb_ref[0, :, rows], wob_ref[...], tdims, preferred_element_type=F32)
        x = x + lax.dot_general(mc_ref[0, :, rows], woc_ref[...], tdims, preferred_element_type=F32)
        x = _swiglu_residual(x, g_ref[...], wg_ref, wu_ref, wd_ref)
        if final:
            x = _rmsnorm(x, gf_ref[...])
        o_ref[0, rows, :] = x


def _mix_ffn_call(x, ma, mb, mc, wo, g, wg, wu, wd, gf, layer, final, later_weights, later_layer):
    b, s, _ = x.shape
    rows = FFN_ROWS
    steps = s // rows
    a_rows, b_rows, c_rows = ma.shape[1], mb.shape[1], mc.shape[1]
    assert a_rows == b_rows and c_rows == a_rows + b_rows
    conv = [_convert_plan(w, later_layer, b * steps, lambda bi, i: bi * steps + i)
            for w in later_weights]
    outs = pl.pallas_call(
        functools.partial(_mix_ffn_kernel, final=final),
        grid=(b, steps),
        in_specs=[
            pl.BlockSpec((1, rows, D_MODEL), lambda bi, i: (bi, i, 0)),
            pl.BlockSpec((1, a_rows, rows), lambda bi, i: (bi, 0, i)),
            pl.BlockSpec((1, b_rows, rows), lambda bi, i: (bi, 0, i)),
            pl.BlockSpec((1, c_rows, rows), lambda bi, i: (bi, 0, i)),
            _layer_resident(wo, layer, a_rows, 0),
            _layer_resident(wo, layer, b_rows, 1),
            _layer_resident(wo, layer, c_rows, 1),
            _layer_resident(g, layer),
            _resident(wg.shape), _resident(wu.shape), _resident(wd.shape),
            _resident((1, D_MODEL)),
        ] + [c[0] for c in conv],
        out_specs=(pl.BlockSpec((1, rows, D_MODEL), lambda bi, i: (bi, i, 0)),)
        + tuple(c[1] for c in conv),
        out_shape=(jax.ShapeDtypeStruct(x.shape, F32),) + tuple(c[2] for c in conv),
        compiler_params=pltpu.CompilerParams(
            dimension_semantics=("arbitrary", "arbitrary"), vmem_limit_bytes=VMEM_LIMIT),
        name="mix_ffn2_final" if final else "mix_ffn2",
    )(x, ma, mb, mc, wo, wo, wo, g, wg, wu, wd, gf, *later_weights)
    return outs[0], outs[1:]


def _split3(x):
    x1 = x.astype(BF16).astype(F32)
    r = x - x1
    x2 = r.astype(BF16).astype(F32)
    x3 = (r - x2).astype(BF16).astype(F32)
    return [x1, x2, x3]


def _bias_rows_kernel(raw_ref, o_ref):
    s = raw_ref.shape[2]
    row = lax.broadcasted_iota(jnp.int32, (16, s), 0)
    lead = AUG_BIAS_ROW - HEAD_DIM
    for hh in range(raw_ref.shape[0]):
        raw = raw_ref[hh]
        pieces = _split3(raw[0:1]) + 2 * _split3(raw[1:2])
        blk = jnp.zeros((16, s), F32)
        for r, piece in enumerate(pieces):
            blk = jnp.where(row == r, piece, blk)
        o_ref[hh] = jnp.concatenate(
            [jnp.zeros((lead, s), F32), blk, jnp.zeros((HEAD_DIM - lead - 16, s), F32)], axis=0)


def _operand_tables(s):
    slopes = _alibi_slopes()
    head_slopes = np.concatenate([
        slopes[SWA_Q_HEADS + DIFF_HEADS:],
        slopes[SWA_Q_HEADS:SWA_Q_HEADS + DIFF_HEADS],
        slopes[:SWA_Q_HEADS],
    ]).astype(np.float32)
    pos = jnp.arange(s, dtype=F32)
    sl = jnp.asarray(head_slopes * np.float32(LOG2E))[:, None]
    raw = jnp.zeros((N_Q_HEADS, 8, s), F32)
    raw = raw.at[:, 0, :].set(-sl * pos[None, :])
    raw = raw.at[:, 1, :].set(jnp.broadcast_to(sl, (N_Q_HEADS, s)))
    augq = pl.pallas_call(
        _bias_rows_kernel,
        out_shape=jax.ShapeDtypeStruct((N_Q_HEADS, HEAD_DIM, s), F32),
        name="bias_rows",
    )(raw)
    blk = jnp.arange(s, dtype=jnp.int32) // MOBA_BLOCK
    onehot = (blk[:, None] == jnp.arange(8, dtype=jnp.int32)[None, :]).astype(F32)
    ones = jnp.ones((s, 3), F32)
    base = jnp.broadcast_to((blk * MOBA_BLOCK).astype(F32)[:, None], (s, 3))
    off = jnp.broadcast_to((jnp.arange(s, dtype=jnp.int32) % MOBA_BLOCK).astype(F32)[:, None], (s, 3))
    augk = jnp.concatenate([
        jnp.zeros((s, HEAD_DIM), F32), onehot, ones, base, off,
        jnp.zeros((s, OPERAND_WIDTH - HEAD_DIM - 17), F32)], axis=1)
    return augq, augk


def kernel(x, norm_ffn1, w1_gate, w1_up, w1_down, norm_mix, w_in, lam_q1, lam_k1, lam_q2, lam_k2,
           diff_subln, sinks, w_out, norm_ffn2, w2_gate, w2_up, w2_down, final_norm):
    b, s, d = x.shape
    depth = w_in.shape[0]
    assert d == D_MODEL and s % ATTN_TILE == 0 and (b * s) % FFN1_ROWS == 0 and s % FFN_ROWS == 0
    augq, augk = _operand_tables(s)
    as_rows = lambda p: p.reshape(depth, 1, -1)
    g1, gm, g2 = as_rows(norm_ffn1), as_rows(norm_mix), as_rows(norm_ffn2)
    ffn1_f32, ffn2_f32 = (w1_gate, w1_up, w1_down), (w2_gate, w2_up, w2_down)
    wt, wo = jnp.swapaxes(w_in.astype(BF16), 1, 2), w_out.astype(BF16)
    lq1, lk1, lq2, lk2 = as_rows(lam_q1), as_rows(lam_k1), as_rows(lam_q2), as_rows(lam_k2)
    lam_init = jnp.asarray([0.8 - 0.6 * math.exp(-0.3 * l) for l in range(depth)],
                           F32).reshape(depth, 1, 1)
    subln = diff_subln.reshape(depth, -1, 1)
    sink_rows = jnp.repeat(sinks.reshape(depth, SWA_KV_HEADS, 1, SWA_GROUP), SWA_BLOCK, axis=3)
    gf = final_norm.reshape(1, d)
    ffn1_w = tuple(w[0].astype(BF16) for w in ffn1_f32)
    for l in range(depth):
        x2, ffn2_w = _ffn_call(x.reshape(b * s, d), g1, *ffn1_w, l, ffn2_f32, l)
        x = x2.reshape(b, s, d)
        qa, qb, qc, ka, kb, kc, va, vb, vc = _proj_call(x, gm, wt, augk, augq, l)
        ma, mb, mc = _attention_call(
            _moba_part(qa, ka, va),
            _diff_part(qb, kb, vb, lq1, lk1, lq2, lk2, lam_init, subln, l),
            _swa_part(qc, kc, vc, sink_rows, l))
        last = l == depth - 1
        x, ffn1_w = _mix_ffn_call(x, ma, mb, mc, wo, g2, *ffn2_w, gf, l, last,
                                  () if last else ffn1_f32, l + 1)
    return x
```

```python
import functools
import math

import numpy as np
import jax
import jax.numpy as jnp
from jax import lax
from jax.experimental import pallas as pl
from jax.experimental.pallas import tpu as pltpu

F32 = jnp.float32
BF16 = jnp.bfloat16

D_MODEL = 1024
D_FF = 2816
HEAD_DIM = 64
MOBA_HEADS = 4
MOBA_BLOCK = 256
MOBA_TOPK = 3
DIFF_HEADS = 4
DIFF_QK_DIM = 32
SWA_Q_HEADS = 8
SWA_KV_HEADS = 2
SWA_GROUP = SWA_Q_HEADS // SWA_KV_HEADS
SWA_WINDOW = 128
SWA_BLOCK = 128
N_ALIBI = MOBA_HEADS + DIFF_HEADS + SWA_Q_HEADS
RMS_EPS = 1e-6
NEG = -1e30

N_Q_HEADS = MOBA_HEADS + DIFF_HEADS + SWA_Q_HEADS
OPERAND_WIDTH = 128
VALUE_ROWS = HEAD_DIM + 16
LOG2E = math.log2(math.e)
AUG_FLAG_ROW = HEAD_DIM
AUG_BIAS_ROW = HEAD_DIM + 8
Q_TILE = 256
FFN_ROWS = 512
FFN1_ROWS = 1024
FFN_CHUNK = 256
VMEM_LIMIT = 56 * 1024 * 1024


def _alibi_slopes():
    n = N_ALIBI
    return 2.0 ** (-8.0 * (np.arange(n, dtype=np.float32) + 1.0) / n)


def _rmsnorm(x, g):
    ms = jnp.mean(x * x, axis=-1, keepdims=True)
    return (x * lax.rsqrt(ms + RMS_EPS)) * g


def _swiglu_residual(x, g, wg_ref, wu_ref, wd_ref):
    h = _rmsnorm(x, g).astype(BF16)
    gate = jnp.dot(h, wg_ref[...], preferred_element_type=F32)
    up = jnp.dot(h, wu_ref[...], preferred_element_type=F32)
    act = (gate * jax.nn.sigmoid(gate) * up).astype(BF16)
    return x + 0.5 * jnp.dot(act, wd_ref[...], preferred_element_type=F32)


def _resident(shape):
    zeros = (0,) * len(shape)
    return pl.BlockSpec(shape, lambda *_: zeros, pipeline_mode=pl.Buffered(1))


def _layer_resident(stacked, layer, rows=None, row_block=0):
    shape = stacked.shape[1:] if rows is None else (rows,) + stacked.shape[2:]
    index = (layer, row_block) + (0,) * (len(shape) - 1)
    return pl.BlockSpec((None,) + shape, lambda *_: index, pipeline_mode=pl.Buffered(1))


def _convert_plan(w, layer, n_steps, step_index):
    rows, cols = w.shape[1:]
    n_blocks = max(d for d in range(1, n_steps + 1) if rows % d == 0 and (rows // d) % 16 == 0)
    block = lambda *ids: jnp.minimum(step_index(*ids), n_blocks - 1)
    in_spec = pl.BlockSpec((None, rows // n_blocks, cols), lambda *ids: (layer, block(*ids), 0))
    out_spec = pl.BlockSpec((rows // n_blocks, cols), lambda *ids: (block(*ids), 0))
    return in_spec, out_spec, jax.ShapeDtypeStruct((rows, cols), BF16)


def _convert_blocks(in_refs, out_refs):
    for src, dst in zip(in_refs, out_refs):
        dst[...] = src[...].astype(BF16)


def _ffn_kernel(x_ref, g_ref, wg_ref, wu_ref, wd_ref, *refs):
    n_conv = len(refs) // 2
    o_ref = refs[n_conv]
    _convert_blocks(refs[:n_conv], refs[n_conv + 1:])
    for r in range(x_ref.shape[0] // FFN_CHUNK):
        rows = slice(r * FFN_CHUNK, (r + 1) * FFN_CHUNK)
        o_ref[rows, :] = _swiglu_residual(x_ref[rows, :], g_ref[...], wg_ref, wu_ref, wd_ref)


def _ffn_call(x2, g, wg, wu, wd, layer, later_weights, later_layer):
    t = x2.shape[0]
    steps = t // FFN1_ROWS
    conv = [_convert_plan(w, later_layer, steps, lambda i: i) for w in later_weights]
    outs = pl.pallas_call(
        _ffn_kernel,
        grid=(steps,),
        in_specs=[
            pl.BlockSpec((FFN1_ROWS, D_MODEL), lambda i: (i, 0)),
            _layer_resident(g, layer),
            _resident(wg.shape), _resident(wu.shape), _resident(wd.shape),
        ] + [c[0] for c in conv],
        out_specs=(pl.BlockSpec((FFN1_ROWS, D_MODEL), lambda i: (i, 0)),) + tuple(c[1] for c in conv),
        out_shape=(jax.ShapeDtypeStruct(x2.shape, F32),) + tuple(c[2] for c in conv),
        compiler_params=pltpu.CompilerParams(
            dimension_semantics=("arbitrary",), vmem_limit_bytes=VMEM_LIMIT),
        name="ffn1",
    )(x2, g, wg, wu, wd, *later_weights)
    return outs[0], outs[1:]


def _split_hi_lo(x):
    hi = x.astype(BF16)
    lo = (x - hi.astype(F32)).astype(BF16)
    return hi, lo


_AQ0 = 0
_AK0 = _AQ0 + MOBA_HEADS * HEAD_DIM
_AV0 = _AK0 + MOBA_HEADS * HEAD_DIM
_BQ0 = _AV0 + MOBA_HEADS * HEAD_DIM
_BK0 = _BQ0 + DIFF_HEADS * HEAD_DIM
_BV0 = _BK0 + DIFF_HEADS * HEAD_DIM
_CQ0 = _BV0 + DIFF_HEADS * HEAD_DIM
_CK0 = _CQ0 + SWA_Q_HEADS * HEAD_DIM
_CV0 = _CK0 + SWA_KV_HEADS * HEAD_DIM
PROJ_ROWS = 4 * Q_TILE


def _proj_kernel(x_ref, g_ref, wt_ref, augk_ref, augq_ref,
                 qa_ref, qb_ref, qc_ref, ka_ref, kb_ref, kc_ref, va_ref, vb_ref, vc_ref,
                 kmean_ref):
    i = pl.program_id(1)
    t = x_ref.shape[1]
    blocks_per_step = t // MOBA_BLOCK

    @pl.when(i == 0)
    def _():
        kmean_ref[...] = jnp.zeros_like(kmean_ref)

    h = _rmsnorm(x_ref[0], g_ref[...]).astype(BF16)
    p = lax.dot_general(wt_ref[...], h, (((1,), (1,)), ((), ())),
                        preferred_element_type=F32)
    augk = augk_ref[...]
    lane = lax.broadcasted_iota(jnp.int32, (t, OPERAND_WIDTH), 1)

    def key_pair(r0):
        kt = p[r0:r0 + 2 * HEAD_DIM].T
        even = jnp.where(lane < HEAD_DIM, kt, augk).astype(BF16)
        odd = jnp.where(lane < HEAD_DIM, pltpu.roll(kt, HEAD_DIM, 1), augk).astype(BF16)
        return kt, (even, odd)

    def store_tiles(ref, hh, slab, rows, axis):
        for n in range(slab.shape[axis] // rows):
            piece = slab[n * rows:(n + 1) * rows] if axis == 0 else slab[:, n * rows:(n + 1) * rows]
            ref[0, hh, n] = piece

    for pair in range(MOBA_HEADS // 2):
        kt, slabs = key_pair(_AK0 + pair * 2 * HEAD_DIM)
        for blk in range(blocks_per_step):
            kmean_ref[pair, pl.ds(blocks_per_step * i + blk, 1), :] = jnp.mean(
                kt[blk * MOBA_BLOCK:(blk + 1) * MOBA_BLOCK], axis=0, keepdims=True)
        for par in range(2):
            store_tiles(ka_ref, 2 * pair + par, slabs[par], Q_TILE, 0)
    for pair in range(DIFF_HEADS // 2):
        _, slabs = key_pair(_BK0 + pair * 2 * HEAD_DIM)
        for par in range(2):
            store_tiles(kb_ref, 2 * pair + par, slabs[par], Q_TILE, 0)
    for pair in range(SWA_KV_HEADS // 2):
        _, slabs = key_pair(_CK0 + pair * 2 * HEAD_DIM)
        for par in range(2):
            store_tiles(kc_ref, 2 * pair + par, slabs[par], SWA_BLOCK, 0)

    ones_rows = jnp.where(
        lax.broadcasted_iota(jnp.int32, (VALUE_ROWS - HEAD_DIM, t), 0) == 0, 1.0, 0.0)

    def v_slab(r0):
        return jnp.concatenate([p[r0:r0 + HEAD_DIM], ones_rows], axis=0).astype(BF16)

    for hh in range(MOBA_HEADS):
        store_tiles(va_ref, hh, v_slab(_AV0 + hh * HEAD_DIM), Q_TILE, 1)
    for hh in range(DIFF_HEADS):
        store_tiles(vb_ref, hh, v_slab(_BV0 + hh * HEAD_DIM), Q_TILE, 1)
    for hh in range(SWA_KV_HEADS):
        store_tiles(vc_ref, hh, v_slab(_CV0 + hh * HEAD_DIM), SWA_BLOCK, 1)

    n_blocks = kmean_ref.shape[1]
    jrow = lax.broadcasted_iota(jnp.int32, (n_blocks, t), 0)
    qcol = lax.broadcasted_iota(jnp.int32, (n_blocks, t), 1)
    qblk = blocks_per_step * i + qcol // MOBA_BLOCK
    zeros_rows = jnp.zeros((HEAD_DIM, t), F32)
    moba_scale = HEAD_DIM ** -0.5 * LOG2E
    for hh in range(MOBA_HEADS):
        qf = p[_AQ0 + hh * HEAD_DIM:_AQ0 + (hh + 1) * HEAD_DIM]
        q_pad = jnp.concatenate([qf, zeros_rows] if hh % 2 == 0 else [zeros_rows, qf], axis=0)
        km_hi, km_lo = _split_hi_lo(kmean_ref[hh // 2])
        q_hi, q_lo = _split_hi_lo(q_pad)
        gate = (jnp.dot(km_hi, q_hi, preferred_element_type=F32)
                + jnp.dot(km_hi, q_lo, preferred_element_type=F32)
                + jnp.dot(km_lo, q_hi, preferred_element_type=F32))
        rank = jnp.zeros((n_blocks, t), F32)
        for m in range(n_blocks):
            gm = gate[m:m + 1, :]
            beats = (gm > gate) | ((gm == gate) & (m < jrow))
            rank = rank + jnp.where(beats & (m < qblk), 1.0, 0.0)
        chosen = ((rank < MOBA_TOPK) & (jrow < qblk)) | (jrow == qblk)
        flags = jnp.where(chosen, 0.0, NEG)
        aug = augq_ref[hh] + jnp.concatenate(
            [flags, jnp.zeros((HEAD_DIM - n_blocks, t), F32)], axis=0)
        qa_ref[0, hh, :HEAD_DIM, :] = (qf * moba_scale).astype(BF16)
        qa_ref[0, hh, HEAD_DIM:, :] = aug.astype(BF16)
    diff_scale = DIFF_QK_DIM ** -0.5 * LOG2E
    for hh in range(DIFF_HEADS):
        qf = p[_BQ0 + hh * HEAD_DIM:_BQ0 + (hh + 1) * HEAD_DIM]
        qb_ref[0, hh, :HEAD_DIM, :] = (qf * diff_scale).astype(BF16)
        qb_ref[0, hh, HEAD_DIM:, :] = augq_ref[MOBA_HEADS + hh].astype(BF16)
    swa_scale = HEAD_DIM ** -0.5 * LOG2E
    for hh in range(SWA_Q_HEADS):
        qf = p[_CQ0 + hh * HEAD_DIM:_CQ0 + (hh + 1) * HEAD_DIM]
        qc_ref[0, hh, :HEAD_DIM, :] = (qf * swa_scale).astype(BF16)
        qc_ref[0, hh, HEAD_DIM:, :] = augq_ref[MOBA_HEADS + DIFF_HEADS + hh].astype(BF16)


def _proj_call(x, g, wt, augk, augq, layer):
    b, s, _ = x.shape
    nt = s // Q_TILE
    nsb = s // SWA_BLOCK
    rows = PROJ_ROWS
    tiles = rows // Q_TILE
    sblocks = rows // SWA_BLOCK
    out_shape = (
        jax.ShapeDtypeStruct((b, MOBA_HEADS, OPERAND_WIDTH, s), BF16),
        jax.ShapeDtypeStruct((b, DIFF_HEADS, OPERAND_WIDTH, s), BF16),
        jax.ShapeDtypeStruct((b, SWA_Q_HEADS, OPERAND_WIDTH, s), BF16),
        jax.ShapeDtypeStruct((b, MOBA_HEADS, nt, Q_TILE, OPERAND_WIDTH), BF16),
        jax.ShapeDtypeStruct((b, DIFF_HEADS, nt, Q_TILE, OPERAND_WIDTH), BF16),
        jax.ShapeDtypeStruct((b, SWA_KV_HEADS, nsb, SWA_BLOCK, OPERAND_WIDTH), BF16),
        jax.ShapeDtypeStruct((b, MOBA_HEADS, nt, VALUE_ROWS, Q_TILE), BF16),
        jax.ShapeDtypeStruct((b, DIFF_HEADS, nt, VALUE_ROWS, Q_TILE), BF16),
        jax.ShapeDtypeStruct((b, SWA_KV_HEADS, nsb, VALUE_ROWS, SWA_BLOCK), BF16),
    )
    out_specs = (
        pl.BlockSpec((1, MOBA_HEADS, OPERAND_WIDTH, rows), lambda bi, i: (bi, 0, 0, i)),
        pl.BlockSpec((1, DIFF_HEADS, OPERAND_WIDTH, rows), lambda bi, i: (bi, 0, 0, i)),
        pl.BlockSpec((1, SWA_Q_HEADS, OPERAND_WIDTH, rows), lambda bi, i: (bi, 0, 0, i)),
        pl.BlockSpec((1, MOBA_HEADS, tiles, Q_TILE, OPERAND_WIDTH), lambda bi, i: (bi, 0, i, 0, 0)),
        pl.BlockSpec((1, DIFF_HEADS, tiles, Q_TILE, OPERAND_WIDTH), lambda bi, i: (bi, 0, i, 0, 0)),
        pl.BlockSpec((1, SWA_KV_HEADS, sblocks, SWA_BLOCK, OPERAND_WIDTH),
                     lambda bi, i: (bi, 0, i, 0, 0)),
        pl.BlockSpec((1, MOBA_HEADS, tiles, VALUE_ROWS, Q_TILE), lambda bi, i: (bi, 0, i, 0, 0)),
        pl.BlockSpec((1, DIFF_HEADS, tiles, VALUE_ROWS, Q_TILE), lambda bi, i: (bi, 0, i, 0, 0)),
        pl.BlockSpec((1, SWA_KV_HEADS, sblocks, VALUE_ROWS, SWA_BLOCK),
                     lambda bi, i: (bi, 0, i, 0, 0)),
    )
    return pl.pallas_call(
        _proj_kernel,
        grid=(b, s // rows),
        in_specs=[
            pl.BlockSpec((1, rows, D_MODEL), lambda bi, i: (bi, i, 0)),
            _layer_resident(g, layer),
            _layer_resident(wt, layer),
            pl.BlockSpec((rows, OPERAND_WIDTH), lambda bi, i: (i, 0)),
            pl.BlockSpec((N_Q_HEADS, HEAD_DIM, rows), lambda bi, i: (0, 0, i)),
        ],
        out_specs=out_specs,
        out_shape=out_shape,
        scratch_shapes=[pltpu.VMEM((MOBA_HEADS // 2, s // MOBA_BLOCK, OPERAND_WIDTH), F32)],
        compiler_params=pltpu.CompilerParams(
            dimension_semantics=("arbitrary", "arbitrary"), vmem_limit_bytes=VMEM_LIMIT),
        name="proj",
    )(x, g, wt, augk, augq)


def _take_cols(x, cols):
    return x if cols is None else jnp.concatenate([x[:, lo:hi] for lo, hi in cols], axis=1)


def _put_cols(full, part, cols):
    if cols is None:
        return part
    pieces, pos, off = [], 0, 0
    for lo, hi in cols:
        if lo > pos:
            pieces.append(full[:, pos:lo])
        pieces.append(part[:, off:off + hi - lo])
        off, pos = off + hi - lo, hi
    if pos < full.shape[1]:
        pieces.append(full[:, pos:])
    return jnp.concatenate(pieces, axis=1)


def _flash_unrolled(n_chains, q_of, tiles):
    def scores_of(t):
        keys, _, _, cols = tiles[t]
        return [jnp.dot(keys(c), _take_cols(q_of(c), cols), preferred_element_type=F32)
                for c in range(n_chains)]

    m = [None] * n_chains
    acc = [None] * n_chains
    upcoming = scores_of(0)
    for t, (_, values, mask, cols) in enumerate(tiles):
        scores = upcoming
        if t + 1 < len(tiles):
            upcoming = scores_of(t + 1)
        probs, alphas = [], []
        for c in range(n_chains):
            s = scores[c] if mask is None else jnp.where(mask, scores[c], NEG)
            m_new = jnp.max(s, axis=0, keepdims=True)
            if t > 0:
                m_old = _take_cols(m[c], cols)
                m_new = jnp.maximum(m_old, m_new)
                alphas.append(jnp.exp2(m_old - m_new))
                m[c] = _put_cols(m[c], m_new, cols)
            else:
                m[c] = m_new
            probs.append(jnp.exp2(s - m_new).astype(BF16))
        for c in range(n_chains):
            pv = jnp.dot(values(c), probs[c], preferred_element_type=F32)
            if t > 0:
                pv = _put_cols(acc[c], alphas[c] * _take_cols(acc[c], cols) + pv, cols)
            acc[c] = pv
    return acc


def _query_tile_keys(k_ref, v_ref, i_static, first_mask, second_mask, second_cols):
    def half(j, mask, cols):
        return (lambda c: k_ref[0, c, j], lambda c: v_ref[0, c, j], mask, cols)

    def past(j):
        return (lambda c: jnp.concatenate([k_ref[0, c, 2 * j], k_ref[0, c, 2 * j + 1]], axis=0),
                lambda c: jnp.concatenate([v_ref[0, c, 2 * j], v_ref[0, c, 2 * j + 1]], axis=1),
                None, None)

    return ([half(2 * i_static, first_mask, None), half(2 * i_static + 1, second_mask, second_cols)]
            + [past(j) for j in range(i_static)])


def _normalized(acc):
    return acc[:HEAD_DIM] / acc[HEAD_DIM:HEAD_DIM + 1]


def _attn_params():
    return pltpu.CompilerParams(
        dimension_semantics=("arbitrary", "arbitrary"), vmem_limit_bytes=VMEM_LIMIT)


ATTN_TILE = 2 * Q_TILE


def _causal(rows, cols, period):
    kr = lax.broadcasted_iota(jnp.int32, (rows, cols), 0)
    qc = lax.broadcasted_iota(jnp.int32, (rows, cols), 1)
    for k in range(cols // period - 1, 0, -1):
        qc = jnp.where(qc >= k * period, qc - period, qc)
    return kr <= qc


def _moba_kernel(q_ref, k_ref, v_ref, o_ref):
    nh = q_ref.shape[1]
    for i_static in range(k_ref.shape[2] * Q_TILE // ATTN_TILE):
        @pl.when(pl.program_id(1) == i_static)
        def _(i_static=i_static):
            tiles = _query_tile_keys(
                k_ref, v_ref, i_static, _causal(Q_TILE, ATTN_TILE, ATTN_TILE),
                _causal(Q_TILE, Q_TILE, Q_TILE), [(Q_TILE, ATTN_TILE)])
            acc = _flash_unrolled(nh, lambda c: q_ref[0, c], tiles)
            for h in range(nh):
                o_ref[0, h * HEAD_DIM:(h + 1) * HEAD_DIM, :] = (
                    _normalized(acc[h]).astype(o_ref.dtype))


def _moba_part(qa, ka, va):
    b, nh, _, s = qa.shape
    nt = s // Q_TILE
    in_specs = [
        pl.BlockSpec((1, nh, OPERAND_WIDTH, ATTN_TILE), lambda bi, i: (bi, 0, 0, i)),
        pl.BlockSpec((1, nh, nt, Q_TILE, OPERAND_WIDTH), lambda bi, i: (bi, 0, 0, 0, 0)),
        pl.BlockSpec((1, nh, nt, VALUE_ROWS, Q_TILE), lambda bi, i: (bi, 0, 0, 0, 0)),
    ]
    out_spec = pl.BlockSpec((1, nh * HEAD_DIM, ATTN_TILE), lambda bi, i: (bi, 0, i))
    return (qa, ka, va), in_specs, out_spec, jax.ShapeDtypeStruct((b, nh * HEAD_DIM, s), BF16), []


def _diff_kernel(q_ref, k_ref, v_ref, lq1_ref, lk1_ref, lq2_ref, lk2_ref, linit_ref, g_ref, o_ref,
                 qm_ref):
    nh = q_ref.shape[1]
    row = lax.broadcasted_iota(jnp.int32, q_ref.shape[2:], 0)
    for h in range(nh):
        q = q_ref[0, h]
        zero = jnp.zeros_like(q)
        qm_ref[h, :, :ATTN_TILE] = jnp.where(
            (row < DIFF_QK_DIM) | (row >= 2 * DIFF_QK_DIM), q, zero)
        qm_ref[h, :, ATTN_TILE:] = jnp.where(row >= DIFF_QK_DIM, q, zero)
    lam_init = linit_ref[...]
    lam = (jnp.exp(jnp.sum(lq1_ref[...] * lk1_ref[...], axis=-1, keepdims=True))
           - jnp.exp(jnp.sum(lq2_ref[...] * lk2_ref[...], axis=-1, keepdims=True))
           + lam_init)
    for i_static in range(k_ref.shape[2] * Q_TILE // ATTN_TILE):
        @pl.when(pl.program_id(1) == i_static)
        def _(i_static=i_static):
            tiles = _query_tile_keys(
                k_ref, v_ref, i_static, _causal(Q_TILE, 2 * ATTN_TILE, ATTN_TILE),
                _causal(Q_TILE, ATTN_TILE, Q_TILE),
                [(Q_TILE, ATTN_TILE), (ATTN_TILE + Q_TILE, 2 * ATTN_TILE)])
            acc = _flash_unrolled(nh, lambda c: qm_ref[c], tiles)
            for h in range(nh):
                w = _normalized(acc[h])
                o = w[:, :ATTN_TILE] - lam * w[:, ATTN_TILE:]
                ms = jnp.mean(o * o, axis=0, keepdims=True)
                y = (o * lax.rsqrt(ms + RMS_EPS)) * g_ref[...]
                o_ref[0, h * HEAD_DIM:(h + 1) * HEAD_DIM, :] = (
                    (y * (1.0 - lam_init)).astype(o_ref.dtype))


def _diff_part(qb, kb, vb, lam_q1, lam_k1, lam_q2, lam_k2, lam_init, g_col, layer):
    b, nh, _, s = qb.shape
    nt = s // Q_TILE
    per_layer = lambda arr: pl.BlockSpec(
        (None,) + arr.shape[1:], lambda bi, i: (layer,) + (0,) * (arr.ndim - 1))
    in_specs = [
        pl.BlockSpec((1, nh, OPERAND_WIDTH, ATTN_TILE), lambda bi, i: (bi, 0, 0, i)),
        pl.BlockSpec((1, nh, nt, Q_TILE, OPERAND_WIDTH), lambda bi, i: (bi, 0, 0, 0, 0)),
        pl.BlockSpec((1, nh, nt, VALUE_ROWS, Q_TILE), lambda bi, i: (bi, 0, 0, 0, 0)),
        per_layer(lam_q1), per_layer(lam_k1), per_layer(lam_q2), per_layer(lam_k2),
        per_layer(lam_init), per_layer(g_col),
    ]
    out_spec = pl.BlockSpec((1, nh * HEAD_DIM, ATTN_TILE), lambda bi, i: (bi, 0, i))
    scratch = [pltpu.VMEM((nh, OPERAND_WIDTH, 2 * ATTN_TILE), BF16)]
    return ((qb, kb, vb, lam_q1, lam_k1, lam_q2, lam_k2, lam_init, g_col), in_specs, out_spec,
            jax.ShapeDtypeStruct((b, nh * HEAD_DIM, s), BF16), scratch)


SWA_BLOCKS_PER_STEP = 4


def _swa_kernel(q_ref, k_ref, v_ref, sink_ref, o_ref):
    i = pl.program_id(1)
    kr = lax.broadcasted_iota(jnp.int32, (2 * SWA_BLOCK, SWA_BLOCK), 0)
    qc = lax.broadcasted_iota(jnp.int32, (2 * SWA_BLOCK, SWA_BLOCK), 1)
    dist = qc + SWA_BLOCK - kr
    in_win = (dist >= 0) & (dist < SWA_WINDOW)
    chains = [(b2, kvh) for b2 in range(SWA_BLOCKS_PER_STEP) for kvh in range(SWA_KV_HEADS)]
    blocks, bands = [], []
    for b2 in range(SWA_BLOCKS_PER_STEP):
        n = SWA_BLOCKS_PER_STEP * i + b2
        blocks.append((jnp.maximum(n - 1, 0), n))
        first_row = jnp.where(n > 0, 0, SWA_BLOCK)
        band = jnp.where(in_win & (kr >= first_row), 0.0, NEG)
        bands.append(jnp.concatenate([band] * SWA_GROUP, axis=1))
    scores = []
    for b2, kvh in chains:
        prev, n = blocks[b2]
        cols = slice(b2 * SWA_BLOCK, (b2 + 1) * SWA_BLOCK)
        q = jnp.concatenate([q_ref[0, kvh * SWA_GROUP + g, :, cols]
                             for g in range(SWA_GROUP)], axis=1)
        k = jnp.concatenate([k_ref[0, kvh, prev], k_ref[0, kvh, n]], axis=0)
        scores.append(jnp.dot(k, q, preferred_element_type=F32))
    probs, sink_terms = [], []
    for (b2, kvh), s in zip(chains, scores):
        s = s + bands[b2]
        sink = sink_ref[kvh] * LOG2E
        m = jnp.maximum(jnp.max(s, axis=0, keepdims=True), sink)
        probs.append(jnp.exp2(s - m).astype(BF16))
        sink_terms.append(jnp.exp2(sink - m))
    for (b2, kvh), p, sink_term in zip(chains, probs, sink_terms):
        prev, n = blocks[b2]
        cols = slice(b2 * SWA_BLOCK, (b2 + 1) * SWA_BLOCK)
        v = jnp.concatenate([v_ref[0, kvh, prev], v_ref[0, kvh, n]], axis=1)
        acc = jnp.dot(v, p, preferred_element_type=F32)
        o = acc[:HEAD_DIM] / (acc[HEAD_DIM:HEAD_DIM + 1] + sink_term)
        for g in range(SWA_GROUP):
            r0 = (kvh * SWA_GROUP + g) * HEAD_DIM
            o_ref[0, r0:r0 + HEAD_DIM, cols] = (
                o[:, g * SWA_BLOCK:(g + 1) * SWA_BLOCK].astype(o_ref.dtype))


def _swa_part(qc, kc, vc, sink_rows, layer):
    b, nq, _, s = qc.shape
    nkv = kc.shape[1]
    nsb = s // SWA_BLOCK
    step = SWA_BLOCKS_PER_STEP * SWA_BLOCK
    assert step == ATTN_TILE
    in_specs = [
        pl.BlockSpec((1, nq, OPERAND_WIDTH, step), lambda bi, i: (bi, 0, 0, i)),
        pl.BlockSpec((1, nkv, nsb, SWA_BLOCK, OPERAND_WIDTH), lambda bi, i: (bi, 0, 0, 0, 0)),
        pl.BlockSpec((1, nkv, nsb, VALUE_ROWS, SWA_BLOCK), lambda bi, i: (bi, 0, 0, 0, 0)),
        pl.BlockSpec((None,) + sink_rows.shape[1:], lambda bi, i: (layer, 0, 0, 0)),
    ]
    out_spec = pl.BlockSpec((1, nq * HEAD_DIM, step), lambda bi, i: (bi, 0, i))
    return ((qc, kc, vc, sink_rows), in_specs, out_spec,
            jax.ShapeDtypeStruct((b, nq * HEAD_DIM, s), BF16), [])


def _attention_kernel(*refs, n_in):
    ins = [refs[sum(n_in[:k]):sum(n_in[:k + 1])] for k in range(3)]
    n_all = sum(n_in)
    oa_ref, ob_ref, oc_ref = refs[n_all:n_all + 3]
    qm_ref = refs[n_all + 3]
    _moba_kernel(*ins[0], oa_ref)
    _diff_kernel(*ins[1], ob_ref, qm_ref)
    _swa_kernel(*ins[2], oc_ref)


def _attention_call(moba_part, diff_part, swa_part):
    parts = (moba_part, diff_part, swa_part)
    operands = [a for part in parts for a in part[0]]
    b, _, _, s = moba_part[0][0].shape
    return pl.pallas_call(
        functools.partial(_attention_kernel, n_in=tuple(len(part[0]) for part in parts)),
        grid=(b, s // ATTN_TILE),
        in_specs=[spec for part in parts for spec in part[1]],
        out_specs=tuple(part[2] for part in parts),
        out_shape=tuple(part[3] for part in parts),
        scratch_shapes=[sc for part in parts for sc in part[4]],
        compiler_params=_attn_params(),
        name="attention",
    )(*operands)


def _mix_ffn_kernel(x_ref, ma_ref, mb_ref, mc_ref, woa_ref, wob_ref, woc_ref,
                    g_ref, wg_ref, wu_ref, wd_ref, gf_ref, *refs, final):
    n_conv = len(refs) // 2
    o_ref = refs[n_conv]
    _convert_blocks(refs[:n_conv], refs[n_conv + 1:])
    tdims = (((0,), (0,)), ((), ()))
    for r in range(x_ref.shape[1] // FFN_ROWS):
        rows = slice(r * FFN_ROWS, (r + 1) * FFN_ROWS)
        x = x_ref[0, rows, :]
        x = x + lax.dot_general(ma_ref[0, :, rows], woa_ref[...], tdims, preferred_element_type=F32)
        x = x + lax.dot_general(mb_ref[0, :, rows], wob_ref[...], tdims, preferred_element_type=F32)
        x = x + lax.dot_general(mc_ref[0, :, rows], woc_ref[...], tdims, preferred_element_type=F32)
        x = _swiglu_residual(x, g_ref[...], wg_ref, wu_ref, wd_ref)
        if final:
            x = _rmsnorm(x, gf_ref[...])
        o_ref[0, rows, :] = x


def _mix_ffn_call(x, ma, mb, mc, wo, g, wg, wu, wd, gf, layer, final, later_weights, later_layer):
    b, s, _ = x.shape
    rows = FFN_ROWS
    steps = s // rows
    a_rows, b_rows, c_rows = ma.shape[1], mb.shape[1], mc.shape[1]
    assert a_rows == b_rows and c_rows == a_rows + b_rows
    conv = [_convert_plan(w, later_layer, b * steps, lambda bi, i: bi * steps + i)
            for w in later_weights]
    outs = pl.pallas_call(
        functools.partial(_mix_ffn_kernel, final=final),
        grid=(b, steps),
        in_specs=[
            pl.BlockSpec((1, rows, D_MODEL), lambda bi, i: (bi, i, 0)),
            pl.BlockSpec((1, a_rows, rows), lambda bi, i: (bi, 0, i)),
            pl.BlockSpec((1, b_rows, rows), lambda bi, i: (bi, 0, i)),
            pl.BlockSpec((1, c_rows, rows), lambda bi, i: (bi, 0, i)),
            _layer_resident(wo, layer, a_rows, 0),
            _layer_resident(wo, layer, b_rows, 1),
            _layer_resident(wo, layer, c_rows, 1),
            _layer_resident(g, layer),
            _resident(wg.shape), _resident(wu.shape), _resident(wd.shape),
            _resident((1, D_MODEL)),
        ] + [c[0] for c in conv],
        out_specs=(pl.BlockSpec((1, rows, D_MODEL), lambda bi, i: (bi, i, 0)),)
        + tuple(c[1] for c in conv),
        out_shape=(jax.ShapeDtypeStruct(x.shape, F32),) + tuple(c[2] for c in conv),
        compiler_params=pltpu.CompilerParams(
            dimension_semantics=("arbitrary", "arbitrary"), vmem_limit_bytes=VMEM_LIMIT),
        name="mix_ffn2_final" if final else "mix_ffn2",
    )(x, ma, mb, mc, wo, wo, wo, g, wg, wu, wd, gf, *later_weights)
    return outs[0], outs[1:]


def _split3(x):
    x1 = x.astype(BF16).astype(F32)
    r = x - x1
    x2 = r.astype(BF16).astype(F32)
    x3 = (r - x2).astype(BF16).astype(F32)
    return [x1, x2, x3]


def _bias_rows_kernel(raw_ref, o_ref):
    s = raw_ref.shape[2]
    row = lax.broadcasted_iota(jnp.int32, (16, s), 0)
    lead = AUG_BIAS_ROW - HEAD_DIM
    for hh in range(raw_ref.shape[0]):
        raw = raw_ref[hh]
        pieces = _split3(raw[0:1]) + 2 * _split3(raw[1:2])
        blk = jnp.zeros((16, s), F32)
        for r, piece in enumerate(pieces):
            blk = jnp.where(row == r, piece, blk)
        o_ref[hh] = jnp.concatenate(
            [jnp.zeros((lead, s), F32), blk, jnp.zeros((HEAD_DIM - lead - 16, s), F32)], axis=0)


def _operand_tables(s):
    slopes = _alibi_slopes()
    head_slopes = np.concatenate([
        slopes[SWA_Q_HEADS + DIFF_HEADS:],
        slopes[SWA_Q_HEADS:SWA_Q_HEADS + DIFF_HEADS],
        slopes[:SWA_Q_HEADS],
    ]).astype(np.float32)
    pos = jnp.arange(s, dtype=F32)
    sl = jnp.asarray(head_slopes * np.float32(LOG2E))[:, None]
    raw = jnp.zeros((N_Q_HEADS, 8, s), F32)
    raw = raw.at[:, 0, :].set(-sl * pos[None, :])
    raw = raw.at[:, 1, :].set(jnp.broadcast_to(sl, (N_Q_HEADS, s)))
    augq = pl.pallas_call(
        _bias_rows_kernel,
        out_shape=jax.ShapeDtypeStruct((N_Q_HEADS, HEAD_DIM, s), F32),
        name="bias_rows",
    )(raw)
    blk = jnp.arange(s, dtype=jnp.int32) // MOBA_BLOCK
    onehot = (blk[:, None] == jnp.arange(8, dtype=jnp.int32)[None, :]).astype(F32)
    ones = jnp.ones((s, 3), F32)
    base = jnp.broadcast_to((blk * MOBA_BLOCK).astype(F32)[:, None], (s, 3))
    off = jnp.broadcast_to((jnp.arange(s, dtype=jnp.int32) % MOBA_BLOCK).astype(F32)[:, None], (s, 3))
    augk = jnp.concatenate([
        jnp.zeros((s, HEAD_DIM), F32), onehot, ones, base, off,
        jnp.zeros((s, OPERAND_WIDTH - HEAD_DIM - 17), F32)], axis=1)
    return augq, augk


def kernel(x, norm_ffn1, w1_gate, w1_up, w1_down, norm_mix, w_in, lam_q1, lam_k1, lam_q2, lam_k2,
           diff_subln, sinks, w_out, norm_ffn2, w2_gate, w2_up, w2_down, final_norm):
    b, s, d = x.shape
    depth = w_in.shape[0]
    assert d == D_MODEL and s % ATTN_TILE == 0 and (b * s) % FFN1_ROWS == 0 and s % FFN_ROWS == 0
    augq, augk = _operand_tables(s)
    as_rows = lambda p: p.reshape(depth, 1, -1)
    g1, gm, g2 = as_rows(norm_ffn1), as_rows(norm_mix), as_rows(norm_ffn2)
    ffn1_f32, ffn2_f32 = (w1_gate, w1_up, w1_down), (w2_gate, w2_up, w2_down)
    wt, wo = jnp.swapaxes(w_in.astype(BF16), 1, 2), w_out.astype(BF16)
    lq1, lk1, lq2, lk2 = as_rows(lam_q1), as_rows(lam_k1), as_rows(lam_q2), as_rows(lam_k2)
    lam_init = jnp.asarray([0.8 - 0.6 * math.exp(-0.3 * l) for l in range(depth)],
                           F32).reshape(depth, 1, 1)
    subln = diff_subln.reshape(depth, -1, 1)
    sink_rows = jnp.repeat(sinks.reshape(depth, SWA_KV_HEADS, 1, SWA_GROUP), SWA_BLOCK, axis=3)
    gf = final_norm.reshape(1, d)
    ffn1_w = tuple(w[0].astype(BF16) for w in ffn1_f32)
    for l in range(depth):
        x2, ffn2_w = _ffn_call(x.reshape(b * s, d), g1, *ffn1_w, l, ffn2_f32, l)
        x = x2.reshape(b, s, d)
        qa, qb, qc, ka, kb, kc, va, vb, vc = _proj_call(x, gm, wt, augk, augq, l)
        ma, mb, mc = _attention_call(
            _moba_part(qa, ka, va),
            _diff_part(qb, kb, vb, lq1, lk1, lq2, lk2, lam_init, subln, l),
            _swa_part(qc, kc, vc, sink_rows, l))
        last = l == depth - 1
        x, ffn1_w = _mix_ffn_call(x, ma, mb, mc, wo, g2, *ffn2_w, gf, l, last,
                                  () if last else ffn1_f32, l + 1)
    return x
```
